```python
import functools
import jax, jax.numpy as jnp
from jax import lax
import numpy as np

D_MODEL = 1024
BATCH = 4
SEQ = 4096
DEPTH = 4
DEC_BATCH = 32
DEC_SEQ = 4
PAST_LEN = 8192
PAGE_SIZE = 128

N_A_LAYERS = DEPTH // 2
N_B_LAYERS = DEPTH - N_A_LAYERS
A_KDIM = 128
A_HEADS = D_MODEL // A_KDIM
A_VDIM = D_MODEL // A_HEADS
A_CHUNK = 64
A_SUB = 16
HEAD_DIM = 128
N_GROUPS = 3
Q_HEADS = D_MODEL // HEAD_DIM
KV_HEADS = 2
WINDOWS = (128, 512, 2048)
DILATIONS = (1, 4, 16)
DIL_KEYS = 128
ROT_DIM = HEAD_DIM // 4
ROPE_THETA = 500000.0
D_FF = 4 * D_MODEL
EPS = 1e-6
NEG = -1e30
EXP_CLAMP = 80.0

kernel_name = 'yoco_hgrn2_dilated_swa_decode_step'


def _rmsnorm(x, gain):
    xf = x.astype(jnp.float32)
    y = xf * lax.rsqrt(jnp.mean(xf * xf, axis=-1, keepdims=True) + EPS)
    return (y * gain.astype(jnp.float32)).astype(x.dtype)


def _head_rmsnorm(x, gain):
    return x * lax.rsqrt(jnp.mean(x * x, axis=-1, keepdims=True) + EPS) * gain.astype(jnp.float32)


def _rotary(x, pos):
    half = ROT_DIM // 2
    inv = ROPE_THETA ** (-2.0 * jnp.arange(half, dtype=jnp.float32) / ROT_DIM)
    ang = pos.astype(jnp.float32)[:, None] * inv[None, :]
    shape = (ang.shape[0],) + (1,) * (x.ndim - 3) + (half,)
    cos, sin = jnp.cos(ang).reshape(shape), jnp.sin(ang).reshape(shape)
    x1, x2 = x[..., :half], x[..., half:ROT_DIM]
    return jnp.concatenate([x1 * cos - x2 * sin, x2 * cos + x1 * sin, x[..., ROT_DIM:]], axis=-1)


def _sqrelu_mlp(x, w_up, w_down):
    return jnp.square(jax.nn.relu(x @ w_up)) @ w_down


def _masked_exp(mask, val):
    return jnp.where(mask, jnp.exp(jnp.where(mask, val, 0.0)), 0.0)


def _gla_chunk(state, inp, sub):
    q, k, v, g = inp
    B, C, H, K = q.shape
    ns = C // sub
    b = jnp.cumsum(g, axis=1)
    o_state = jnp.einsum('bchk,bhkv->bchv', q * jnp.exp(b), state)
    qs = q.reshape(B, ns, sub, H, K)
    ks = k.reshape(B, ns, sub, H, K)
    bs = b.reshape(B, ns, sub, H, K)
    vs = v.reshape(B, ns, sub, H, -1)
    ref = jnp.concatenate([jnp.zeros_like(bs[:, :1, 0]), bs[:, :-1, -1]], axis=1)
    qf = qs * jnp.exp(bs - ref[:, :, None])
    earlier = jnp.arange(C)[None, :] < (jnp.arange(ns) * sub)[:, None]
    kf = k[:, None] * _masked_exp(earlier[None, :, :, None, None], ref[:, :, None] - b[:, None])
    a_prev = jnp.einsum('bithk,bishk->bitsh', qf, kf)
    o_prev = jnp.einsum('bitsh,bshv->bithv', a_prev, v)
    tri = jnp.arange(sub)[:, None] >= jnp.arange(sub)[None, :]
    dec = _masked_exp(tri[None, None, :, :, None, None], bs[:, :, :, None] - bs[:, :, None, :])
    a_diag = jnp.einsum('bithk,biuhk,bituhk->bituh', qs, ks, dec)
    o_diag = jnp.einsum('bituh,biuhv->bithv', a_diag, vs)
    o = o_state + (o_prev + o_diag).reshape(B, C, H, -1)
    b_last = b[:, -1]
    new_state = jnp.exp(b_last)[..., None] * state + jnp.einsum(
        'bchk,bchv->bhkv', k * jnp.exp(b_last[:, None] - b), v)
    return new_state, o


def _hgrn2(xn, w_in, lb, out_gain, w_out, state0, chunk, sub):
    B, T, _ = xn.shape
    q, f, i, og = jnp.split((xn @ w_in).astype(jnp.float32), 4, axis=-1)
    heads = lambda a, d: a.reshape(B, T, A_HEADS, d)
    q = heads(jax.nn.silu(q), A_KDIM)
    log_f = heads(jax.nn.log_sigmoid(f) + jnp.log1p(lb * jnp.exp(jnp.minimum(-f, EXP_CLAMP))), A_KDIM)
    k = heads((1.0 - lb) * jax.nn.sigmoid(-f), A_KDIM)
    v = heads(i, A_VDIM)
    n = T // chunk
    to_chunks = lambda a: a.reshape(B, n, chunk, *a.shape[2:]).swapaxes(0, 1)
    s_fin, o = lax.scan(functools.partial(_gla_chunk, sub=sub), state0.astype(jnp.float32),
                        (to_chunks(q), to_chunks(k), to_chunks(v), to_chunks(log_f)))
    o = o.swapaxes(0, 1).reshape(B, T, A_HEADS, A_VDIM)
    o = _head_rmsnorm(o, out_gain.reshape(A_HEADS, A_VDIM)) * heads(jax.nn.sigmoid(og), A_VDIM)
    return o.reshape(B, T, D_MODEL).astype(xn.dtype) @ w_out, s_fin


def _softmax_av(s, mask, v, eq):
    s = jnp.where(mask, s, NEG)
    m = jnp.max(s, axis=-1, keepdims=True)
    p = jnp.where(mask, jnp.exp(s - m), 0.0)
    den = jnp.sum(p, axis=-1, keepdims=True)
    o = jnp.einsum(eq, p, v) / den
    return o, (m + jnp.log(den))[..., 0]


def _dilated_prompt(q, k, v, dil):
    B, S, Hq, Dh = q.shape
    G = k.shape[2]
    R = Hq // G
    L = S // dil
    P = DIL_KEYS
    nb = -(-L // P)
    by_res = lambda a: a.reshape(B, L, dil, *a.shape[2:]).swapaxes(1, 2)
    qr = jnp.pad(by_res(q), ((0, 0), (0, 0), (0, nb * P - L), (0, 0), (0, 0))).reshape(B, dil, nb, P, G, R, Dh)

    def key_windows(a):
        ap = jnp.pad(by_res(a), ((0, 0), (0, 0), (P, nb * P - L), (0, 0), (0, 0))).reshape(B, dil, nb + 1, P, G, Dh)
        return jnp.concatenate([ap[:, :, :-1], ap[:, :, 1:]], axis=3)

    kw, vw = key_windows(k), key_windows(v)
    s = jnp.einsum('bdnqgrk,bdnsgk->bdngrqs', qr, kw) * HEAD_DIM ** -0.5
    qq = jnp.arange(P)[:, None]
    kk = jnp.arange(2 * P)[None, :]
    band = (kk >= qq) & (kk <= qq + P)
    real = (jnp.arange(nb)[:, None, None] * P + kk[None] - P) >= 0
    mask = (band[None] & real)[None, None, :, None, None]
    o, lse = _softmax_av(s, mask, vw, 'bdngrqs,bdnsgk->bdngrqk')
    o = o.transpose(0, 1, 2, 5, 3, 4, 6).reshape(B, dil, nb * P, Hq, Dh)[:, :, :L]
    lse = lse.transpose(0, 1, 2, 5, 3, 4).reshape(B, dil, nb * P, Hq)[:, :, :L]
    return o.swapaxes(1, 2).reshape(B, S, Hq, Dh), lse.swapaxes(1, 2).reshape(B, S, Hq)


def _dilated_sample(q, k, v, dil, n_past):
    B, T, Hq, Dh = q.shape
    G = k.shape[2]
    R = Hq // G
    idx = n_past + jnp.arange(T)[:, None] - dil * jnp.arange(DIL_KEYS + 1)[None, :]
    valid = idx >= 0
    idx = jnp.maximum(idx, 0)
    kg, vg = k[:, idx], v[:, idx]
    s = jnp.einsum('btgrk,btjgk->btgrj', q.reshape(B, T, G, R, Dh), kg) * HEAD_DIM ** -0.5
    o, lse = _softmax_av(s, valid[None, :, None, None, :], vg, 'btgrj,btjgk->btgrk')
    return o.reshape(B, T, Hq, Dh), lse.reshape(B, T, Hq)


def _trunk(x, pos, hgrn_state0, kv_past, chunk, sub, params):
    (a_norm, a_w_in, a_lb_logits, a_out_norm, a_w_out, kv_norm, w_kv, k_norm,
     b_norm, b_w_q, q_norm, b_w_o, mlp_norm, mlp_w_up, mlp_w_down) = params
    B, T, _ = x.shape
    sm = jax.nn.softmax(a_lb_logits.astype(jnp.float32), axis=0)
    lbs = jnp.cumsum(sm, axis=0) - sm[0]
    finals = []
    kv_new, attn_kv = None, None
    for layer in range(DEPTH):
        if layer < N_A_LAYERS:
            h, s_fin = _hgrn2(_rmsnorm(x, a_norm[layer]), a_w_in[layer], lbs[layer], a_out_norm[layer],
                              a_w_out[layer], hgrn_state0[layer], chunk, sub)
            finals.append(s_fin)
        else:
            if layer == N_A_LAYERS:
                kv = (_rmsnorm(x, kv_norm) @ w_kv).astype(jnp.float32).reshape(B, T, N_GROUPS, 2, KV_HEADS, HEAD_DIM)
                k = _rotary(_head_rmsnorm(kv[:, :, :, 0], k_norm[:, None, :]), pos)
                v = kv[:, :, :, 1]
                kv_new = [jnp.stack([k[:, :, g], v[:, :, g]], axis=2) for g in range(N_GROUPS)]
                if kv_past is None:
                    attn_kv = kv_new
                else:
                    attn_kv = [jnp.concatenate([kv_past[g].astype(jnp.float32), kv_new[g]], axis=1)
                               for g in range(N_GROUPS)]
            j = layer - N_A_LAYERS
            q = (_rmsnorm(x, b_norm[j]) @ b_w_q[j]).astype(jnp.float32).reshape(B, T, N_GROUPS, Q_HEADS, HEAD_DIM)
            q = _rotary(_head_rmsnorm(q, q_norm[j][:, None, :]), pos)
            outs, lses = [], []
            for g in range(N_GROUPS):
                kg, vg = attn_kv[g][:, :, 0], attn_kv[g][:, :, 1]
                if kv_past is None:
                    o, l = _dilated_prompt(q[:, :, g], kg, vg, DILATIONS[g])
                else:
                    o, l = _dilated_sample(q[:, :, g], kg, vg, DILATIONS[g], kv_past[g].shape[1])
                outs.append(o)
                lses.append(l)
            w = jax.nn.softmax(jnp.stack(lses), axis=0)
            o = jnp.sum(w[..., None] * jnp.stack(outs), axis=0).reshape(B, T, Q_HEADS * HEAD_DIM)
            h = o.astype(x.dtype) @ b_w_o[j]
        x = x + h.astype(x.dtype)
        x = x + _sqrelu_mlp(_rmsnorm(x, mlp_norm[layer]), mlp_w_up[layer], mlp_w_down[layer]).astype(x.dtype)
    return x, jnp.stack(finals), kv_new


def setup_inputs(seed: int = 0) -> dict:
    key = jax.random.key(seed)
    ks = jax.random.split(key, 21)

    def nrm(k, shape, scale=1.0):
        return jax.random.normal(k, shape, jnp.float32) * scale

    def gain(k, shape):
        return 1.0 + 0.05 * jax.random.normal(k, shape, jnp.float32)

    kv_row = (2, KV_HEADS, HEAD_DIM)
    return {
        'x_prompt': nrm(ks[0], (BATCH, SEQ, D_MODEL)),
        'x_sample': nrm(ks[1], (DEC_BATCH, DEC_SEQ, D_MODEL)),
        'state_hgrn': nrm(ks[2], (N_A_LAYERS, DEC_BATCH, A_HEADS, A_KDIM, A_VDIM), 0.5),
        'cache_win1_kv': nrm(ks[3], (DEC_BATCH, min(WINDOWS[0], PAST_LEN)) + kv_row),
        'cache_win2_kv': nrm(ks[4], (DEC_BATCH, min(WINDOWS[1], PAST_LEN)) + kv_row),
        'cache_win3_kv': nrm(ks[5], (DEC_BATCH, min(WINDOWS[2], PAST_LEN)) + kv_row),
        'a_norm': gain(ks[6], (N_A_LAYERS, D_MODEL)),
        'a_w_in': nrm(ks[7], (N_A_LAYERS, D_MODEL, 4 * D_MODEL), D_MODEL ** -0.5),
        'a_lb_logits': nrm(ks[8], (N_A_LAYERS, D_MODEL), 0.5),
        'a_out_norm': gain(ks[9], (N_A_LAYERS, D_MODEL)),
        'a_w_out': nrm(ks[10], (N_A_LAYERS, D_MODEL, D_MODEL), D_MODEL ** -0.5),
        'kv_norm': gain(ks[11], (D_MODEL,)),
        'w_kv': nrm(ks[12], (D_MODEL, N_GROUPS * 2 * KV_HEADS * HEAD_DIM), D_MODEL ** -0.5),
        'k_norm': gain(ks[13], (N_GROUPS, HEAD_DIM)),
        'b_norm': gain(ks[14], (N_B_LAYERS, D_MODEL)),
        'b_w_q': nrm(ks[15], (N_B_LAYERS, D_MODEL, N_GROUPS * Q_HEADS * HEAD_DIM), D_MODEL ** -0.5),
        'q_norm': gain(ks[16], (N_B_LAYERS, N_GROUPS, HEAD_DIM)),
        'b_w_o': nrm(ks[17], (N_B_LAYERS, Q_HEADS * HEAD_DIM, D_MODEL), (Q_HEADS * HEAD_DIM) ** -0.5),
        'mlp_norm': gain(ks[18], (DEPTH, D_MODEL)),
        'mlp_w_up': nrm(ks[19], (DEPTH, D_MODEL, D_FF), D_MODEL ** -0.5),
        'mlp_w_down': nrm(ks[20], (DEPTH, D_FF, D_MODEL), D_FF ** -0.5),
    }


def reference(x_prompt, x_sample, state_hgrn, cache_win1_kv, cache_win2_kv, cache_win3_kv,
              a_norm, a_w_in, a_lb_logits, a_out_norm, a_w_out, kv_norm, w_kv, k_norm,
              b_norm, b_w_q, q_norm, b_w_o, mlp_norm, mlp_w_up, mlp_w_down):
    params = (a_norm, a_w_in, a_lb_logits, a_out_norm, a_w_out, kv_norm, w_kv, k_norm,
              b_norm, b_w_q, q_norm, b_w_o, mlp_norm, mlp_w_up, mlp_w_down)
    bp, tp, _ = x_prompt.shape
    ts = x_sample.shape[1]
    zero_state = jnp.zeros((N_A_LAYERS, bp, A_HEADS, A_KDIM, A_VDIM), jnp.float32)
    y_prompt, state_hgrn_prompt, kv_p = _trunk(x_prompt, jnp.arange(tp), zero_state, None,
                                               A_CHUNK, A_SUB, params)
    y_sample, state_hgrn_sample, kv_s = _trunk(x_sample, PAST_LEN + jnp.arange(ts), state_hgrn,
                                               (cache_win1_kv, cache_win2_kv, cache_win3_kv),
                                               ts, ts, params)
    win_p = [kv_p[g][:, max(tp - WINDOWS[g], 0):] for g in range(N_GROUPS)]
    return (y_prompt, y_sample, state_hgrn_prompt, state_hgrn_sample,
            win_p[0], win_p[1], win_p[2], kv_s[0], kv_s[1], kv_s[2])
```

```python
import functools
import math

import jax
import jax.numpy as jnp
import numpy as np
from jax import lax
from jax.experimental import pallas as pl
from jax.experimental.pallas import tpu as pltpu

D_MODEL = 1024
DEPTH = 4
N_A_LAYERS = DEPTH // 2
HEAD_DIM = 128
A_HEADS = D_MODEL // HEAD_DIM
Q_HEADS = D_MODEL // HEAD_DIM
KV_HEADS = 2
Q_PER_KV = Q_HEADS // KV_HEADS
N_GROUPS = 3
WINDOWS = (128, 512, 2048)
DILATIONS = (1, 4, 16)
DIL_KEYS = 128
ROT_DIM = HEAD_DIM // 4
ROT_HALF = ROT_DIM // 2
ROPE_THETA = 500000.0
D_FF = 4 * D_MODEL
EPS = 1e-6
NEG = -1e30
EXP_CLAMP = 80.0
PAST_LEN = 8192

KV_COLS = N_GROUPS * 2 * KV_HEADS * HEAD_DIM
Q_COLS = N_GROUPS * Q_HEADS * HEAD_DIM
SUBLANES = 8
VMEM_LIMIT = 56 * 1024 * 1024

_NT = (((1,), (1,)), ((), ()))
_TN = (((0,), (0,)), ((), ()))


def _params(*sem):
    return pltpu.CompilerParams(dimension_semantics=sem, vmem_limit_bytes=VMEM_LIMIT)


def _bf(x):
    return x.astype(jnp.bfloat16)


def _rms_rows(x, gain):
    return x * lax.rsqrt(jnp.mean(x * x, axis=-1, keepdims=True) + EPS) * gain


def _norm_matmul_kernel(x_ref, g_ref, w_ref, o_ref, xn_ref):
    @pl.when(pl.program_id(1) == 0)
    def _():
        xn_ref[...] = _bf(_rms_rows(x_ref[...], g_ref[...]))

    o_ref[...] = jnp.dot(xn_ref[...], w_ref[...], preferred_element_type=jnp.float32)


def _norm_matmul(x, gain, w, tm, tn):
    n, d = x.shape
    m = w.shape[1]
    return pl.pallas_call(
        _norm_matmul_kernel,
        grid=(n // tm, m // tn),
        in_specs=[pl.BlockSpec((tm, d), lambda i, j: (i, 0)),
                  pl.BlockSpec((1, d), lambda i, j: (0, 0)),
                  pl.BlockSpec((d, tn), lambda i, j: (0, j))],
        out_specs=pl.BlockSpec((tm, tn), lambda i, j: (i, j)),
        out_shape=jax.ShapeDtypeStruct((n, m), jnp.float32),
        scratch_shapes=[pltpu.VMEM((tm, d), jnp.bfloat16)],
        compiler_params=_params("parallel", "arbitrary"),
    )(x, gain.reshape(1, d), w)


def _proj_heads_kernel(x_ref, g_ref, w_ref, hg_ref, cos_ref, sin_fwd_ref, sin_bwd_ref, *out_refs, normed):
    xn = _bf(_rms_rows(x_ref[...], g_ref[...]))
    z = jnp.dot(xn, w_ref[...], preferred_element_type=jnp.float32)
    cos, sin_fwd, sin_bwd = cos_ref[...], sin_fwd_ref[...], sin_bwd_ref[...]
    for h, is_normed in enumerate(normed):
        zh = z[:, h * HEAD_DIM:(h + 1) * HEAD_DIM]
        if is_normed:
            zh = _rms_rows(zh, hg_ref[h:h + 1, :])
            zh = (zh * cos + pltpu.roll(zh, ROT_HALF, 1) * sin_fwd
                  + pltpu.roll(zh, HEAD_DIM - ROT_HALF, 1) * sin_bwd)
        for o_ref in out_refs:
            o_ref[:, h * HEAD_DIM:(h + 1) * HEAD_DIM] = zh.astype(o_ref.dtype)


def _proj_heads(x, gain, w, head_gain, normed, rope, rope_period_blocks, tm, out_dtypes):
    n, d = x.shape
    m = w.shape[1]
    rope_spec = pl.BlockSpec((tm, HEAD_DIM), lambda i: (i % rope_period_blocks, 0))
    return pl.pallas_call(
        functools.partial(_proj_heads_kernel, normed=normed),
        grid=(n // tm,),
        in_specs=[pl.BlockSpec((tm, d), lambda i: (i, 0)),
                  pl.BlockSpec((1, d), lambda i: (0, 0)),
                  pl.BlockSpec((d, m), lambda i: (0, 0)),
                  pl.BlockSpec(head_gain.shape, lambda i: (0, 0)),
                  rope_spec, rope_spec, rope_spec],
        out_specs=[pl.BlockSpec((tm, m), lambda i: (i, 0)) for _ in out_dtypes],
        out_shape=[jax.ShapeDtypeStruct((n, m), dt) for dt in out_dtypes],
        compiler_params=_params("parallel"),
    )(x, gain.reshape(1, d), w, head_gain, *rope)


def _gla_kernel(zq_ref, zf_ref, zi_ref, zo_ref, lb_ref, og_ref, s0_ref, a_ref, sfin_ref, st_ref,
                *, chunk, heads, t_valid):
    c = pl.program_id(2)
    n_levels = int(math.log2(chunk))
    row = lax.broadcasted_iota(jnp.int32, (chunk, HEAD_DIM), 0)
    row_sq = lax.broadcasted_iota(jnp.int32, (chunk, chunk), 0)
    col_sq = lax.broadcasted_iota(jnp.int32, (chunk, chunk), 1)

    for h in range(heads):
        @pl.when(c == 0)
        def _():
            st_ref[h] = s0_ref[0, h].T

        cols = slice(h * HEAD_DIM, (h + 1) * HEAD_DIM)
        zq, zf, v = zq_ref[0, :, cols], zf_ref[0, :, cols], zi_ref[0, :, cols]
        lb = lb_ref[:, cols]
        q = zq / (1.0 + jnp.exp(-zq))
        log_f = (jnp.minimum(zf, 0.0) - jnp.log1p(jnp.exp(-jnp.abs(zf)))
                 + jnp.log1p(lb * jnp.exp(jnp.minimum(-zf, EXP_CLAMP))))
        k = (1.0 - lb) / (1.0 + jnp.exp(zf))
        if t_valid < chunk:
            live = row < t_valid
            log_f = jnp.where(live, log_f, 0.0)
            k = jnp.where(live, k, 0.0)

        scores = jnp.where(row_sq == col_sq,
                           lax.dot_general(_bf(q), _bf(k), _NT, preferred_element_type=jnp.float32), 0.0)
        pre, tot = log_f, log_f
        for lvl in range(1, n_levels + 1):
            half = 1 << (lvl - 1)
            upper = (row & half) != 0
            e = jnp.exp(jnp.where(upper, pre, tot - pre))
            q_l = jnp.where(upper, q * e, 0.0)
            k_l = jnp.where(upper, 0.0, k * e)
            s_l = lax.dot_general(_bf(q_l), _bf(k_l), _NT, preferred_element_type=jnp.float32)
            scores = scores + jnp.where((row_sq >> lvl) == (col_sq >> lvl), s_l, 0.0)
            tot_sib = jnp.where(upper, pltpu.roll(tot, half, 0), pltpu.roll(tot, chunk - half, 0))
            pre = pre + jnp.where(upper, tot_sib, 0.0)
            tot = tot + tot_sib

        st = st_ref[h]
        o = lax.dot_general(_bf(q * jnp.exp(pre)), _bf(st), _NT, preferred_element_type=jnp.float32)
        o = o + jnp.dot(_bf(scores), _bf(v), preferred_element_type=jnp.float32)
        k_end = k * jnp.exp(tot - pre)
        st_new = st * jnp.exp(tot[0:1, :]) + lax.dot_general(
            _bf(v), _bf(k_end), _TN, preferred_element_type=jnp.float32)
        st_ref[h] = st_new

        zo = zo_ref[0, :, cols]
        a = _rms_rows(o, og_ref[:, cols]) / (1.0 + jnp.exp(-zo))
        a_ref[0, :, cols] = a.astype(a_ref.dtype)

        @pl.when(c == pl.num_programs(2) - 1)
        def _():
            sfin_ref[0, h] = st_new.T


def _gla(z, lb, out_gain, state0, chunk, heads, t_valid):
    b, t, _ = z.shape
    hw = heads * HEAD_DIM
    nh = A_HEADS // heads
    zspec = lambda part: pl.BlockSpec((1, chunk, hw), lambda bi, hi, ci: (bi, ci, part * nh + hi))
    vec_spec = pl.BlockSpec((1, hw), lambda bi, hi, ci: (0, hi))
    st_spec = pl.BlockSpec((1, heads, HEAD_DIM, HEAD_DIM), lambda bi, hi, ci: (bi, hi, 0, 0))
    return pl.pallas_call(
        functools.partial(_gla_kernel, chunk=chunk, heads=heads, t_valid=t_valid),
        grid=(b, nh, t // chunk),
        in_specs=[zspec(0), zspec(1), zspec(2), zspec(3), vec_spec, vec_spec, st_spec],
        out_specs=[pl.BlockSpec((1, chunk, hw), lambda bi, hi, ci: (bi, ci, hi)), st_spec],
        out_shape=[jax.ShapeDtypeStruct((b, t, D_MODEL), jnp.bfloat16),
                   jax.ShapeDtypeStruct(state0.shape, jnp.float32)],
        scratch_shapes=[pltpu.VMEM((heads, HEAD_DIM, HEAD_DIM), jnp.float32)],
        compiler_params=_params("parallel", "parallel", "arbitrary"),
    )(z, z, z, z, lb.reshape(1, D_MODEL), out_gain.reshape(1, D_MODEL), state0)


def _post_mlp_kernel(*refs, merge):
    if merge:
        x_ref, o_ref, lse_ref, wp_ref, g_ref, wu_ref, wd_ref, out_ref, x1_ref, xn_ref, acc_ref = refs
    else:
        x_ref, a_ref, wp_ref, g_ref, wu_ref, wd_ref, out_ref, x1_ref, xn_ref, acc_ref = refs
    j = pl.program_id(1)

    @pl.when(j == 0)
    def _():
        if merge:
            lse = lse_ref[...]
            w = jnp.exp(lse - jnp.max(lse, axis=0, keepdims=True))
            w = w / jnp.sum(w, axis=0, keepdims=True)
            parts = []
            for h in range(Q_HEADS):
                cols = slice(h * HEAD_DIM, (h + 1) * HEAD_DIM)
                parts.append(sum(w[g, :, h:h + 1] * o_ref[g, :, cols] for g in range(N_GROUPS)))
            a = _bf(jnp.concatenate(parts, axis=1))
        else:
            a = a_ref[...]
        x1 = x_ref[...] + jnp.dot(a, wp_ref[...], preferred_element_type=jnp.float32)
        x1_ref[...] = x1
        xn_ref[...] = _bf(_rms_rows(x1, g_ref[...]))
        acc_ref[...] = jnp.zeros_like(acc_ref)

    hdn = jnp.maximum(jnp.dot(xn_ref[...], wu_ref[...], preferred_element_type=jnp.float32), 0.0)
    acc_ref[...] += jnp.dot(_bf(hdn * hdn), wd_ref[...], preferred_element_type=jnp.float32)

    @pl.when(j == pl.num_programs(1) - 1)
    def _():
        out_ref[...] = x1_ref[...] + acc_ref[...]


def _post_mlp(x, mixer_in, wp, gain, wu, wd, tm, tf):
    n, d = x.shape
    f = wu.shape[1]
    merge = isinstance(mixer_in, tuple)
    row_spec = pl.BlockSpec((tm, d), lambda i, j: (i, 0))
    if merge:
        o3, lse3 = mixer_in
        mixer_specs = [pl.BlockSpec((N_GROUPS, tm, d), lambda i, j: (0, i, 0)),
                       pl.BlockSpec((N_GROUPS, tm, HEAD_DIM), lambda i, j: (0, i, 0))]
        mixer_args = (o3, lse3)
    else:
        mixer_specs = [row_spec]
        mixer_args = (mixer_in,)
    return pl.pallas_call(
        functools.partial(_post_mlp_kernel, merge=merge),
        grid=(n // tm, f // tf),
        in_specs=[row_spec, *mixer_specs,
                  pl.BlockSpec((d, d), lambda i, j: (0, 0)),
                  pl.BlockSpec((1, d), lambda i, j: (0, 0)),
                  pl.BlockSpec((d, tf), lambda i, j: (0, j)),
                  pl.BlockSpec((tf, d), lambda i, j: (j, 0))],
        out_specs=row_spec,
        out_shape=jax.ShapeDtypeStruct((n, d), jnp.float32),
        scratch_shapes=[pltpu.VMEM((tm, d), jnp.float32), pltpu.VMEM((tm, d), jnp.bfloat16),
                        pltpu.VMEM((tm, d), jnp.float32)],
        compiler_params=_params("parallel", "arbitrary"),
    )(x, *mixer_args, wp, gain.reshape(1, d), wu, wd)


def _attn_prompt_kernel(q_ref, kp_ref, kc_ref, vp_ref, vc_ref, o_ref, lse_ref):
    p = DIL_KEYS
    has_prev = pl.program_id(2) > 0
    qq = lax.broadcasted_iota(jnp.int32, (p, 2 * p), 0)
    kk = lax.broadcasted_iota(jnp.int32, (p, 2 * p), 1)
    band = (kk >= qq) & (kk <= qq + p) & ((kk >= p) | has_prev)
    mask = jnp.concatenate([band] * Q_PER_KV, axis=0)
    lane = lax.broadcasted_iota(jnp.int32, (p, HEAD_DIM), 1)
    lse_all = jnp.zeros((p, HEAD_DIM), jnp.float32)
    for kvh in range(KV_HEADS):
        kcols = slice(kvh * HEAD_DIM, (kvh + 1) * HEAD_DIM)
        keys = jnp.concatenate([kp_ref[0, :, kcols], kc_ref[0, :, kcols]], axis=0)
        vals = jnp.concatenate([vp_ref[0, :, kcols], vc_ref[0, :, kcols]], axis=0)
        heads = [kvh * Q_PER_KV + r for r in range(Q_PER_KV)]
        q = jnp.concatenate([q_ref[0, :, h * HEAD_DIM:(h + 1) * HEAD_DIM] for h in heads], axis=0)
        s = lax.dot_general(q, keys, _NT, preferred_element_type=jnp.float32) * HEAD_DIM ** -0.5
        s = jnp.where(mask, s, NEG)
        m = jnp.max(s, axis=-1, keepdims=True)
        pr = jnp.where(mask, jnp.exp(s - m), 0.0)
        den = jnp.sum(pr, axis=-1, keepdims=True)
        o = jnp.dot(_bf(pr), vals, preferred_element_type=jnp.float32) / den
        lse = m + jnp.log(den)
        for r, h in enumerate(heads):
            o_ref[0, :, h * HEAD_DIM:(h + 1) * HEAD_DIM] = o[r * p:(r + 1) * p].astype(o_ref.dtype)
            lse_all = jnp.where(lane == h, lse[r * p:(r + 1) * p], lse_all)
    lse_ref[0] = lse_all


def _attn_prompt(q, kv, group, b, s):
    dil = DILATIONS[group]
    l = s // dil
    nb = l // DIL_KEYS
    qv = q.reshape(b, l, dil * Q_COLS)
    kvv = kv.reshape(b, l, dil * KV_COLS)
    qpb = Q_COLS // D_MODEL
    kpb = KV_COLS // (KV_HEADS * HEAD_DIM)
    kw = KV_HEADS * HEAD_DIM

    def kv_spec(is_v, prev):
        def idx(bi, r, n):
            return (bi, jnp.maximum(n - 1, 0) if prev else n, r * kpb + 2 * group + is_v)
        return pl.BlockSpec((1, DIL_KEYS, kw), idx)

    o, lse = pl.pallas_call(
        _attn_prompt_kernel,
        grid=(b, dil, nb),
        in_specs=[pl.BlockSpec((1, DIL_KEYS, D_MODEL), lambda bi, r, n: (bi, n, r * qpb + group)),
                  kv_spec(0, True), kv_spec(0, False), kv_spec(1, True), kv_spec(1, False)],
        out_specs=[pl.BlockSpec((1, DIL_KEYS, D_MODEL), lambda bi, r, n: (bi, n, r)),
                   pl.BlockSpec((1, DIL_KEYS, HEAD_DIM), lambda bi, r, n: (bi, n, r))],
        out_shape=[jax.ShapeDtypeStruct((b, l, dil * D_MODEL), jnp.float32),
                   jax.ShapeDtypeStruct((b, l, dil * HEAD_DIM), jnp.float32)],
        compiler_params=_params("parallel", "parallel", "arbitrary"),
    )(qv, kvv, kvv, kvv, kvv)
    return o.reshape(b * s, D_MODEL), lse.reshape(b * s, HEAD_DIM)


def _sample_key_blocks(n_new):
    plans, tables = [], []
    for g, (win, dil) in enumerate(zip(WINDOWS, DILATIONS)):
        n_past = min(win, PAST_LEN)
        stride = n_past // DIL_KEYS
        n_res = min(stride, n_new) if dil > 1 else stride
        blocks = []
        for r in range(n_res):
            pos = stride * np.arange(DIL_KEYS) + r
            blocks.append((("cache", r), pos))
        blocks.append((("new", 0), np.where(np.arange(DIL_KEYS) < SUBLANES, n_past + np.arange(DIL_KEYS), -10 ** 9)))
        for (src, pos) in blocks:
            t = np.arange(SUBLANES)[:, None]
            delta = n_past + t - pos[None, :]
            ok = (delta >= 0) & (delta % dil == 0) & (delta // dil <= DIL_KEYS)
            ok = np.where(t < n_new, ok, True)
            tables.append(ok)
        plans.append([src for src, _ in blocks])
    return plans, np.stack(tables).astype(np.float32)


def _attn_sample_kernel(q_ref, kvn_ref, c1_ref, c2_ref, c3_ref, ok_ref, a_ref, *, plans):
    rec = 2 * KV_HEADS * HEAD_DIM
    caches = (c1_ref, c2_ref, c3_ref)
    pad = jnp.zeros((DIL_KEYS - SUBLANES, HEAD_DIM), jnp.float32)
    for kvh in range(KV_HEADS):
        heads = [kvh * Q_PER_KV + r for r in range(Q_PER_KV)]
        outs, lses = [], []
        blk = 0
        for g in range(N_GROUPS):
            q = jnp.concatenate(
                [q_ref[0, :, (g * Q_HEADS + h) * HEAD_DIM:(g * Q_HEADS + h + 1) * HEAD_DIM] for h in heads], axis=0)
            scores, values, oks = [], [], []
            for (src, r) in plans[g]:
                if src == "cache":
                    base = r * rec + kvh * HEAD_DIM
                    keys = _bf(caches[g][0, :, base:base + HEAD_DIM])
                    vals = _bf(caches[g][0, :, base + KV_HEADS * HEAD_DIM:base + (KV_HEADS + 1) * HEAD_DIM])
                else:
                    base = g * rec + kvh * HEAD_DIM
                    keys = _bf(jnp.concatenate([kvn_ref[0, :, base:base + HEAD_DIM], pad], axis=0))
                    vals = _bf(jnp.concatenate(
                        [kvn_ref[0, :, base + KV_HEADS * HEAD_DIM:base + (KV_HEADS + 1) * HEAD_DIM], pad], axis=0))
                ok = jnp.concatenate([ok_ref[blk]] * Q_PER_KV, axis=0) > 0.5
                blk += 1
                s = lax.dot_general(q, keys, _NT, preferred_element_type=jnp.float32) * HEAD_DIM ** -0.5
                scores.append(jnp.where(ok, s, NEG))
                values.append(vals)
                oks.append(ok)
            m = functools.reduce(jnp.maximum, [jnp.max(s, axis=-1, keepdims=True) for s in scores])
            probs = [jnp.where(ok, jnp.exp(s - m), 0.0) for s, ok in zip(scores, oks)]
            den = sum(jnp.sum(pr, axis=-1, keepdims=True) for pr in probs)
            o = sum(jnp.dot(_bf(pr), vals, preferred_element_type=jnp.float32) for pr, vals in zip(probs, values))
            outs.append(o / den)
            lses.append(m + jnp.log(den))
        top = functools.reduce(jnp.maximum, lses)
        w = [jnp.exp(l - top) for l in lses]
        wsum = sum(w)
        merged = sum((wg / wsum) * og for wg, og in zip(w, outs))
        for r, h in enumerate(heads):
            a_ref[0, :, h * HEAD_DIM:(h + 1) * HEAD_DIM] = merged[r * SUBLANES:(r + 1) * SUBLANES].astype(a_ref.dtype)


def _attn_sample(q, kv_new, caches, n_new):
    b = q.shape[0]
    rec = 2 * KV_HEADS * HEAD_DIM
    plans, table = _sample_key_blocks(n_new)
    cache_views, cache_specs = [], []
    for g, cache in enumerate(caches):
        n_past = cache.shape[1]
        stride = n_past // DIL_KEYS
        n_res = sum(1 for src, _ in plans[g] if src == "cache")
        cache_views.append(cache.reshape(b, DIL_KEYS, stride * rec))
        cache_specs.append(pl.BlockSpec((1, DIL_KEYS, n_res * rec), lambda bi: (bi, 0, 0)))
    return pl.pallas_call(
        functools.partial(_attn_sample_kernel, plans=plans),
        grid=(b,),
        in_specs=[pl.BlockSpec((1, SUBLANES, Q_COLS), lambda bi: (bi, 0, 0)),
                  pl.BlockSpec((1, SUBLANES, KV_COLS), lambda bi: (bi, 0, 0)),
                  *cache_specs,
                  pl.BlockSpec(table.shape, lambda bi: (0, 0, 0))],
        out_specs=pl.BlockSpec((1, SUBLANES, D_MODEL), lambda bi: (bi, 0, 0)),
        out_shape=jax.ShapeDtypeStruct((b, SUBLANES, D_MODEL), jnp.bfloat16),
        compiler_params=_params("parallel"),
    )(q, kv_new, *cache_views, jnp.asarray(table))


def _rope_tables(pos):
    inv = ROPE_THETA ** (-2.0 * jnp.arange(ROT_HALF, dtype=jnp.float32) / ROT_DIM)
    ang = pos.astype(jnp.float32)[:, None] * inv[None, :]
    cos, sin = jnp.cos(ang), jnp.sin(ang)
    rest = HEAD_DIM - ROT_DIM
    one, zero = jnp.ones((pos.shape[0], rest), jnp.float32), jnp.zeros((pos.shape[0], rest), jnp.float32)
    zh = jnp.zeros_like(sin)
    return (jnp.concatenate([cos, cos, one], axis=1),
            jnp.concatenate([zh, sin, zero], axis=1),
            jnp.concatenate([-sin, zh, zero], axis=1))


def _trunk(x, b, t, pos, state0, caches, n_new, weights, lbs, chunk, gla_heads, tm, tm_proj):
    (a_norm, a_w_in, a_out_norm, a_w_out, kv_norm, w_kv, k_norm,
     b_norm, b_w_q, q_norm, b_w_o, mlp_norm, mlp_w_up, mlp_w_down) = weights
    n = b * t
    tf = 512
    rope = _rope_tables(pos)
    if t % tm_proj == 0:
        rope_period = t // tm_proj
    else:
        rope = tuple(jnp.tile(r, (tm_proj // t, 1)) for r in rope)
        rope_period = 1
    finals = []
    kv32 = kv16 = None
    for layer in range(DEPTH):
        if layer < N_A_LAYERS:
            z = _norm_matmul(x, a_norm[layer], a_w_in[layer], tm, 1024)
            a, s_fin = _gla(z.reshape(b, t, 4 * D_MODEL), lbs[layer], a_out_norm[layer], state0[layer],
                            chunk, gla_heads, n_new)
            finals.append(s_fin)
            mixer_in, wp = a.reshape(n, D_MODEL), a_w_out[layer]
        else:
            j = layer - N_A_LAYERS
            if j == 0:
                kv_gain = jnp.repeat(jnp.repeat(k_norm, KV_HEADS, axis=0), 2, axis=0)
                normed = tuple(bool((hh // KV_HEADS) % 2 == 0) for hh in range(KV_COLS // HEAD_DIM))
                kv32, kv16 = _proj_heads(x, kv_norm, w_kv, kv_gain, normed, rope, rope_period, tm_proj,
                                         (jnp.float32, jnp.bfloat16))
            q_gain = jnp.repeat(q_norm[j], Q_HEADS, axis=0)
            (q,) = _proj_heads(x, b_norm[j], b_w_q[j], q_gain, (True,) * (Q_COLS // HEAD_DIM), rope, rope_period,
                               tm_proj, (jnp.bfloat16,))
            if caches is None:
                res = [_attn_prompt(q, kv16, g, b, t) for g in range(N_GROUPS)]
                mixer_in = (jnp.stack([r[0] for r in res]), jnp.stack([r[1] for r in res]))
            else:
                a = _attn_sample(q.reshape(b, t, Q_COLS), kv32.reshape(b, t, KV_COLS), caches, n_new)
                mixer_in = a.reshape(n, D_MODEL)
            wp = b_w_o[j]
        x = _post_mlp(x, mixer_in, wp, mlp_norm[layer], mlp_w_up[layer], mlp_w_down[layer], tm, tf)
    return x, jnp.stack(finals), kv32


def kernel(x_prompt, x_sample, state_hgrn, cache_win1_kv, cache_win2_kv, cache_win3_kv, a_norm, a_w_in, a_lb_logits, a_out_norm, a_w_out, kv_norm, w_kv, k_norm, b_norm, b_w_q, q_norm, b_w_o, mlp_norm, mlp_w_up, mlp_w_down):
    bp, tp, d = x_prompt.shape
    bs, ts, _ = x_sample.shape
    sm = jax.nn.softmax(a_lb_logits.astype(jnp.float32), axis=0)
    lbs = jnp.cumsum(sm, axis=0) - sm[0]
    weights = (a_norm, _bf(a_w_in), a_out_norm, _bf(a_w_out), kv_norm, _bf(w_kv), k_norm,
               b_norm, _bf(b_w_q), q_norm, _bf(b_w_o), mlp_norm, _bf(mlp_w_up), _bf(mlp_w_down))

    zero_state = jnp.zeros((N_A_LAYERS, bp, A_HEADS, HEAD_DIM, HEAD_DIM), jnp.float32)
    y_p, st_p, kv_p = _trunk(x_prompt.reshape(bp * tp, d), bp, tp, jnp.arange(tp), zero_state, None, 128,
                             weights, lbs, chunk=128, gla_heads=2, tm=512, tm_proj=256)

    xs = jnp.pad(x_sample, ((0, 0), (0, SUBLANES - ts), (0, 0))).reshape(bs * SUBLANES, d)
    caches = tuple(c.reshape(bs, c.shape[1], 2 * KV_HEADS * HEAD_DIM)
                   for c in (cache_win1_kv, cache_win2_kv, cache_win3_kv))
    y_s, st_s, kv_s = _trunk(xs, bs, SUBLANES, PAST_LEN + jnp.arange(SUBLANES), state_hgrn, caches, ts,
                             weights, lbs, chunk=SUBLANES, gla_heads=A_HEADS, tm=bs * SUBLANES,
                             tm_proj=bs * SUBLANES)

    kv_p = kv_p.reshape(bp, tp, N_GROUPS, 2, KV_HEADS, HEAD_DIM)
    kv_s = kv_s.reshape(bs, SUBLANES, N_GROUPS, 2, KV_HEADS, HEAD_DIM)[:, :ts]
    win_p = [kv_p[:, max(tp - WINDOWS[g], 0):, g] for g in range(N_GROUPS)]
    win_s = [kv_s[:, :, g] for g in range(N_GROUPS)]
    return (y_p.reshape(bp, tp, d), y_s.reshape(bs, SUBLANES, d)[:, :ts], st_p, st_s,
            win_p[0], win_p[1], win_p[2], win_s[0], win_s[1], win_s[2])
```

```python
import functools
import math

import jax
import jax.numpy as jnp
import numpy as np
from jax import lax
from jax.experimental import pallas as pl
from jax.experimental.pallas import tpu as pltpu

D_MODEL = 1024
DEPTH = 4
N_A_LAYERS = DEPTH // 2
HEAD_DIM = 128
A_HEADS = D_MODEL // HEAD_DIM
Q_HEADS = D_MODEL // HEAD_DIM
KV_HEADS = 2
Q_PER_KV = Q_HEADS // KV_HEADS
N_GROUPS = 3
WINDOWS = (128, 512, 2048)
DILATIONS = (1, 4, 16)
DIL_KEYS = 128
ROT_DIM = HEAD_DIM // 4
ROT_HALF = ROT_DIM // 2
ROPE_THETA = 500000.0
D_FF = 4 * D_MODEL
EPS = 1e-6
NEG = -1e30
EXP_CLAMP = 80.0
PAST_LEN = 8192

KV_COLS = N_GROUPS * 2 * KV_HEADS * HEAD_DIM
Q_COLS = N_GROUPS * Q_HEADS * HEAD_DIM
SUBLANES = 8
VMEM_LIMIT = 56 * 1024 * 1024

_NT = (((1,), (1,)), ((), ()))
_TN = (((0,), (0,)), ((), ()))


def _params(*sem):
    return pltpu.CompilerParams(dimension_semantics=sem, vmem_limit_bytes=VMEM_LIMIT)


def _bf(x):
    return x.astype(jnp.bfloat16)


def _rms_rows(x, gain):
    return x * lax.rsqrt(jnp.mean(x * x, axis=-1, keepdims=True) + EPS) * gain


def _norm_matmul_kernel(x_ref, g_ref, w_ref, o_ref, xn_ref):
    @pl.when(pl.program_id(1) == 0)
    def _():
        xn_ref[...] = _bf(_rms_rows(x_ref[...], g_ref[...]))

    o_ref[...] = jnp.dot(xn_ref[...], w_ref[...], preferred_element_type=jnp.float32)


def _norm_matmul(x, gain, w, tm, tn):
    n, d = x.shape
    m = w.shape[1]
    return pl.pallas_call(
        _norm_matmul_kernel,
        grid=(n // tm, m // tn),
        in_specs=[pl.BlockSpec((tm, d), lambda i, j: (i, 0)),
                  pl.BlockSpec((1, d), lambda i, j: (0, 0)),
                  pl.BlockSpec((d, tn), lambda i, j: (0, j))],
        out_specs=pl.BlockSpec((tm, tn), lambda i, j: (i, j)),
        out_shape=jax.ShapeDtypeStruct((n, m), jnp.float32),
        scratch_shapes=[pltpu.VMEM((tm, d), jnp.bfloat16)],
        compiler_params=_params("parallel", "arbitrary"),
    )(x, gain.reshape(1, d), w)


def _proj_heads_kernel(x_ref, g_ref, w_ref, hg_ref, cos_ref, sin_fwd_ref, sin_bwd_ref, *out_refs, normed, outs):
    xn = _bf(_rms_rows(x_ref[...], g_ref[...]))
    z = jnp.dot(xn, w_ref[...], preferred_element_type=jnp.float32)
    cos, sin_fwd, sin_bwd = cos_ref[...], sin_fwd_ref[...], sin_bwd_ref[...]
    for h, is_normed in enumerate(normed):
        zh = z[:, h * HEAD_DIM:(h + 1) * HEAD_DIM]
        if is_normed:
            zh = _rms_rows(zh, hg_ref[h:h + 1, :])
            zh = (zh * cos + pltpu.roll(zh, ROT_HALF, 1) * sin_fwd
                  + pltpu.roll(zh, HEAD_DIM - ROT_HALF, 1) * sin_bwd)
        for o_ref, (_, h0, h1) in zip(out_refs, outs):
            if h0 <= h < h1:
                o_ref[:, (h - h0) * HEAD_DIM:(h - h0 + 1) * HEAD_DIM] = zh.astype(o_ref.dtype)


def _proj_heads(x, gain, w, head_gain, normed, rope, rope_period_blocks, tm, outs):
    n, d = x.shape
    m = w.shape[1]
    rope_spec = pl.BlockSpec((tm, HEAD_DIM), lambda i: (i % rope_period_blocks, 0))
    return pl.pallas_call(
        functools.partial(_proj_heads_kernel, normed=normed, outs=outs),
        grid=(n // tm,),
        in_specs=[pl.BlockSpec((tm, d), lambda i: (i, 0)),
                  pl.BlockSpec((1, d), lambda i: (0, 0)),
                  pl.BlockSpec((d, m), lambda i: (0, 0)),
                  pl.BlockSpec(head_gain.shape, lambda i: (0, 0)),
                  rope_spec, rope_spec, rope_spec],
        out_specs=[pl.BlockSpec((tm, (h1 - h0) * HEAD_DIM), lambda i: (i, 0)) for _, h0, h1 in outs],
        out_shape=[jax.ShapeDtypeStruct((n, (h1 - h0) * HEAD_DIM), dt) for dt, h0, h1 in outs],
        compiler_params=_params("parallel"),
    )(x, gain.reshape(1, d), w, head_gain, *rope)


def _gla_kernel(zq_ref, zf_ref, zi_ref, zo_ref, lb_ref, og_ref, s0_ref, a_ref, sfin_ref, st_ref,
                *, chunk, heads, t_valid):
    c = pl.program_id(2)
    n_levels = int(math.log2(chunk))
    row = lax.broadcasted_iota(jnp.int32, (chunk, HEAD_DIM), 0)
    row_sq = lax.broadcasted_iota(jnp.int32, (chunk, chunk), 0)
    col_sq = lax.broadcasted_iota(jnp.int32, (chunk, chunk), 1)

    for h in range(heads):
        @pl.when(c == 0)
        def _():
            st_ref[h] = s0_ref[0, h].T

        cols = slice(h * HEAD_DIM, (h + 1) * HEAD_DIM)
        zq, zf, v = zq_ref[0, :, cols], zf_ref[0, :, cols], zi_ref[0, :, cols]
        lb = lb_ref[:, cols]
        q = zq / (1.0 + jnp.exp(-zq))
        log_f = (jnp.minimum(zf, 0.0) - jnp.log1p(jnp.exp(-jnp.abs(zf)))
                 + jnp.log1p(lb * jnp.exp(jnp.minimum(-zf, EXP_CLAMP))))
        k = (1.0 - lb) / (1.0 + jnp.exp(zf))
        if t_valid < chunk:
            live = row < t_valid
            log_f = jnp.where(live, log_f, 0.0)
            k = jnp.where(live, k, 0.0)

        scores = jnp.where(row_sq == col_sq,
                           lax.dot_general(_bf(q), _bf(k), _NT, preferred_element_type=jnp.float32), 0.0)
        pre, tot = log_f, log_f
        for lvl in range(1, n_levels + 1):
            half = 1 << (lvl - 1)
            upper = (row & half) != 0
            e = jnp.exp(jnp.where(upper, pre, tot - pre))
            q_l = jnp.where(upper, q * e, 0.0)
            k_l = jnp.where(upper, 0.0, k * e)
            s_l = lax.dot_general(_bf(q_l), _bf(k_l), _NT, preferred_element_type=jnp.float32)
            scores = scores + jnp.where((row_sq >> lvl) == (col_sq >> lvl), s_l, 0.0)
            tot_sib = jnp.where(upper, pltpu.roll(tot, half, 0), pltpu.roll(tot, chunk - half, 0))
            pre = pre + jnp.where(upper, tot_sib, 0.0)
            tot = tot + tot_sib

        st = st_ref[h]
        o = lax.dot_general(_bf(q * jnp.exp(pre)), _bf(st), _NT, preferred_element_type=jnp.float32)
        o = o + jnp.dot(_bf(scores), _bf(v), preferred_element_type=jnp.float32)
        k_end = k * jnp.exp(tot - pre)
        st_new = st * jnp.exp(tot[0:1, :]) + lax.dot_general(
            _bf(v), _bf(k_end), _TN, preferred_element_type=jnp.float32)
        st_ref[h] = st_new

        zo = zo_ref[0, :, cols]
        a = _rms_rows(o, og_ref[:, cols]) / (1.0 + jnp.exp(-zo))
        a_ref[0, :, cols] = a.astype(a_ref.dtype)

        @pl.when(c == pl.num_programs(2) - 1)
        def _():
            sfin_ref[0, h] = st_new.T


def _gla(z, lb, out_gain, state0, chunk, heads, t_valid):
    b, t, _ = z.shape
    hw = heads * HEAD_DIM
    nh = A_HEADS // heads
    zspec = lambda part: pl.BlockSpec((1, chunk, hw), lambda bi, hi, ci: (bi, ci, part * nh + hi))
    vec_spec = pl.BlockSpec((1, hw), lambda bi, hi, ci: (0, hi))
    st_spec = pl.BlockSpec((1, heads, HEAD_DIM, HEAD_DIM), lambda bi, hi, ci: (bi, hi, 0, 0))
    return pl.pallas_call(
        functools.partial(_gla_kernel, chunk=chunk, heads=heads, t_valid=t_valid),
        grid=(b, nh, t // chunk),
        in_specs=[zspec(0), zspec(1), zspec(2), zspec(3), vec_spec, vec_spec, st_spec],
        out_specs=[pl.BlockSpec((1, chunk, hw), lambda bi, hi, ci: (bi, ci, hi)), st_spec],
        out_shape=[jax.ShapeDtypeStruct((b, t, D_MODEL), jnp.bfloat16),
                   jax.ShapeDtypeStruct(state0.shape, jnp.float32)],
        scratch_shapes=[pltpu.VMEM((heads, HEAD_DIM, HEAD_DIM), jnp.float32)],
        compiler_params=_params("parallel", "parallel", "arbitrary"),
    )(z, z, z, z, lb.reshape(1, D_MODEL), out_gain.reshape(1, D_MODEL), state0)


def _post_mlp_kernel(*refs, merge):
    if merge:
        x_ref, o_ref, lse_ref, wp_ref, g_ref, wu_ref, wd_ref, out_ref, x1_ref, xn_ref, acc_ref = refs
    else:
        x_ref, a_ref, wp_ref, g_ref, wu_ref, wd_ref, out_ref, x1_ref, xn_ref, acc_ref = refs
    j = pl.program_id(1)

    @pl.when(j == 0)
    def _():
        if merge:
            lse = lse_ref[...]
            w = jnp.exp(lse - jnp.max(lse, axis=0, keepdims=True))
            w = w / jnp.sum(w, axis=0, keepdims=True)
            parts = []
            for h in range(Q_HEADS):
                cols = slice(h * HEAD_DIM, (h + 1) * HEAD_DIM)
                parts.append(sum(w[g, :, h:h + 1] * o_ref[g, :, cols] for g in range(N_GROUPS)))
            a = _bf(jnp.concatenate(parts, axis=1))
        else:
            a = a_ref[...]
        x1 = x_ref[...] + jnp.dot(a, wp_ref[...], preferred_element_type=jnp.float32)
        x1_ref[...] = x1
        xn_ref[...] = _bf(_rms_rows(x1, g_ref[...]))
        acc_ref[...] = jnp.zeros_like(acc_ref)

    hdn = jnp.maximum(jnp.dot(xn_ref[...], wu_ref[...], preferred_element_type=jnp.float32), 0.0)
    acc_ref[...] += jnp.dot(_bf(hdn * hdn), wd_ref[...], preferred_element_type=jnp.float32)

    @pl.when(j == pl.num_programs(1) - 1)
    def _():
        out_ref[...] = x1_ref[...] + acc_ref[...]


def _post_mlp(x, mixer_in, wp, gain, wu, wd, tm, tf):
    n, d = x.shape
    f = wu.shape[1]
    merge = isinstance(mixer_in, tuple)
    row_spec = pl.BlockSpec((tm, d), lambda i, j: (i, 0))
    if merge:
        o3, lse3 = mixer_in
        mixer_specs = [pl.BlockSpec((N_GROUPS, tm, d), lambda i, j: (0, i, 0)),
                       pl.BlockSpec((N_GROUPS, tm, HEAD_DIM), lambda i, j: (0, i, 0))]
        mixer_args = (o3, lse3)
    else:
        mixer_specs = [row_spec]
        mixer_args = (mixer_in,)
    return pl.pallas_call(
        functools.partial(_post_mlp_kernel, merge=merge),
        grid=(n // tm, f // tf),
        in_specs=[row_spec, *mixer_specs,
                  pl.BlockSpec((d, d), lambda i, j: (0, 0)),
                  pl.BlockSpec((1, d), lambda i, j: (0, 0)),
                  pl.BlockSpec((d, tf), lambda i, j: (0, j)),
                  pl.BlockSpec((tf, d), lambda i, j: (j, 0))],
        out_specs=row_spec,
        out_shape=jax.ShapeDtypeStruct((n, d), jnp.float32),
        scratch_shapes=[pltpu.VMEM((tm, d), jnp.float32), pltpu.VMEM((tm, d), jnp.bfloat16),
                        pltpu.VMEM((tm, d), jnp.float32)],
        compiler_params=_params("parallel", "arbitrary"),
    )(x, *mixer_args, wp, gain.reshape(1, d), wu, wd)


RES = DILATIONS[-1]


def _block_pieces(group):
    pieces = RES // DILATIONS[group]
    return pieces, DIL_KEYS // pieces


def _band_table(group):
    pieces, per = _block_pieces(group)
    i = np.arange(DIL_KEYS)
    idx = pieces * (i % per) + i // per
    k_true = np.concatenate([idx, DIL_KEYS + idx])
    delta = (DIL_KEYS + idx)[:, None] - k_true[None, :]
    return ((delta >= 0) & (delta <= DIL_KEYS)).astype(np.float32)


def _attn_prompt_kernel(band_ref, q_ref, kp_ref, kc_ref, vp_ref, vc_ref, o_ref, lse_ref, *, pieces, per):
    p = DIL_KEYS
    rows = lambda ref, cols: _bf(jnp.concatenate([ref[0, c, 0, :, cols] for c in range(pieces)], axis=0))
    has_prev = pl.program_id(2) > 0
    kk = lax.broadcasted_iota(jnp.int32, (p, 2 * p), 1)
    band = (band_ref[...] > 0.5) & ((kk >= p) | has_prev)
    mask = jnp.concatenate([band] * Q_PER_KV, axis=0)
    lane = lax.broadcasted_iota(jnp.int32, (p, HEAD_DIM), 1)
    lse_all = jnp.zeros((p, HEAD_DIM), jnp.float32)
    for kvh in range(KV_HEADS):
        kcols = slice(kvh * HEAD_DIM, (kvh + 1) * HEAD_DIM)
        keys = jnp.concatenate([rows(kp_ref, kcols), rows(kc_ref, kcols)], axis=0)
        vals = jnp.concatenate([rows(vp_ref, kcols), rows(vc_ref, kcols)], axis=0)
        heads = [kvh * Q_PER_KV + r for r in range(Q_PER_KV)]
        q = jnp.concatenate([rows(q_ref, slice(h * HEAD_DIM, (h + 1) * HEAD_DIM)) for h in heads], axis=0)
        s = lax.dot_general(q, keys, _NT, preferred_element_type=jnp.float32) * HEAD_DIM ** -0.5
        s = jnp.where(mask, s, NEG)
        m = jnp.max(s, axis=-1, keepdims=True)
        pr = jnp.where(mask, jnp.exp(s - m), 0.0)
        den = jnp.sum(pr, axis=-1, keepdims=True)
        o = jnp.dot(_bf(pr), vals, preferred_element_type=jnp.float32) / den
        lse = m + jnp.log(den)
        for r, h in enumerate(heads):
            for c in range(pieces):
                o_ref[0, c, 0, :, h * HEAD_DIM:(h + 1) * HEAD_DIM] = (
                    o[r * p + c * per:r * p + (c + 1) * per].astype(o_ref.dtype))
            lse_all = jnp.where(lane == h, lse[r * p:(r + 1) * p], lse_all)
    for c in range(pieces):
        lse_ref[0, c, 0] = lse_all[c * per:(c + 1) * per]


def _attn_prompt(q, q_col, kv, group, b, s):
    dil = DILATIONS[group]
    pieces, per = _block_pieces(group)
    lr = s // RES
    view = lambda a: a.reshape(b, pieces, dil, lr, a.shape[-1])
    kw = KV_HEADS * HEAD_DIM

    def spec(width, col, prev=False):
        def idx(bi, r, n):
            return (bi, 0, r, jnp.maximum(n - 1, 0) if prev else n, col)
        return pl.BlockSpec((1, pieces, 1, per, width), idx)

    band = jnp.asarray(_band_table(group))
    o, lse = pl.pallas_call(
        functools.partial(_attn_prompt_kernel, pieces=pieces, per=per),
        grid=(b, dil, lr // per),
        in_specs=[pl.BlockSpec(band.shape, lambda bi, r, n: (0, 0)),
                  spec(D_MODEL, q_col),
                  spec(kw, 2 * group, True), spec(kw, 2 * group),
                  spec(kw, 2 * group + 1, True), spec(kw, 2 * group + 1)],
        out_specs=[spec(D_MODEL, 0), spec(HEAD_DIM, 0)],
        out_shape=[jax.ShapeDtypeStruct((b, pieces, dil, lr, D_MODEL), jnp.float32),
                   jax.ShapeDtypeStruct((b, pieces, dil, lr, HEAD_DIM), jnp.float32)],
        compiler_params=_params("parallel", "parallel", "arbitrary"),
    )(band, view(q), view(kv), view(kv), view(kv), view(kv))
    return o.reshape(b * s, D_MODEL), lse.reshape(b * s, HEAD_DIM)


def _sample_key_blocks(n_new):
    n_cache, tables = [], []
    i = np.arange(DIL_KEYS)
    for g, (win, dil) in enumerate(zip(WINDOWS, DILATIONS)):
        n_past = min(win, PAST_LEN)
        if n_past // DIL_KEYS <= SUBLANES:
            blocks = [DIL_KEYS * k + i for k in range(n_past // DIL_KEYS)]
        else:
            assert n_past % RES == 0 and n_new <= SUBLANES <= RES and dil == RES
            groups = DIL_KEYS // SUBLANES
            blocks = [RES * (groups * k + i // SUBLANES) + i % SUBLANES for k in range(n_past // RES // groups)]
        n_cache.append(len(blocks))
        blocks.append(np.where(i < SUBLANES, n_past + i, -10 ** 9))
        for pos in blocks:
            t = np.arange(SUBLANES)[:, None]
            delta = n_past + t - pos[None, :]
            ok = (delta >= 0) & (delta % dil == 0) & (delta // dil <= DIL_KEYS)
            ok = np.where(t < n_new, ok, True)
            tables.append(ok)
    return n_cache, np.stack(tables).astype(np.float32)


def _attn_sample_kernel(q_ref, kvn_ref, c1_ref, c2_ref, c3_ref, ok_ref, a_ref, *, n_cache):
    rec = 2 * KV_HEADS * HEAD_DIM
    caches = (c1_ref, c2_ref, c3_ref)
    pad = jnp.zeros((DIL_KEYS - SUBLANES, HEAD_DIM), jnp.float32)

    def cache_rows(g, k, cols):
        ref = caches[g]
        if len(ref.shape) == 3:
            return ref[0, k * DIL_KEYS:(k + 1) * DIL_KEYS, cols]
        groups = DIL_KEYS // SUBLANES
        return ref[0, k * groups:(k + 1) * groups, :, cols].reshape(DIL_KEYS, HEAD_DIM)

    for kvh in range(KV_HEADS):
        heads = [kvh * Q_PER_KV + r for r in range(Q_PER_KV)]
        outs, lses = [], []
        blk = 0
        for g in range(N_GROUPS):
            q = jnp.concatenate(
                [q_ref[0, :, (g * Q_HEADS + h) * HEAD_DIM:(g * Q_HEADS + h + 1) * HEAD_DIM] for h in heads], axis=0)
            scores, values, oks = [], [], []
            for k in range(n_cache[g] + 1):
                if k < n_cache[g]:
                    base = kvh * HEAD_DIM
                    keys = _bf(cache_rows(g, k, slice(base, base + HEAD_DIM)))
                    vals = _bf(cache_rows(g, k, slice(base + KV_HEADS * HEAD_DIM, base + (KV_HEADS + 1) * HEAD_DIM)))
                else:
                    base = g * rec + kvh * HEAD_DIM
                    keys = _bf(jnp.concatenate([kvn_ref[0, :, base:base + HEAD_DIM], pad], axis=0))
                    vals = _bf(jnp.concatenate(
                        [kvn_ref[0, :, base + KV_HEADS * HEAD_DIM:base + (KV_HEADS + 1) * HEAD_DIM], pad], axis=0))
                ok = jnp.concatenate([ok_ref[blk]] * Q_PER_KV, axis=0) > 0.5
                blk += 1
                s = lax.dot_general(q, keys, _NT, preferred_element_type=jnp.float32) * HEAD_DIM ** -0.5
                scores.append(jnp.where(ok, s, NEG))
                values.append(vals)
                oks.append(ok)
            m = functools.reduce(jnp.maximum, [jnp.max(s, axis=-1, keepdims=True) for s in scores])
            probs = [jnp.where(ok, jnp.exp(s - m), 0.0) for s, ok in zip(scores, oks)]
            den = sum(jnp.sum(pr, axis=-1, keepdims=True) for pr in probs)
            o = sum(jnp.dot(_bf(pr), vals, preferred_element_type=jnp.float32) for pr, vals in zip(probs, values))
            outs.append(o / den)
            lses.append(m + jnp.log(den))
        top = functools.reduce(jnp.maximum, lses)
        w = [jnp.exp(l - top) for l in lses]
        wsum = sum(w)
        merged = sum((wg / wsum) * og for wg, og in zip(w, outs))
        for r, h in enumerate(heads):
            a_ref[0, :, h * HEAD_DIM:(h + 1) * HEAD_DIM] = merged[r * SUBLANES:(r + 1) * SUBLANES].astype(a_ref.dtype)


def _attn_sample(q, kv_new, caches, n_new):
    b = q.shape[0]
    rec = 2 * KV_HEADS * HEAD_DIM
    n_cache, table = _sample_key_blocks(n_new)
    cache_views, cache_specs = [], []
    for cache in caches:
        n_past = cache.shape[1]
        if n_past // DIL_KEYS <= SUBLANES:
            cache_views.append(cache)
            cache_specs.append(pl.BlockSpec((1, n_past, rec), lambda bi: (bi, 0, 0)))
        else:
            cache_views.append(cache.reshape(b, n_past // RES, RES, rec))
            cache_specs.append(pl.BlockSpec((1, n_past // RES, SUBLANES, rec), lambda bi: (bi, 0, 0, 0)))
    return pl.pallas_call(
        functools.partial(_attn_sample_kernel, n_cache=n_cache),
        grid=(b,),
        in_specs=[pl.BlockSpec((1, SUBLANES, Q_COLS), lambda bi: (bi, 0, 0)),
                  pl.BlockSpec((1, SUBLANES, KV_COLS), lambda bi: (bi, 0, 0)),
                  *cache_specs,
                  pl.BlockSpec(table.shape, lambda bi: (0, 0, 0))],
        out_specs=pl.BlockSpec((1, SUBLANES, D_MODEL), lambda bi: (bi, 0, 0)),
        out_shape=jax.ShapeDtypeStruct((b, SUBLANES, D_MODEL), jnp.bfloat16),
        compiler_params=_params("parallel"),
    )(q, kv_new, *cache_views, jnp.asarray(table))


def _rope_tables(pos):
    inv = ROPE_THETA ** (-2.0 * jnp.arange(ROT_HALF, dtype=jnp.float32) / ROT_DIM)
    ang = pos.astype(jnp.float32)[:, None] * inv[None, :]
    cos, sin = jnp.cos(ang), jnp.sin(ang)
    rest = HEAD_DIM - ROT_DIM
    one, zero = jnp.ones((pos.shape[0], rest), jnp.float32), jnp.zeros((pos.shape[0], rest), jnp.float32)
    zh = jnp.zeros_like(sin)
    return (jnp.concatenate([cos, cos, one], axis=1),
            jnp.concatenate([zh, sin, zero], axis=1),
            jnp.concatenate([-sin, zh, zero], axis=1))


def _to_residue_major(a, b, t):
    return a.reshape(b, t // RES, RES, a.shape[-1]).swapaxes(1, 2).reshape(b * t, a.shape[-1])


def _from_residue_major(a, b, t):
    return a.reshape(b, RES, t // RES, a.shape[-1]).swapaxes(1, 2).reshape(b * t, a.shape[-1])


def _trunk(x, b, t, pos, state0, caches, n_new, weights, lbs, chunk, gla_heads, tm, tm_merge, tm_proj):
    (a_norm, a_w_in, a_out_norm, a_w_out, kv_norm, w_kv, k_norm,
     b_norm, b_w_q, q_norm, b_w_o, mlp_norm, mlp_w_up, mlp_w_down) = weights
    n = b * t
    tf = 512
    fresh = caches is None
    rope = _rope_tables(pos)
    if fresh:
        rope = tuple(_to_residue_major(r, 1, t) for r in rope)
        rope_period = t // tm_proj
    else:
        rope = tuple(jnp.tile(r, (tm_proj // t, 1)) for r in rope)
        rope_period = 1
    n_heads_q = Q_COLS // HEAD_DIM
    finals = []
    kv32 = kv16 = None
    for layer in range(DEPTH):
        if layer < N_A_LAYERS:
            z = _norm_matmul(x, a_norm[layer], a_w_in[layer], tm, 1024)
            a, s_fin = _gla(z.reshape(b, t, 4 * D_MODEL), lbs[layer], a_out_norm[layer], state0[layer],
                            chunk, gla_heads, n_new)
            finals.append(s_fin)
            x = _post_mlp(x, a.reshape(n, D_MODEL), a_w_out[layer], mlp_norm[layer], mlp_w_up[layer],
                          mlp_w_down[layer], tm, tf)
            continue
        j = layer - N_A_LAYERS
        if j == 0:
            if fresh:
                x = _to_residue_major(x, b, t)
            kv_gain = jnp.repeat(jnp.repeat(k_norm, KV_HEADS, axis=0), 2, axis=0)
            n_heads_kv = KV_COLS // HEAD_DIM
            normed = tuple(bool((hh // KV_HEADS) % 2 == 0) for hh in range(n_heads_kv))
            kv32, kv16 = _proj_heads(x, kv_norm, w_kv, kv_gain, normed, rope, rope_period, tm_proj,
                                     ((jnp.float32, 0, n_heads_kv), (jnp.bfloat16, 0, n_heads_kv)))
        q_gain = jnp.repeat(q_norm[j], Q_HEADS, axis=0)
        q_normed = (True,) * n_heads_q
        if fresh:
            q_near, q_far = _proj_heads(x, b_norm[j], b_w_q[j], q_gain, q_normed, rope, rope_period, tm_proj,
                                        ((jnp.float32, 0, Q_HEADS), (jnp.bfloat16, Q_HEADS, n_heads_q)))
            res = [_attn_prompt(q_near, 0, kv32, 0, b, t),
                   _attn_prompt(q_far, 0, kv16, 1, b, t),
                   _attn_prompt(q_far, 1, kv16, 2, b, t)]
            mixer_in = (jnp.stack([r[0] for r in res]), jnp.stack([r[1] for r in res]))
        else:
            (q,) = _proj_heads(x, b_norm[j], b_w_q[j], q_gain, q_normed, rope, rope_period, tm_proj,
                               ((jnp.bfloat16, 0, n_heads_q),))
            a = _attn_sample(q.reshape(b, t, Q_COLS), kv32.reshape(b, t, KV_COLS), caches, n_new)
            mixer_in = a.reshape(n, D_MODEL)
        x = _post_mlp(x, mixer_in, b_w_o[j], mlp_norm[layer], mlp_w_up[layer], mlp_w_down[layer],
                      tm_merge if fresh else tm, tf)
    if fresh:
        x = _from_residue_major(x, b, t)
        keep = WINDOWS[-1] // RES
        kv32 = kv32.reshape(b, RES, t // RES, KV_COLS)[:, :, t // RES - keep:]
        kv32 = kv32.swapaxes(1, 2).reshape(b, keep * RES, KV_COLS)
    else:
        kv32 = kv32.reshape(b, t, KV_COLS)
    return x, jnp.stack(finals), kv32


def _cast_kernel(w_ref, o_ref):
    o_ref[...] = w_ref[...].astype(o_ref.dtype)


def _to_bf16(w, row_block=512):
    rows, cols = math.prod(w.shape[:-1]), w.shape[-1]
    spec = pl.BlockSpec((row_block, cols), lambda i: (i, 0))
    out = pl.pallas_call(
        _cast_kernel, grid=(rows // row_block,), in_specs=[spec], out_specs=spec,
        out_shape=jax.ShapeDtypeStruct((rows, cols), jnp.bfloat16),
        compiler_params=_params("parallel"),
    )(w.reshape(rows, cols))
    return out.reshape(w.shape)


def kernel(x_prompt, x_sample, state_hgrn, cache_win1_kv, cache_win2_kv, cache_win3_kv, a_norm, a_w_in, a_lb_logits, a_out_norm, a_w_out, kv_norm, w_kv, k_norm, b_norm, b_w_q, q_norm, b_w_o, mlp_norm, mlp_w_up, mlp_w_down):
    bp, tp, d = x_prompt.shape
    bs, ts, _ = x_sample.shape
    sm = jax.nn.softmax(a_lb_logits.astype(jnp.float32), axis=0)
    lbs = jnp.cumsum(sm, axis=0) - sm[0]
    weights = (a_norm, _to_bf16(a_w_in), a_out_norm, _to_bf16(a_w_out), kv_norm, _to_bf16(w_kv), k_norm,
               b_norm, _to_bf16(b_w_q), q_norm, _to_bf16(b_w_o), mlp_norm, _to_bf16(mlp_w_up),
               _to_bf16(mlp_w_down))

    zero_state = jnp.zeros((N_A_LAYERS, bp, A_HEADS, HEAD_DIM, HEAD_DIM), jnp.float32)
    y_p, st_p, kv_p = _trunk(x_prompt.reshape(bp * tp, d), bp, tp, jnp.arange(tp), zero_state, None, 128,
                             weights, lbs, chunk=128, gla_heads=2, tm=1024, tm_merge=512, tm_proj=256)

    xs = jnp.pad(x_sample, ((0, 0), (0, SUBLANES - ts), (0, 0))).reshape(bs * SUBLANES, d)
    caches = tuple(c.reshape(bs, c.shape[1], 2 * KV_HEADS * HEAD_DIM)
                   for c in (cache_win1_kv, cache_win2_kv, cache_win3_kv))
    y_s, st_s, kv_s = _trunk(xs, bs, SUBLANES, PAST_LEN + jnp.arange(SUBLANES), state_hgrn, caches, ts,
                             weights, lbs, chunk=SUBLANES, gla_heads=A_HEADS, tm=bs * SUBLANES,
                             tm_merge=bs * SUBLANES, tm_proj=bs * SUBLANES)

    kept = kv_p.shape[1]
    kv_p = kv_p.reshape(bp, kept, N_GROUPS, 2, KV_HEADS, HEAD_DIM)
    kv_s = kv_s.reshape(bs, SUBLANES, N_GROUPS, 2, KV_HEADS, HEAD_DIM)[:, :ts]
    win_p = [kv_p[:, max(kept - WINDOWS[g], 0):, g] for g in range(N_GROUPS)]
    win_s = [kv_s[:, :, g] for g in range(N_GROUPS)]
    return (y_p.reshape(bp, tp, d), y_s.reshape(bs, SUBLANES, d)[:, :ts], st_p, st_s,
            win_p[0], win_p[1], win_p[2], win_s[0], win_s[1], win_s[2])
```

```python
import functools
import math

import jax
import jax.numpy as jnp
import numpy as np
from jax import lax
from jax.experimental import pallas as pl
from jax.experimental.pallas import tpu as pltpu

D_MODEL = 1024
DEPTH = 4
N_A_LAYERS = DEPTH // 2
HEAD_DIM = 128
A_HEADS = D_MODEL // HEAD_DIM
Q_HEADS = D_MODEL // HEAD_DIM
KV_HEADS = 2
Q_PER_KV = Q_HEADS // KV_HEADS
N_GROUPS = 3
WINDOWS = (128, 512, 2048)
DILATIONS = (1, 4, 16)
DIL_KEYS = 128
ROT_DIM = HEAD_DIM // 4
ROT_HALF = ROT_DIM // 2
ROPE_THETA = 500000.0
D_FF = 4 * D_MODEL
EPS = 1e-6
NEG = -1e30
EXP_CLAMP = 80.0
PAST_LEN = 8192

KV_COLS = N_GROUPS * 2 * KV_HEADS * HEAD_DIM
Q_COLS = N_GROUPS * Q_HEADS * HEAD_DIM
SUBLANES = 8
VMEM_LIMIT = 56 * 1024 * 1024

_NT = (((1,), (1,)), ((), ()))
_TN = (((0,), (0,)), ((), ()))


def _params(*sem):
    return pltpu.CompilerParams(dimension_semantics=sem, vmem_limit_bytes=VMEM_LIMIT)


def _bf(x):
    return x.astype(jnp.bfloat16)


def _rms_rows(x, gain):
    return x * lax.rsqrt(jnp.mean(x * x, axis=-1, keepdims=True) + EPS) * gain


def _norm_matmul_kernel(x_ref, g_ref, w_ref, *refs):
    out_refs, xn_ref = refs[:-1], refs[-1]
    j = pl.program_id(1)

    @pl.when(j == 0)
    def _():
        xn_ref[...] = _bf(_rms_rows(x_ref[...], g_ref[...]))

    z = jnp.dot(xn_ref[...], w_ref[...], preferred_element_type=jnp.float32)
    for part, o_ref in enumerate(out_refs):
        @pl.when(j == part)
        def _():
            o_ref[...] = z.astype(o_ref.dtype)


def _norm_matmul(x, gain, w, tm, out_dtypes):
    n, d = x.shape
    parts = len(out_dtypes)
    tn = w.shape[1] // parts
    return pl.pallas_call(
        _norm_matmul_kernel,
        grid=(n // tm, parts),
        in_specs=[pl.BlockSpec((tm, d), lambda i, j: (i, 0)),
                  pl.BlockSpec((1, d), lambda i, j: (0, 0)),
                  pl.BlockSpec((d, tn), lambda i, j: (0, j))],
        out_specs=[pl.BlockSpec((tm, tn), lambda i, j: (i, 0)) for _ in out_dtypes],
        out_shape=[jax.ShapeDtypeStruct((n, tn), dt) for dt in out_dtypes],
        scratch_shapes=[pltpu.VMEM((tm, d), jnp.bfloat16)],
        compiler_params=_params("parallel", "arbitrary"),
    )(x, gain.reshape(1, d), w)


def _proj_heads_kernel(x_ref, g_ref, w_ref, hg_ref, cos_ref, sin_fwd_ref, sin_bwd_ref, *out_refs, normed, outs):
    xn = _bf(_rms_rows(x_ref[...], g_ref[...]))
    z = jnp.dot(xn, w_ref[...], preferred_element_type=jnp.float32)
    cos, sin_fwd, sin_bwd = cos_ref[...], sin_fwd_ref[...], sin_bwd_ref[...]
    for h, is_normed in enumerate(normed):
        zh = z[:, h * HEAD_DIM:(h + 1) * HEAD_DIM]
        if is_normed:
            zh = _rms_rows(zh, hg_ref[h:h + 1, :])
            zh = (zh * cos + pltpu.roll(zh, ROT_HALF, 1) * sin_fwd
                  + pltpu.roll(zh, HEAD_DIM - ROT_HALF, 1) * sin_bwd)
        for o_ref, (_, h0, h1) in zip(out_refs, outs):
            if h0 <= h < h1:
                o_ref[:, (h - h0) * HEAD_DIM:(h - h0 + 1) * HEAD_DIM] = zh.astype(o_ref.dtype)


def _proj_heads(x, gain, w, head_gain, normed, rope, rope_period_blocks, tm, outs):
    n, d = x.shape
    m = w.shape[1]
    rope_spec = pl.BlockSpec((tm, HEAD_DIM), lambda i: (i % rope_period_blocks, 0))
    return pl.pallas_call(
        functools.partial(_proj_heads_kernel, normed=normed, outs=outs),
        grid=(n // tm,),
        in_specs=[pl.BlockSpec((tm, d), lambda i: (i, 0)),
                  pl.BlockSpec((1, d), lambda i: (0, 0)),
                  pl.BlockSpec((d, m), lambda i: (0, 0)),
                  pl.BlockSpec(head_gain.shape, lambda i: (0, 0)),
                  rope_spec, rope_spec, rope_spec],
        out_specs=[pl.BlockSpec((tm, (h1 - h0) * HEAD_DIM), lambda i: (i, 0)) for _, h0, h1 in outs],
        out_shape=[jax.ShapeDtypeStruct((n, (h1 - h0) * HEAD_DIM), dt) for dt, h0, h1 in outs],
        compiler_params=_params("parallel"),
    )(x, gain.reshape(1, d), w, head_gain, *rope)


LOG2E = 1.4426950408889634
MXU_SCAN_MIN_CHUNK = 128


def _gla_tables(chunk):
    n_levels = int(math.log2(chunk))
    t, s = np.arange(chunk)[:, None], np.arange(chunk)[None, :]
    mats, masks, uppers = [], [t == s], []
    for lvl in range(1, n_levels + 1):
        half = 1 << (lvl - 1)
        same_half = (t >> (lvl - 1)) == (s >> (lvl - 1))
        upper = (t & half) != 0
        mats.append(np.where(upper, same_half & (s <= t), same_half & (s > t)))
        masks.append((t >> lvl) == (s >> lvl))
        uppers.append(np.broadcast_to(upper, (chunk, HEAD_DIM)))
    mats += [s <= t, s > t]
    scan = np.concatenate(mats)
    if chunk < MXU_SCAN_MIN_CHUNK:
        scan = np.zeros((2 * SUBLANES, HEAD_DIM))
    return scan.astype(np.float32), np.stack(masks).astype(np.float32), np.stack(uppers).astype(np.float32)


def _gla_kernel(zq_ref, zf_ref, zi_ref, zo_ref, lb_ref, og_ref, s0_ref, scan_ref, same_ref, upper_ref,
                a_ref, sfin_ref, st_ref, *, chunk, heads, t_valid):
    c = pl.program_id(2)
    n_levels = int(math.log2(chunk))
    mxu_scan = chunk >= MXU_SCAN_MIN_CHUNK
    row = lax.broadcasted_iota(jnp.int32, (chunk, HEAD_DIM), 0)

    @pl.when(c == 0)
    def _():
        for h in range(heads):
            st_ref[h] = s0_ref[0, 0, h].T

    hs = range(heads)
    cols = [slice(h * HEAD_DIM, (h + 1) * HEAD_DIM) for h in hs]
    qs, ks, vs, log_fs = [], [], [], []
    for h in hs:
        zq, zf = zq_ref[0, :, cols[h]].astype(jnp.float32), zf_ref[0, :, cols[h]]
        lb = lb_ref[:, cols[h]]
        q = zq / (1.0 + jnp.exp(-zq))
        log_f = (jnp.minimum(zf, 0.0) - jnp.log1p(jnp.exp(-jnp.abs(zf)))
                 + jnp.log1p(lb * jnp.exp(jnp.minimum(-zf, EXP_CLAMP))))
        k = (1.0 - lb) / (1.0 + jnp.exp(zf))
        if t_valid < chunk:
            live = row < t_valid
            log_f = jnp.where(live, log_f, 0.0)
            k = jnp.where(live, k, 0.0)
        qs.append(q), ks.append(k), vs.append(zi_ref[0, :, cols[h]]), log_fs.append(log_f)

    scores = [same_ref[0] * lax.dot_general(_bf(qs[h]), _bf(ks[h]), _NT, preferred_element_type=jnp.float32)
              for h in hs]
    q_ins, k_ends, decays = [], [], []
    if mxu_scan:
        scan = scan_ref[...]
        x_alls = []
        for h in hs:
            g2 = log_fs[h] * LOG2E
            hi = _bf(g2)
            rest = g2 - hi.astype(jnp.float32)
            mid = _bf(rest)
            lo = _bf(rest - mid.astype(jnp.float32))
            x_alls.append(jnp.dot(scan, hi, preferred_element_type=jnp.float32)
                          + jnp.dot(scan, mid, preferred_element_type=jnp.float32)
                          + jnp.dot(scan, lo, preferred_element_type=jnp.float32))
        for lvl in range(1, n_levels + 1):
            half = 1 << (lvl - 1)
            for h in hs:
                q, k = qs[h], ks[h]
                e = jnp.exp2(x_alls[h][(lvl - 1) * chunk:lvl * chunk])
                if half >= SUBLANES:
                    zeros = jnp.zeros((half, HEAD_DIM), jnp.float32)
                    q_parts, k_parts = [], []
                    for start in range(0, chunk, 2 * half):
                        low, up = slice(start, start + half), slice(start + half, start + 2 * half)
                        k_parts += [k[low] * e[low], zeros]
                        q_parts += [zeros, q[up] * e[up]]
                    q_l, k_l = jnp.concatenate(q_parts, axis=0), jnp.concatenate(k_parts, axis=0)
                else:
                    e_up = e * upper_ref[lvl - 1]
                    q_l, k_l = q * e_up, k * (e - e_up)
                s_l = lax.dot_general(_bf(q_l), _bf(k_l), _NT, preferred_element_type=jnp.float32)
                scores[h] = scores[h] + same_ref[lvl] * s_l
        for h in hs:
            x_all = x_alls[h]
            q_ins.append(qs[h] * jnp.exp2(x_all[n_levels * chunk:(n_levels + 1) * chunk]))
            k_ends.append(ks[h] * jnp.exp2(x_all[(n_levels + 1) * chunk:]))
            decays.append(jnp.exp2(x_all[(n_levels + 1) * chunk - 1:(n_levels + 1) * chunk]))
    else:
        for h in hs:
            q, k, pre, tot = qs[h], ks[h], log_fs[h], log_fs[h]
            for lvl in range(1, n_levels + 1):
                half = 1 << (lvl - 1)
                upper = (row & half) != 0
                e = jnp.exp(jnp.where(upper, pre, tot - pre))
                q_l = jnp.where(upper, q * e, 0.0)
                k_l = jnp.where(upper, 0.0, k * e)
                s_l = lax.dot_general(_bf(q_l), _bf(k_l), _NT, preferred_element_type=jnp.float32)
                scores[h] = scores[h] + same_ref[lvl] * s_l
                tot_sib = jnp.where(upper, pltpu.roll(tot, half, 0), pltpu.roll(tot, chunk - half, 0))
                pre = pre + jnp.where(upper, tot_sib, 0.0)
                tot = tot + tot_sib
            q_ins.append(q * jnp.exp(pre)), k_ends.append(k * jnp.exp(tot - pre)), decays.append(jnp.exp(tot[0:1, :]))

    for h in hs:
        st = st_ref[h]
        o = lax.dot_general(_bf(q_ins[h]), _bf(st), _NT, preferred_element_type=jnp.float32)
        o = o + jnp.dot(_bf(scores[h]), _bf(vs[h]), preferred_element_type=jnp.float32)
        st_ref[h] = st * decays[h] + lax.dot_general(_bf(vs[h]), _bf(k_ends[h]), _TN,
                                                     preferred_element_type=jnp.float32)
        zo = zo_ref[0, :, cols[h]].astype(jnp.float32)
        a = _rms_rows(o, og_ref[:, cols[h]]) / (1.0 + jnp.exp(-zo))
        a_ref[0, :, cols[h]] = a.astype(a_ref.dtype)

    @pl.when(c == pl.num_programs(2) - 1)
    def _():
        for h in range(heads):
            sfin_ref[0, 0, h] = st_ref[h].T


def _gla(zs, lb, out_gain, states, layer, chunk, heads, t_valid):
    b, t, _ = zs[0].shape
    hw = heads * HEAD_DIM
    nh = A_HEADS // heads
    scan, same, upper = _gla_tables(chunk)
    zspec = pl.BlockSpec((1, chunk, hw), lambda bi, hi, ci: (bi, ci, hi))
    vec_spec = pl.BlockSpec((1, hw), lambda bi, hi, ci: (0, hi))
    st_block = (1, 1, heads, HEAD_DIM, HEAD_DIM)
    whole = lambda a: pl.BlockSpec(a.shape, lambda bi, hi, ci: (0,) * a.ndim)
    return pl.pallas_call(
        functools.partial(_gla_kernel, chunk=chunk, heads=heads, t_valid=t_valid),
        grid=(b, nh, t // chunk),
        in_specs=[zspec, zspec, zspec, zspec, vec_spec, vec_spec,
                  pl.BlockSpec(st_block, lambda bi, hi, ci: (layer, bi, hi, 0, 0)),
                  whole(scan), whole(same), whole(upper)],
        out_specs=[pl.BlockSpec((1, chunk, hw), lambda bi, hi, ci: (bi, ci, hi)),
                   pl.BlockSpec(st_block, lambda bi, hi, ci: (0, bi, hi, 0, 0))],
        out_shape=[jax.ShapeDtypeStruct((b, t, D_MODEL), jnp.bfloat16),
                   jax.ShapeDtypeStruct((1,) + states.shape[1:], jnp.float32)],
        scratch_shapes=[pltpu.VMEM((heads, HEAD_DIM, HEAD_DIM), jnp.float32)],
        compiler_params=_params("parallel", "parallel", "arbitrary"),
    )(*zs, lb.reshape(1, D_MODEL), out_gain.reshape(1, D_MODEL), states,
      jnp.asarray(scan, jnp.bfloat16), jnp.asarray(same), jnp.asarray(upper))


def _post_mlp_kernel(*refs, merge):
    if merge:
        x_ref, *mix_refs, wp_ref, g_ref, wu_ref, wd_ref, out_ref, xn_ref = refs
    else:
        x_ref, a_ref, wp_ref, g_ref, wu_ref, wd_ref, out_ref, xn_ref = refs
    j = pl.program_id(1)

    @pl.when(j == 0)
    def _():
        if merge:
            o_refs, lse_refs = mix_refs[:N_GROUPS], mix_refs[N_GROUPS:]
            lses = [r[...] for r in lse_refs]
            top = functools.reduce(jnp.maximum, lses)
            w = [jnp.exp(l - top) for l in lses]
            inv = 1.0 / sum(w)
            w = [wg * inv for wg in w]
            parts = []
            for h in range(Q_HEADS):
                cols = slice(h * HEAD_DIM, (h + 1) * HEAD_DIM)
                parts.append(sum(w[g][:, h:h + 1] * o_refs[g][:, cols].astype(jnp.float32)
                                 for g in range(N_GROUPS)))
            a = _bf(jnp.concatenate(parts, axis=1))
        else:
            a = a_ref[...]
        x1 = x_ref[...] + jnp.dot(a, wp_ref[...], preferred_element_type=jnp.float32)
        out_ref[...] = x1
        xn_ref[...] = _bf(_rms_rows(x1, g_ref[...]))

    hdn = jnp.maximum(jnp.dot(xn_ref[...], wu_ref[...], preferred_element_type=jnp.float32), 0.0)
    out_ref[...] += jnp.dot(_bf(hdn * hdn), wd_ref[...], preferred_element_type=jnp.float32)


def _post_mlp(x, mixer_in, wp, gain, wu, wd, tm, tf):
    n, d = x.shape
    f = wu.shape[1]
    merge = isinstance(mixer_in, tuple)
    row_spec = pl.BlockSpec((tm, d), lambda i, j: (i, 0))
    if merge:
        os_, lses = mixer_in
        mixer_specs = [row_spec] * N_GROUPS + [pl.BlockSpec((tm, HEAD_DIM), lambda i, j: (i, 0))] * N_GROUPS
        mixer_args = (*os_, *lses)
    else:
        mixer_specs = [row_spec]
        mixer_args = (mixer_in,)
    return pl.pallas_call(
        functools.partial(_post_mlp_kernel, merge=merge),
        grid=(n // tm, f // tf),
        in_specs=[row_spec, *mixer_specs,
                  pl.BlockSpec((d, d), lambda i, j: (0, 0)),
                  pl.BlockSpec((1, d), lambda i, j: (0, 0)),
                  pl.BlockSpec((d, tf), lambda i, j: (0, j)),
                  pl.BlockSpec((tf, d), lambda i, j: (j, 0))],
        out_specs=row_spec,
        out_shape=jax.ShapeDtypeStruct((n, d), jnp.float32),
        scratch_shapes=[pltpu.VMEM((tm, d), jnp.bfloat16)],
        compiler_params=_params("parallel", "arbitrary"),
    )(x, *mixer_args, wp, gain.reshape(1, d), wu, wd)


RES = DILATIONS[-1]


def _block_pieces(group):
    pieces = RES // DILATIONS[group]
    return pieces, DIL_KEYS // pieces


def _band_table(group):
    pieces, per = _block_pieces(group)
    i = np.arange(DIL_KEYS)
    idx = pieces * (i % per) + i // per
    k_true = np.concatenate([idx, DIL_KEYS + idx])
    delta = (DIL_KEYS + idx)[:, None] - k_true[None, :]
    return ((delta >= 0) & (delta <= DIL_KEYS)).astype(np.float32)


def _attn_prompt_kernel(band_ref, q_ref, kp_ref, kc_ref, vp_ref, vc_ref, o_ref, lse_ref, *, pieces, per):
    p = DIL_KEYS
    rows = lambda ref, cols: _bf(jnp.concatenate([ref[0, c, 0, :, cols] for c in range(pieces)], axis=0))
    has_prev = pl.program_id(2) > 0
    kk = lax.broadcasted_iota(jnp.int32, (p, 2 * p), 1)
    band = (band_ref[...] > 0.5) & ((kk >= p) | has_prev)
    mask = jnp.concatenate([band] * Q_PER_KV, axis=0)
    lane = lax.broadcasted_iota(jnp.int32, (p, HEAD_DIM), 1)
    lse_all = jnp.zeros((p, HEAD_DIM), jnp.float32)
    for kvh in range(KV_HEADS):
        kcols = slice(kvh * HEAD_DIM, (kvh + 1) * HEAD_DIM)
        keys = jnp.concatenate([rows(kp_ref, kcols), rows(kc_ref, kcols)], axis=0)
        vals = jnp.concatenate([rows(vp_ref, kcols), rows(vc_ref, kcols)], axis=0)
        heads = [kvh * Q_PER_KV + r for r in range(Q_PER_KV)]
        q = jnp.concatenate([rows(q_ref, slice(h * HEAD_DIM, (h + 1) * HEAD_DIM)) for h in heads], axis=0)
        s = lax.dot_general(q, keys, _NT, preferred_element_type=jnp.float32) * HEAD_DIM ** -0.5
        s = jnp.where(mask, s, NEG)
        m = jnp.max(s, axis=-1, keepdims=True)
        pr = jnp.where(mask, jnp.exp(s - m), 0.0)
        den = jnp.sum(pr, axis=-1, keepdims=True)
        o = jnp.dot(_bf(pr), vals, preferred_element_type=jnp.float32) / den
        lse = m + jnp.log(den)
        for r, h in enumerate(heads):
            for c in range(pieces):
                o_ref[0, c, 0, :, h * HEAD_DIM:(h + 1) * HEAD_DIM] = (
                    o[r * p + c * per:r * p + (c + 1) * per].astype(o_ref.dtype))
            lse_all = jnp.where(lane == h, lse[r * p:(r + 1) * p], lse_all)
    for c in range(pieces):
        lse_ref[0, c, 0] = lse_all[c * per:(c + 1) * per]


def _attn_prompt(q, q_col, kv, group, b, s):
    dil = DILATIONS[group]
    pieces, per = _block_pieces(group)
    lr = s // RES
    view = lambda a: a.reshape(b, pieces, dil, lr, a.shape[-1])
    kw = KV_HEADS * HEAD_DIM

    def spec(width, col, prev=False):
        def idx(bi, r, n):
            return (bi, 0, r, jnp.maximum(n - 1, 0) if prev else n, col)
        return pl.BlockSpec((1, pieces, 1, per, width), idx)

    band = jnp.asarray(_band_table(group))
    o, lse = pl.pallas_call(
        functools.partial(_attn_prompt_kernel, pieces=pieces, per=per),
        grid=(b, dil, lr // per),
        in_specs=[pl.BlockSpec(band.shape, lambda bi, r, n: (0, 0)),
                  spec(D_MODEL, q_col),
                  spec(kw, 2 * group, True), spec(kw, 2 * group),
                  spec(kw, 2 * group + 1, True), spec(kw, 2 * group + 1)],
        out_specs=[spec(D_MODEL, 0), spec(HEAD_DIM, 0)],
        out_shape=[jax.ShapeDtypeStruct((b, pieces, dil, lr, D_MODEL), jnp.bfloat16),
                   jax.ShapeDtypeStruct((b, pieces, dil, lr, HEAD_DIM), jnp.float32)],
        compiler_params=_params("parallel", "parallel", "arbitrary"),
    )(band, view(q), view(kv), view(kv), view(kv), view(kv))
    return o.reshape(b * s, D_MODEL), lse.reshape(b * s, HEAD_DIM)


def _sample_key_blocks(n_new):
    n_cache, tables = [], []
    i = np.arange(DIL_KEYS)
    for g, (win, dil) in enumerate(zip(WINDOWS, DILATIONS)):
        n_past = min(win, PAST_LEN)
        if n_past // DIL_KEYS <= SUBLANES:
            blocks = [DIL_KEYS * k + i for k in range(n_past // DIL_KEYS)]
        else:
            assert n_past % RES == 0 and n_new <= SUBLANES <= RES and dil == RES
            groups = DIL_KEYS // SUBLANES
            blocks = [RES * (groups * k + i // SUBLANES) + i % SUBLANES for k in range(n_past // RES // groups)]
        n_cache.append(len(blocks))
        blocks.append(np.where(i < SUBLANES, n_past + i, -10 ** 9))
        for pos in blocks:
            t = np.arange(SUBLANES)[:, None]
            delta = n_past + t - pos[None, :]
            ok = (delta >= 0) & (delta % dil == 0) & (delta // dil <= DIL_KEYS)
            ok = np.where(t < n_new, ok, True)
            tables.append(ok)
    return n_cache, np.stack(tables).astype(np.float32)


def _attn_sample_kernel(q_ref, kvn_ref, c1_ref, c2_ref, c3_ref, ok_ref, a_ref, *, n_cache):
    rec = 2 * KV_HEADS * HEAD_DIM
    caches = (c1_ref, c2_ref, c3_ref)
    pad = jnp.zeros((DIL_KEYS - SUBLANES, HEAD_DIM), jnp.float32)

    def cache_rows(g, k, cols):
        ref = caches[g]
        if len(ref.shape) == 3:
            return ref[0, k * DIL_KEYS:(k + 1) * DIL_KEYS, cols]
        groups = DIL_KEYS // SUBLANES
        return ref[0, k * groups:(k + 1) * groups, :, cols].reshape(DIL_KEYS, HEAD_DIM)

    for kvh in range(KV_HEADS):
        heads = [kvh * Q_PER_KV + r for r in range(Q_PER_KV)]
        outs, lses = [], []
        blk = 0
        for g in range(N_GROUPS):
            q = jnp.concatenate(
                [q_ref[0, :, (g * Q_HEADS + h) * HEAD_DIM:(g * Q_HEADS + h + 1) * HEAD_DIM] for h in heads], axis=0)
            scores, values, oks = [], [], []
            for k in range(n_cache[g] + 1):
                if k < n_cache[g]:
                    base = kvh * HEAD_DIM
                    keys = _bf(cache_rows(g, k, slice(base, base + HEAD_DIM)))
                    vals = _bf(cache_rows(g, k, slice(base + KV_HEADS * HEAD_DIM, base + (KV_HEADS + 1) * HEAD_DIM)))
                else:
                    base = g * rec + kvh * HEAD_DIM
                    keys = _bf(jnp.concatenate([kvn_ref[0, :, base:base + HEAD_DIM], pad], axis=0))
                    vals = _bf(jnp.concatenate(
                        [kvn_ref[0, :, base + KV_HEADS * HEAD_DIM:base + (KV_HEADS + 1) * HEAD_DIM], pad], axis=0))
                ok = jnp.concatenate([ok_ref[blk]] * Q_PER_KV, axis=0) > 0.5
                blk += 1
                s = lax.dot_general(q, keys, _NT, preferred_element_type=jnp.float32) * HEAD_DIM ** -0.5
                scores.append(jnp.where(ok, s, NEG))
                values.append(vals)
                oks.append(ok)
            m = functools.reduce(jnp.maximum, [jnp.max(s, axis=-1, keepdims=True) for s in scores])
            probs = [jnp.where(ok, jnp.exp(s - m), 0.0) for s, ok in zip(scores, oks)]
            den = sum(jnp.sum(pr, axis=-1, keepdims=True) for pr in probs)
            o = sum(jnp.dot(_bf(pr), vals, preferred_element_type=jnp.float32) for pr, vals in zip(probs, values))
            outs.append(o / den)
            lses.append(m + jnp.log(den))
        top = functools.reduce(jnp.maximum, lses)
        w = [jnp.exp(l - top) for l in lses]
        wsum = sum(w)
        merged = sum((wg / wsum) * og for wg, og in zip(w, outs))
        for r, h in enumerate(heads):
            a_ref[0, :, h * HEAD_DIM:(h + 1) * HEAD_DIM] = merged[r * SUBLANES:(r + 1) * SUBLANES].astype(a_ref.dtype)


def _attn_sample(q, kv_new, caches, n_new):
    b = q.shape[0]
    rec = 2 * KV_HEADS * HEAD_DIM
    n_cache, table = _sample_key_blocks(n_new)
    cache_views, cache_specs = [], []
    for cache in caches:
        n_past = cache.shape[1]
        if n_past // DIL_KEYS <= SUBLANES:
            cache_views.append(cache)
            cache_specs.append(pl.BlockSpec((1, n_past, rec), lambda bi: (bi, 0, 0)))
        else:
            cache_views.append(cache.reshape(b, n_past // RES, RES, rec))
            cache_specs.append(pl.BlockSpec((1, n_past // RES, SUBLANES, rec), lambda bi: (bi, 0, 0, 0)))
    return pl.pallas_call(
        functools.partial(_attn_sample_kernel, n_cache=n_cache),
        grid=(b,),
        in_specs=[pl.BlockSpec((1, SUBLANES, Q_COLS), lambda bi: (bi, 0, 0)),
                  pl.BlockSpec((1, SUBLANES, KV_COLS), lambda bi: (bi, 0, 0)),
                  *cache_specs,
                  pl.BlockSpec(table.shape, lambda bi: (0, 0, 0))],
        out_specs=pl.BlockSpec((1, SUBLANES, D_MODEL), lambda bi: (bi, 0, 0)),
        out_shape=jax.ShapeDtypeStruct((b, SUBLANES, D_MODEL), jnp.bfloat16),
        compiler_params=_params("parallel"),
    )(q, kv_new, *cache_views, jnp.asarray(table))


def _rope_tables(pos):
    inv = ROPE_THETA ** (-2.0 * jnp.arange(ROT_HALF, dtype=jnp.float32) / ROT_DIM)
    ang = pos.astype(jnp.float32)[:, None] * inv[None, :]
    cos, sin = jnp.cos(ang), jnp.sin(ang)
    rest = HEAD_DIM - ROT_DIM
    one, zero = jnp.ones((pos.shape[0], rest), jnp.float32), jnp.zeros((pos.shape[0], rest), jnp.float32)
    zh = jnp.zeros_like(sin)
    return (jnp.concatenate([cos, cos, one], axis=1),
            jnp.concatenate([zh, sin, zero], axis=1),
            jnp.concatenate([-sin, zh, zero], axis=1))


def _to_residue_major(a, b, t):
    return a.reshape(b, t // RES, RES, a.shape[-1]).swapaxes(1, 2).reshape(b * t, a.shape[-1])


def _from_residue_major(a, b, t):
    return a.reshape(b, RES, t // RES, a.shape[-1]).swapaxes(1, 2).reshape(b * t, a.shape[-1])


def _trunk(x, b, t, pos, state0, caches, n_new, weights, lbs, chunk, gla_heads, tm, tm_merge, tm_proj):
    (a_norm, a_w_in, a_out_norm, a_w_out, kv_norm, w_kv, k_norm,
     b_norm, b_w_q, q_norm, b_w_o, mlp_norm, mlp_w_up, mlp_w_down) = weights
    n = b * t
    tf = 512
    fresh = caches is None
    rope = _rope_tables(pos)
    if fresh:
        rope = tuple(_to_residue_major(r, 1, t) for r in rope)
        rope_period = t // tm_proj
    else:
        rope = tuple(jnp.tile(r, (tm_proj // t, 1)) for r in rope)
        rope_period = 1
    n_heads_q = Q_COLS // HEAD_DIM
    finals = []
    kv32 = kv16 = None
    for layer in range(DEPTH):
        if layer < N_A_LAYERS:
            zs = _norm_matmul(x, a_norm[layer], a_w_in[layer], tm,
                              (jnp.bfloat16, jnp.float32, jnp.bfloat16, jnp.bfloat16))
            a, s_fin = _gla([z.reshape(b, t, D_MODEL) for z in zs], lbs[layer], a_out_norm[layer], state0, layer,
                            chunk, gla_heads, n_new)
            finals.append(s_fin)
            x = _post_mlp(x, a.reshape(n, D_MODEL), a_w_out[layer], mlp_norm[layer], mlp_w_up[layer],
                          mlp_w_down[layer], tm, tf)
            continue
        j = layer - N_A_LAYERS
        if j == 0:
            if fresh:
                x = _to_residue_major(x, b, t)
            kv_gain = jnp.repeat(jnp.repeat(k_norm, KV_HEADS, axis=0), 2, axis=0)
            n_heads_kv = KV_COLS // HEAD_DIM
            normed = tuple(bool((hh // KV_HEADS) % 2 == 0) for hh in range(n_heads_kv))
            kv32, kv16 = _proj_heads(x, kv_norm, w_kv, kv_gain, normed, rope, rope_period, tm_proj,
                                     ((jnp.float32, 0, n_heads_kv), (jnp.bfloat16, 0, n_heads_kv)))
        q_gain = jnp.repeat(q_norm[j], Q_HEADS, axis=0)
        q_normed = (True,) * n_heads_q
        if fresh:
            q_near, q_far = _proj_heads(x, b_norm[j], b_w_q[j], q_gain, q_normed, rope, rope_period, tm_proj,
                                        ((jnp.float32, 0, Q_HEADS), (jnp.bfloat16, Q_HEADS, n_heads_q)))
            res = [_attn_prompt(q_near, 0, kv32, 0, b, t),
                   _attn_prompt(q_far, 0, kv16, 1, b, t),
                   _attn_prompt(q_far, 1, kv16, 2, b, t)]
            mixer_in = ([r[0] for r in res], [r[1] for r in res])
        else:
            (q,) = _proj_heads(x, b_norm[j], b_w_q[j], q_gain, q_normed, rope, rope_period, tm_proj,
                               ((jnp.bfloat16, 0, n_heads_q),))
            a = _attn_sample(q.reshape(b, t, Q_COLS), kv32.reshape(b, t, KV_COLS), caches, n_new)
            mixer_in = a.reshape(n, D_MODEL)
        x = _post_mlp(x, mixer_in, b_w_o[j], mlp_norm[layer], mlp_w_up[layer], mlp_w_down[layer],
                      tm_merge if fresh else tm, tf)
    if fresh:
        x = _from_residue_major(x, b, t)
        keep = WINDOWS[-1] // RES
        kv32 = kv32.reshape(b, RES, t // RES, KV_COLS)[:, :, t // RES - keep:]
        kv32 = kv32.swapaxes(1, 2).reshape(b, keep * RES, KV_COLS)
    else:
        kv32 = kv32.reshape(b, t, KV_COLS)
    return x, jnp.concatenate(finals), kv32


def _cast_kernel(w_ref, o_ref):
    o_ref[...] = w_ref[...].astype(o_ref.dtype)


def _to_bf16(w, row_block=512):
    rows, cols = math.prod(w.shape[:-1]), w.shape[-1]
    spec = pl.BlockSpec((row_block, cols), lambda i: (i, 0))
    out = pl.pallas_call(
        _cast_kernel, grid=(rows // row_block,), in_specs=[spec], out_specs=spec,
        out_shape=jax.ShapeDtypeStruct((rows, cols), jnp.bfloat16),
        compiler_params=_params("parallel"),
    )(w.reshape(rows, cols))
    return out.reshape(w.shape)


def kernel(x_prompt, x_sample, state_hgrn, cache_win1_kv, cache_win2_kv, cache_win3_kv, a_norm, a_w_in, a_lb_logits, a_out_norm, a_w_out, kv_norm, w_kv, k_norm, b_norm, b_w_q, q_norm, b_w_o, mlp_norm, mlp_w_up, mlp_w_down):
    bp, tp, d = x_prompt.shape
    bs, ts, _ = x_sample.shape
    sm = jax.nn.softmax(a_lb_logits.astype(jnp.float32), axis=0)
    lbs = jnp.cumsum(sm, axis=0) - sm[0]
    weights = (a_norm, _to_bf16(a_w_in), a_out_norm, _to_bf16(a_w_out), kv_norm, _to_bf16(w_kv), k_norm,
               b_norm, _to_bf16(b_w_q), q_norm, _to_bf16(b_w_o), mlp_norm, _to_bf16(mlp_w_up),
               _to_bf16(mlp_w_down))

    zero_state = jnp.zeros((N_A_LAYERS, bp, A_HEADS, HEAD_DIM, HEAD_DIM), jnp.float32)
    y_p, st_p, kv_p = _trunk(x_prompt.reshape(bp * tp, d), bp, tp, jnp.arange(tp), zero_state, None, 128,
                             weights, lbs, chunk=128, gla_heads=8, tm=1024, tm_merge=1024, tm_proj=256)

    xs = jnp.pad(x_sample, ((0, 0), (0, SUBLANES - ts), (0, 0))).reshape(bs * SUBLANES, d)
    caches = tuple(c.reshape(bs, c.shape[1], 2 * KV_HEADS * HEAD_DIM)
                   for c in (cache_win1_kv, cache_win2_kv, cache_win3_kv))
    y_s, st_s, kv_s = _trunk(xs, bs, SUBLANES, PAST_LEN + jnp.arange(SUBLANES), state_hgrn, caches, ts,
                             weights, lbs, chunk=SUBLANES, gla_heads=A_HEADS, tm=bs * SUBLANES,
                             tm_merge=bs * SUBLANES, tm_proj=bs * SUBLANES)

    kept, rec = kv_p.shape[1], 2 * KV_HEADS * HEAD_DIM
    kv_s = kv_s.reshape(bs, SUBLANES, N_GROUPS, 2, KV_HEADS, HEAD_DIM)[:, :ts]
    win_p = [kv_p[:, max(kept - WINDOWS[g], 0):, g * rec:(g + 1) * rec].reshape(bp, -1, 2, KV_HEADS, HEAD_DIM)
             for g in range(N_GROUPS)]
    win_s = [kv_s[:, :, g] for g in range(N_GROUPS)]
    return (y_p.reshape(bp, tp, d), y_s.reshape(bs, SUBLANES, d)[:, :ts], st_p, st_s,
            win_p[0], win_p[1], win_p[2], win_s[0], win_s[1], win_s[2])
```

```python
import functools
import math

import jax
import jax.numpy as jnp
import numpy as np
from jax import lax
from jax.experimental import pallas as pl
from jax.experimental.pallas import tpu as pltpu

D_MODEL = 1024
DEPTH = 4
N_A_LAYERS = DEPTH // 2
HEAD_DIM = 128
A_HEADS = D_MODEL // HEAD_DIM
Q_HEADS = D_MODEL // HEAD_DIM
KV_HEADS = 2
Q_PER_KV = Q_HEADS // KV_HEADS
N_GROUPS = 3
WINDOWS = (128, 512, 2048)
DILATIONS = (1, 4, 16)
DIL_KEYS = 128
ROT_DIM = HEAD_DIM // 4
ROT_HALF = ROT_DIM // 2
ROPE_THETA = 500000.0
D_FF = 4 * D_MODEL
EPS = 1e-6
NEG = -1e30
EXP_CLAMP = 80.0
PAST_LEN = 8192

KV_COLS = N_GROUPS * 2 * KV_HEADS * HEAD_DIM
Q_COLS = N_GROUPS * Q_HEADS * HEAD_DIM
SUBLANES = 8
MXU_WIDTH = 256
VMEM_LIMIT = 56 * 1024 * 1024

_NT = (((1,), (1,)), ((), ()))
_TN = (((0,), (0,)), ((), ()))


def _params(*sem):
    return pltpu.CompilerParams(dimension_semantics=sem, vmem_limit_bytes=VMEM_LIMIT)


def _bf(x):
    return x.astype(jnp.bfloat16)


def _rms_rows(x, gain):
    return x * lax.rsqrt(jnp.mean(x * x, axis=-1, keepdims=True) + EPS) * gain


def _norm_matmul_kernel(x_ref, g_ref, w_ref, *refs):
    out_refs, xn_ref = refs[:-1], refs[-1]
    j = pl.program_id(1)

    @pl.when(j == 0)
    def _():
        xn_ref[...] = _bf(_rms_rows(x_ref[...], g_ref[...]))

    z = jnp.dot(xn_ref[...], w_ref[...], preferred_element_type=jnp.float32)
    for part, o_ref in enumerate(out_refs):
        @pl.when(j == part)
        def _():
            o_ref[...] = z.astype(o_ref.dtype)


def _norm_matmul(x, gain, w, tm, out_dtypes):
    n, d = x.shape
    parts = len(out_dtypes)
    tn = w.shape[1] // parts
    return pl.pallas_call(
        _norm_matmul_kernel,
        grid=(n // tm, parts),
        in_specs=[pl.BlockSpec((tm, d), lambda i, j: (i, 0)),
                  pl.BlockSpec((1, d), lambda i, j: (0, 0)),
                  pl.BlockSpec((d, tn), lambda i, j: (0, j))],
        out_specs=[pl.BlockSpec((tm, tn), lambda i, j: (i, 0)) for _ in out_dtypes],
        out_shape=[jax.ShapeDtypeStruct((n, tn), dt) for dt in out_dtypes],
        scratch_shapes=[pltpu.VMEM((tm, d), jnp.bfloat16)],
        compiler_params=_params("parallel", "arbitrary"),
    )(x, gain.reshape(1, d), w)


def _proj_heads_kernel(x_ref, g_ref, w_ref, hg_ref, cos_ref, sin_fwd_ref, sin_bwd_ref, *out_refs, normed, outs):
    xn = _bf(_rms_rows(x_ref[...], g_ref[...]))
    cos, sin_fwd, sin_bwd = cos_ref[...], sin_fwd_ref[...], sin_bwd_ref[...]
    per_dot = MXU_WIDTH // HEAD_DIM
    for h, is_normed in enumerate(normed):
        if h % per_dot == 0:
            z = jnp.dot(xn, w_ref[:, h * HEAD_DIM:(h + per_dot) * HEAD_DIM], preferred_element_type=jnp.float32)
        zh = z[:, (h % per_dot) * HEAD_DIM:(h % per_dot + 1) * HEAD_DIM]
        if is_normed:
            zh = _rms_rows(zh, hg_ref[h:h + 1, :])
            zh = (zh * cos + pltpu.roll(zh, ROT_HALF, 1) * sin_fwd
                  + pltpu.roll(zh, HEAD_DIM - ROT_HALF, 1) * sin_bwd)
        for o_ref, (_, h0, h1) in zip(out_refs, outs):
            if h0 <= h < h1:
                o_ref[:, (h - h0) * HEAD_DIM:(h - h0 + 1) * HEAD_DIM] = zh.astype(o_ref.dtype)


def _proj_heads(x, gain, w, head_gain, normed, rope, rope_period_blocks, tm, outs):
    n, d = x.shape
    m = w.shape[1]
    rope_spec = pl.BlockSpec((tm, HEAD_DIM), lambda i: (i % rope_period_blocks, 0))
    return pl.pallas_call(
        functools.partial(_proj_heads_kernel, normed=normed, outs=outs),
        grid=(n // tm,),
        in_specs=[pl.BlockSpec((tm, d), lambda i: (i, 0)),
                  pl.BlockSpec((1, d), lambda i: (0, 0)),
                  pl.BlockSpec((d, m), lambda i: (0, 0)),
                  pl.BlockSpec(head_gain.shape, lambda i: (0, 0)),
                  rope_spec, rope_spec, rope_spec],
        out_specs=[pl.BlockSpec((tm, (h1 - h0) * HEAD_DIM), lambda i: (i, 0)) for _, h0, h1 in outs],
        out_shape=[jax.ShapeDtypeStruct((n, (h1 - h0) * HEAD_DIM), dt) for dt, h0, h1 in outs],
        compiler_params=_params("parallel"),
    )(x, gain.reshape(1, d), w, head_gain, *rope)


LOG2E = 1.4426950408889634
MXU_SCAN_MIN_CHUNK = 128


TILE_LEVELS = int(math.log2(SUBLANES))


def _gla_tables(chunk):
    n_levels = int(math.log2(chunk))
    t, s = np.arange(chunk)[:, None], np.arange(chunk)[None, :]
    mats, masks, uppers = [], [t == s], []
    for lvl in range(1, n_levels + 1):
        half = 1 << (lvl - 1)
        same_half = (t >> (lvl - 1)) == (s >> (lvl - 1))
        upper = (t & half) != 0
        if lvl <= TILE_LEVELS:
            mats.append(np.where(upper, same_half & (s <= t), same_half & (s > t)))
        masks.append((t >> lvl) == (s >> lvl))
        uppers.append(np.broadcast_to(upper, (chunk, HEAD_DIM)))
    mats += [s <= t, s > t]
    scan = np.concatenate(mats)
    if chunk < MXU_SCAN_MIN_CHUNK:
        scan = np.zeros((2 * SUBLANES, HEAD_DIM))
    return scan.astype(np.float32), np.stack(masks).astype(np.float32), np.stack(uppers).astype(np.float32)


def _gla_kernel(zq_ref, zf_ref, zi_ref, zo_ref, lb_ref, og_ref, s0_ref, scan_ref, same_ref, upper_ref,
                a_ref, sfin_ref, st_ref, *, chunk, heads, t_valid):
    c = pl.program_id(2)
    n_levels = int(math.log2(chunk))
    mxu_scan = chunk >= MXU_SCAN_MIN_CHUNK
    row = lax.broadcasted_iota(jnp.int32, (chunk, HEAD_DIM), 0)

    @pl.when(c == 0)
    def _():
        for h in range(heads):
            st_ref[h] = s0_ref[0, 0, h].T

    hs = range(heads)
    cols = [slice(h * HEAD_DIM, (h + 1) * HEAD_DIM) for h in hs]
    qs, ks, vs, log_fs = [], [], [], []
    for h in hs:
        zq, zf = zq_ref[0, :, cols[h]].astype(jnp.float32), zf_ref[0, :, cols[h]]
        lb = lb_ref[:, cols[h]]
        q = zq / (1.0 + jnp.exp(-zq))
        log_f = (jnp.minimum(zf, 0.0) - jnp.log1p(jnp.exp(-jnp.abs(zf)))
                 + jnp.log1p(lb * jnp.exp(jnp.minimum(-zf, EXP_CLAMP))))
        k = (1.0 - lb) / (1.0 + jnp.exp(zf))
        if t_valid < chunk:
            live = row < t_valid
            log_f = jnp.where(live, log_f, 0.0)
            k = jnp.where(live, k, 0.0)
        qs.append(q), ks.append(k), vs.append(zi_ref[0, :, cols[h]]), log_fs.append(log_f)

    scores = [same_ref[0] * lax.dot_general(_bf(qs[h]), _bf(ks[h]), _NT, preferred_element_type=jnp.float32)
              for h in hs]
    q_ins, k_ends, decays = [], [], []
    if mxu_scan:
        scan = scan_ref[...]
        x_alls = []
        for h in hs:
            g2 = log_fs[h] * LOG2E
            hi = _bf(g2)
            rest = g2 - hi.astype(jnp.float32)
            mid = _bf(rest)
            lo = _bf(rest - mid.astype(jnp.float32))
            x_alls.append(jnp.dot(scan, hi, preferred_element_type=jnp.float32)
                          + jnp.dot(scan, mid, preferred_element_type=jnp.float32)
                          + jnp.dot(scan, lo, preferred_element_type=jnp.float32))
        prefixes = [x[TILE_LEVELS * chunk:(TILE_LEVELS + 1) * chunk] for x in x_alls]
        for lvl in range(1, n_levels + 1):
            half = 1 << (lvl - 1)
            for h in hs:
                q, k, pre = qs[h], ks[h], prefixes[h]
                if half >= SUBLANES:
                    zeros = jnp.zeros((half, HEAD_DIM), jnp.float32)
                    q_parts, k_parts = [], []
                    for start in range(0, chunk, 2 * half):
                        low, up = slice(start, start + half), slice(start + half, start + 2 * half)
                        edge = pre[start + half - 1:start + half]
                        k_parts += [k[low] * jnp.exp2(edge - pre[low]), zeros]
                        q_parts += [zeros, q[up] * jnp.exp2(pre[up] - edge)]
                    q_l, k_l = jnp.concatenate(q_parts, axis=0), jnp.concatenate(k_parts, axis=0)
                else:
                    e = jnp.exp2(x_alls[h][(lvl - 1) * chunk:lvl * chunk])
                    e_up = e * upper_ref[lvl - 1]
                    q_l, k_l = q * e_up, k * (e - e_up)
                s_l = lax.dot_general(_bf(q_l), _bf(k_l), _NT, preferred_element_type=jnp.float32)
                scores[h] = scores[h] + same_ref[lvl] * s_l
        for h in hs:
            q_ins.append(qs[h] * jnp.exp2(prefixes[h]))
            k_ends.append(ks[h] * jnp.exp2(x_alls[h][(TILE_LEVELS + 1) * chunk:]))
            decays.append(jnp.exp2(prefixes[h][chunk - 1:chunk]))
    else:
        for h in hs:
            q, k, pre, tot = qs[h], ks[h], log_fs[h], log_fs[h]
            for lvl in range(1, n_levels + 1):
                half = 1 << (lvl - 1)
                upper = (row & half) != 0
                e = jnp.exp(jnp.where(upper, pre, tot - pre))
                q_l = jnp.where(upper, q * e, 0.0)
                k_l = jnp.where(upper, 0.0, k * e)
                s_l = lax.dot_general(_bf(q_l), _bf(k_l), _NT, preferred_element_type=jnp.float32)
                scores[h] = scores[h] + same_ref[lvl] * s_l
                tot_sib = jnp.where(upper, pltpu.roll(tot, half, 0), pltpu.roll(tot, chunk - half, 0))
                pre = pre + jnp.where(upper, tot_sib, 0.0)
                tot = tot + tot_sib
            q_ins.append(q * jnp.exp(pre)), k_ends.append(k * jnp.exp(tot - pre)), decays.append(jnp.exp(tot[0:1, :]))

    for h in hs:
        st = st_ref[h]
        o = lax.dot_general(_bf(q_ins[h]), _bf(st), _NT, preferred_element_type=jnp.float32)
        o = o + jnp.dot(_bf(scores[h]), _bf(vs[h]), preferred_element_type=jnp.float32)
        st_ref[h] = st * decays[h] + lax.dot_general(_bf(vs[h]), _bf(k_ends[h]), _TN,
                                                     preferred_element_type=jnp.float32)
        zo = zo_ref[0, :, cols[h]].astype(jnp.float32)
        a = _rms_rows(o, og_ref[:, cols[h]]) / (1.0 + jnp.exp(-zo))
        a_ref[0, :, cols[h]] = a.astype(a_ref.dtype)

    @pl.when(c == pl.num_programs(2) - 1)
    def _():
        for h in range(heads):
            sfin_ref[0, 0, h] = st_ref[h].T


def _gla(zs, lb, out_gain, states, layer, chunk, heads, t_valid):
    b, t, _ = zs[0].shape
    hw = heads * HEAD_DIM
    nh = A_HEADS // heads
    scan, same, upper = _gla_tables(chunk)
    zspec = pl.BlockSpec((1, chunk, hw), lambda bi, hi, ci: (bi, ci, hi))
    vec_spec = pl.BlockSpec((1, hw), lambda bi, hi, ci: (0, hi))
    st_block = (1, 1, heads, HEAD_DIM, HEAD_DIM)
    whole = lambda a: pl.BlockSpec(a.shape, lambda bi, hi, ci: (0,) * a.ndim)
    return pl.pallas_call(
        functools.partial(_gla_kernel, chunk=chunk, heads=heads, t_valid=t_valid),
        grid=(b, nh, t // chunk),
        in_specs=[zspec, zspec, zspec, zspec, vec_spec, vec_spec,
                  pl.BlockSpec(st_block, lambda bi, hi, ci: (layer, bi, hi, 0, 0)),
                  whole(scan), whole(same), whole(upper)],
        out_specs=[pl.BlockSpec((1, chunk, hw), lambda bi, hi, ci: (bi, ci, hi)),
                   pl.BlockSpec(st_block, lambda bi, hi, ci: (0, bi, hi, 0, 0))],
        out_shape=[jax.ShapeDtypeStruct((b, t, D_MODEL), jnp.bfloat16),
                   jax.ShapeDtypeStruct((1,) + states.shape[1:], jnp.float32)],
        scratch_shapes=[pltpu.VMEM((heads, HEAD_DIM, HEAD_DIM), jnp.float32)],
        compiler_params=_params("parallel", "parallel", "arbitrary"),
    )(*zs, lb.reshape(1, D_MODEL), out_gain.reshape(1, D_MODEL), states,
      jnp.asarray(scan, jnp.bfloat16), jnp.asarray(same), jnp.asarray(upper))


def _post_mlp_kernel(*refs, merge):
    if merge:
        x_ref, *mix_refs, wp_ref, g_ref, wu_ref, wd_ref, out_ref, xn_ref = refs
    else:
        x_ref, a_ref, wp_ref, g_ref, wu_ref, wd_ref, out_ref, xn_ref = refs
    j = pl.program_id(1)

    @pl.when(j == 0)
    def _():
        if merge:
            o_refs, lse_refs = mix_refs[:N_GROUPS], mix_refs[N_GROUPS:]
            lses = [r[...] for r in lse_refs]
            top = functools.reduce(jnp.maximum, lses)
            w = [jnp.exp(l - top) for l in lses]
            inv = 1.0 / sum(w)
            w = [wg * inv for wg in w]
            parts = []
            for h in range(Q_HEADS):
                cols = slice(h * HEAD_DIM, (h + 1) * HEAD_DIM)
                parts.append(sum(w[g][:, h:h + 1] * o_refs[g][:, cols].astype(jnp.float32)
                                 for g in range(N_GROUPS)))
            a = _bf(jnp.concatenate(parts, axis=1))
        else:
            a = a_ref[...]
        x1 = x_ref[...] + jnp.dot(a, wp_ref[...], preferred_element_type=jnp.float32)
        out_ref[...] = x1
        xn_ref[...] = _bf(_rms_rows(x1, g_ref[...]))

    hdn = jnp.maximum(jnp.dot(xn_ref[...], wu_ref[...], preferred_element_type=jnp.float32), 0.0)
    out_ref[...] += jnp.dot(_bf(hdn * hdn), wd_ref[...], preferred_element_type=jnp.float32)


def _post_mlp(x, mixer_in, wp, gain, wu, wd, tm, tf):
    n, d = x.shape
    f = wu.shape[1]
    merge = isinstance(mixer_in, tuple)
    row_spec = pl.BlockSpec((tm, d), lambda i, j: (i, 0))
    if merge:
        os_, lses = mixer_in
        mixer_specs = [row_spec] * N_GROUPS + [pl.BlockSpec((tm, HEAD_DIM), lambda i, j: (i, 0))] * N_GROUPS
        mixer_args = (*os_, *lses)
    else:
        mixer_specs = [row_spec]
        mixer_args = (mixer_in,)
    return pl.pallas_call(
        functools.partial(_post_mlp_kernel, merge=merge),
        grid=(n // tm, f // tf),
        in_specs=[row_spec, *mixer_specs,
                  pl.BlockSpec((d, d), lambda i, j: (0, 0)),
                  pl.BlockSpec((1, d), lambda i, j: (0, 0)),
                  pl.BlockSpec((d, tf), lambda i, j: (0, j)),
                  pl.BlockSpec((tf, d), lambda i, j: (j, 0))],
        out_specs=row_spec,
        out_shape=jax.ShapeDtypeStruct((n, d), jnp.float32),
        scratch_shapes=[pltpu.VMEM((tm, d), jnp.bfloat16)],
        compiler_params=_params("parallel", "arbitrary"),
    )(x, *mixer_args, wp, gain.reshape(1, d), wu, wd)


RES = DILATIONS[-1]


def _block_pieces(group):
    pieces = RES // DILATIONS[group]
    return pieces, DIL_KEYS // pieces


def _band_bias(group):
    pieces, per = _block_pieces(group)
    i = np.arange(DIL_KEYS)
    idx = pieces * (i % per) + i // per
    k_true = np.concatenate([idx, DIL_KEYS + idx])
    delta = (DIL_KEYS + idx)[:, None] - k_true[None, :]
    band = (delta >= 0) & (delta <= DIL_KEYS)
    first = band & (np.arange(2 * DIL_KEYS) >= DIL_KEYS)[None, :]
    return np.where(np.stack([first, band]), 0.0, NEG).astype(np.float32)


def _attn_prompt_kernel(bias_ref, q_ref, kp_ref, kc_ref, vp_ref, vc_ref, o_ref, lse_ref, *, pieces, per, qb):
    p = DIL_KEYS

    def rows(ref, j, cols):
        return _bf(jnp.concatenate([ref[0, c, 0, j * per:(j + 1) * per, cols] for c in range(pieces)], axis=0))

    lane = lax.broadcasted_iota(jnp.int32, (p, HEAD_DIM), 1)
    for j in range(qb):
        table = jnp.minimum(pl.program_id(2), 1) if j == 0 else 1
        bias = jnp.concatenate([bias_ref[table]] * Q_PER_KV, axis=0)
        lse_all = jnp.zeros((p, HEAD_DIM), jnp.float32)
        for kvh in range(KV_HEADS):
            kcols = slice(kvh * HEAD_DIM, (kvh + 1) * HEAD_DIM)
            if j == 0:
                k_before, v_before = rows(kp_ref, 0, kcols), rows(vp_ref, 0, kcols)
            else:
                k_before, v_before = rows(kc_ref, j - 1, kcols), rows(vc_ref, j - 1, kcols)
            keys = jnp.concatenate([k_before, rows(kc_ref, j, kcols)], axis=0)
            vals = jnp.concatenate([v_before, rows(vc_ref, j, kcols)], axis=0)
            heads = [kvh * Q_PER_KV + r for r in range(Q_PER_KV)]
            q = jnp.concatenate([rows(q_ref, j, slice(h * HEAD_DIM, (h + 1) * HEAD_DIM)) for h in heads], axis=0)
            s2 = (lax.dot_general(q, keys, _NT, preferred_element_type=jnp.float32) * (HEAD_DIM ** -0.5 * LOG2E)
                  + bias)
            m2 = jnp.max(s2, axis=-1, keepdims=True)
            pr = jnp.exp2(s2 - m2)
            den = jnp.sum(pr, axis=-1, keepdims=True)
            o = jnp.dot(_bf(pr), vals, preferred_element_type=jnp.float32) / den
            lse = m2 * (1.0 / LOG2E) + jnp.log(den)
            for r, h in enumerate(heads):
                for c in range(pieces):
                    o_ref[0, c, 0, j * per:(j + 1) * per, h * HEAD_DIM:(h + 1) * HEAD_DIM] = (
                        o[r * p + c * per:r * p + (c + 1) * per].astype(o_ref.dtype))
                lse_all = jnp.where(lane == h, lse[r * p:(r + 1) * p], lse_all)
        for c in range(pieces):
            lse_ref[0, c, 0, j * per:(j + 1) * per] = lse_all[c * per:(c + 1) * per]


def _attn_prompt(q, q_col, kv, group, b, s, qb=2):
    dil = DILATIONS[group]
    pieces, per = _block_pieces(group)
    lr = s // RES
    view = lambda a: a.reshape(b, pieces, dil, lr, a.shape[-1])
    kw = KV_HEADS * HEAD_DIM

    def spec(width, col):
        return pl.BlockSpec((1, pieces, 1, qb * per, width), lambda bi, r, n: (bi, 0, r, n, col))

    def before_spec(col):
        return pl.BlockSpec((1, pieces, 1, per, kw), lambda bi, r, n: (bi, 0, r, jnp.maximum(n * qb - 1, 0), col))

    band = jnp.asarray(_band_bias(group))
    o, lse = pl.pallas_call(
        functools.partial(_attn_prompt_kernel, pieces=pieces, per=per, qb=qb),
        grid=(b, dil, lr // (per * qb)),
        in_specs=[pl.BlockSpec(band.shape, lambda bi, r, n: (0, 0, 0)),
                  spec(D_MODEL, q_col),
                  before_spec(2 * group), spec(kw, 2 * group),
                  before_spec(2 * group + 1), spec(kw, 2 * group + 1)],
        out_specs=[spec(D_MODEL, 0), spec(HEAD_DIM, 0)],
        out_shape=[jax.ShapeDtypeStruct((b, pieces, dil, lr, D_MODEL), jnp.bfloat16),
                   jax.ShapeDtypeStruct((b, pieces, dil, lr, HEAD_DIM), jnp.float32)],
        compiler_params=_params("parallel", "parallel", "arbitrary"),
    )(band, view(q), view(kv), view(kv), view(kv), view(kv))
    return o.reshape(b * s, D_MODEL), lse.reshape(b * s, HEAD_DIM)


def _sample_key_blocks(n_new):
    n_cache, tables = [], []
    i = np.arange(DIL_KEYS)
    for g, (win, dil) in enumerate(zip(WINDOWS, DILATIONS)):
        n_past = min(win, PAST_LEN)
        if n_past // DIL_KEYS <= SUBLANES:
            blocks = [DIL_KEYS * k + i for k in range(n_past // DIL_KEYS)]
        else:
            assert n_past % RES == 0 and n_new <= SUBLANES <= RES and dil == RES
            groups = DIL_KEYS // SUBLANES
            blocks = [RES * (groups * k + i // SUBLANES) + i % SUBLANES for k in range(n_past // RES // groups)]
        n_cache.append(len(blocks))
        blocks.append(np.where(i < SUBLANES, n_past + i, -10 ** 9))
        for pos in blocks:
            t = np.arange(SUBLANES)[:, None]
            delta = n_past + t - pos[None, :]
            ok = (delta >= 0) & (delta % dil == 0) & (delta // dil <= DIL_KEYS)
            ok = np.where(t < n_new, ok, True)
            tables.append(ok)
    return n_cache, np.stack(tables).astype(np.float32)


def _attn_sample_kernel(q_ref, kvn_ref, c1_ref, c2_ref, c3_ref, ok_ref, a_ref, *, n_cache):
    rec = 2 * KV_HEADS * HEAD_DIM
    caches = (c1_ref, c2_ref, c3_ref)
    pad = jnp.zeros((DIL_KEYS - SUBLANES, HEAD_DIM), jnp.float32)

    def cache_rows(g, k, is_v, kvh):
        ref = caches[g]
        if len(ref.shape) == 5:
            return ref[0, k * DIL_KEYS:(k + 1) * DIL_KEYS, is_v, kvh, :]
        groups = DIL_KEYS // SUBLANES
        return ref[0, k * groups:(k + 1) * groups, :, is_v, kvh, :].reshape(DIL_KEYS, HEAD_DIM)

    for kvh in range(KV_HEADS):
        heads = [kvh * Q_PER_KV + r for r in range(Q_PER_KV)]
        outs, lses = [], []
        blk = 0
        for g in range(N_GROUPS):
            q = jnp.concatenate(
                [q_ref[0, :, (g * Q_HEADS + h) * HEAD_DIM:(g * Q_HEADS + h + 1) * HEAD_DIM] for h in heads], axis=0)
            scores, values, oks = [], [], []
            for k in range(n_cache[g] + 1):
                if k < n_cache[g]:
                    keys, vals = _bf(cache_rows(g, k, 0, kvh)), _bf(cache_rows(g, k, 1, kvh))
                else:
                    base = g * rec + kvh * HEAD_DIM
                    keys = _bf(jnp.concatenate([kvn_ref[0, :, base:base + HEAD_DIM], pad], axis=0))
                    vals = _bf(jnp.concatenate(
                        [kvn_ref[0, :, base + KV_HEADS * HEAD_DIM:base + (KV_HEADS + 1) * HEAD_DIM], pad], axis=0))
                ok = jnp.concatenate([ok_ref[blk]] * Q_PER_KV, axis=0) > 0.5
                blk += 1
                s = lax.dot_general(q, keys, _NT, preferred_element_type=jnp.float32) * HEAD_DIM ** -0.5
                scores.append(jnp.where(ok, s, NEG))
                values.append(vals)
                oks.append(ok)
            m = functools.reduce(jnp.maximum, [jnp.max(s, axis=-1, keepdims=True) for s in scores])
            probs = [jnp.where(ok, jnp.exp(s - m), 0.0) for s, ok in zip(scores, oks)]
            den = sum(jnp.sum(pr, axis=-1, keepdims=True) for pr in probs)
            o = sum(jnp.dot(_bf(pr), vals, preferred_element_type=jnp.float32) for pr, vals in zip(probs, values))
            outs.append(o / den)
            lses.append(m + jnp.log(den))
        top = functools.reduce(jnp.maximum, lses)
        w = [jnp.exp(l - top) for l in lses]
        wsum = sum(w)
        merged = sum((wg / wsum) * og for wg, og in zip(w, outs))
        for r, h in enumerate(heads):
            a_ref[0, :, h * HEAD_DIM:(h + 1) * HEAD_DIM] = merged[r * SUBLANES:(r + 1) * SUBLANES].astype(a_ref.dtype)


def _attn_sample(q, kv_new, caches, n_new):
    b = q.shape[0]
    rec = 2 * KV_HEADS * HEAD_DIM
    n_cache, table = _sample_key_blocks(n_new)
    cache_views, cache_specs = [], []
    rec_shape = (2, KV_HEADS, HEAD_DIM)
    for cache in caches:
        n_past = cache.shape[1]
        if n_past // DIL_KEYS <= SUBLANES:
            cache_views.append(cache)
            cache_specs.append(pl.BlockSpec((1, n_past) + rec_shape, lambda bi: (bi, 0, 0, 0, 0)))
        else:
            cache_views.append(cache.reshape((b, n_past // RES, RES) + rec_shape))
            cache_specs.append(pl.BlockSpec((1, n_past // RES, SUBLANES) + rec_shape, lambda bi: (bi, 0, 0, 0, 0, 0)))
    return pl.pallas_call(
        functools.partial(_attn_sample_kernel, n_cache=n_cache),
        grid=(b,),
        in_specs=[pl.BlockSpec((1, SUBLANES, Q_COLS), lambda bi: (bi, 0, 0)),
                  pl.BlockSpec((1, SUBLANES, KV_COLS), lambda bi: (bi, 0, 0)),
                  *cache_specs,
                  pl.BlockSpec(table.shape, lambda bi: (0, 0, 0))],
        out_specs=pl.BlockSpec((1, SUBLANES, D_MODEL), lambda bi: (bi, 0, 0)),
        out_shape=jax.ShapeDtypeStruct((b, SUBLANES, D_MODEL), jnp.bfloat16),
        compiler_params=_params("parallel"),
    )(q, kv_new, *cache_views, jnp.asarray(table))


def _rope_tables(pos):
    inv = ROPE_THETA ** (-2.0 * jnp.arange(ROT_HALF, dtype=jnp.float32) / ROT_DIM)
    ang = pos.astype(jnp.float32)[:, None] * inv[None, :]
    cos, sin = jnp.cos(ang), jnp.sin(ang)
    rest = HEAD_DIM - ROT_DIM
    one, zero = jnp.ones((pos.shape[0], rest), jnp.float32), jnp.zeros((pos.shape[0], rest), jnp.float32)
    zh = jnp.zeros_like(sin)
    return (jnp.concatenate([cos, cos, one], axis=1),
            jnp.concatenate([zh, sin, zero], axis=1),
            jnp.concatenate([-sin, zh, zero], axis=1))


def _to_residue_major(a, b, t):
    return a.reshape(b, t // RES, RES, a.shape[-1]).swapaxes(1, 2).reshape(b * t, a.shape[-1])


def _from_residue_major(a, b, t):
    return a.reshape(b, RES, t // RES, a.shape[-1]).swapaxes(1, 2).reshape(b * t, a.shape[-1])


def _trunk(x, b, t, pos, state0, caches, n_new, weights, lbs, chunk, gla_heads, tm, tm_merge, tm_proj):
    (a_norm, a_w_in, a_out_norm, a_w_out, kv_norm, w_kv, k_norm,
     b_norm, b_w_q, q_norm, b_w_o, mlp_norm, mlp_w_up, mlp_w_down) = weights
    n = b * t
    tf = 512
    fresh = caches is None
    rope = _rope_tables(pos)
    if fresh:
        rope = tuple(_to_residue_major(r, 1, t) for r in rope)
        rope_period = t // tm_proj
    else:
        rope = tuple(jnp.tile(r, (tm_proj // t, 1)) for r in rope)
        rope_period = 1
    n_heads_q = Q_COLS // HEAD_DIM
    finals = []
    kv32 = kv16 = None
    for layer in range(DEPTH):
        if layer < N_A_LAYERS:
            zs = _norm_matmul(x, a_norm[layer], a_w_in[layer], tm,
                              (jnp.bfloat16, jnp.float32, jnp.bfloat16, jnp.bfloat16))
            a, s_fin = _gla([z.reshape(b, t, D_MODEL) for z in zs], lbs[layer], a_out_norm[layer], state0, layer,
                            chunk, gla_heads, n_new)
            finals.append(s_fin)
            x = _post_mlp(x, a.reshape(n, D_MODEL), a_w_out[layer], mlp_norm[layer], mlp_w_up[layer],
                          mlp_w_down[layer], tm, tf)
            continue
        j = layer - N_A_LAYERS
        if j == 0:
            if fresh:
                x = _to_residue_major(x, b, t)
            kv_gain = jnp.repeat(jnp.repeat(k_norm, KV_HEADS, axis=0), 2, axis=0)
            n_heads_kv = KV_COLS // HEAD_DIM
            normed = tuple(bool((hh // KV_HEADS) % 2 == 0) for hh in range(n_heads_kv))
            kv32, kv16 = _proj_heads(x, kv_norm, w_kv, kv_gain, normed, rope, rope_period, tm_proj,
                                     ((jnp.float32, 0, n_heads_kv), (jnp.bfloat16, 0, n_heads_kv)))
        q_gain = jnp.repeat(q_norm[j], Q_HEADS, axis=0)
        q_normed = (True,) * n_heads_q
        if fresh:
            q_near, q_far = _proj_heads(x, b_norm[j], b_w_q[j], q_gain, q_normed, rope, rope_period, tm_proj,
                                        ((jnp.float32, 0, Q_HEADS), (jnp.bfloat16, Q_HEADS, n_heads_q)))
            res = [_attn_prompt(q_near, 0, kv32, 0, b, t),
                   _attn_prompt(q_far, 0, kv16, 1, b, t),
                   _attn_prompt(q_far, 1, kv16, 2, b, t)]
            mixer_in = ([r[0] for r in res], [r[1] for r in res])
        else:
            (q,) = _proj_heads(x, b_norm[j], b_w_q[j], q_gain, q_normed, rope, rope_period, tm_proj,
                               ((jnp.bfloat16, 0, n_heads_q),))
            a = _attn_sample(q.reshape(b, t, Q_COLS), kv32.reshape(b, t, KV_COLS), caches, n_new)
            mixer_in = a.reshape(n, D_MODEL)
        x = _post_mlp(x, mixer_in, b_w_o[j], mlp_norm[layer], mlp_w_up[layer], mlp_w_down[layer],
                      tm_merge if fresh else tm, tf)
    if fresh:
        x = _from_residue_major(x, b, t)
        keep = WINDOWS[-1] // RES
        kv32 = kv32.reshape(b, RES, t // RES, KV_COLS)[:, :, t // RES - keep:]
        kv32 = kv32.swapaxes(1, 2).reshape(b, keep * RES, KV_COLS)
    else:
        kv32 = kv32.reshape(b, t, KV_COLS)
    return x, jnp.concatenate(finals), kv32


def _cast_kernel(w_ref, o_ref):
    o_ref[...] = w_ref[...].astype(o_ref.dtype)


def _to_bf16(w, row_block=512):
    rows, cols = math.prod(w.shape[:-1]), w.shape[-1]
    spec = pl.BlockSpec((row_block, cols), lambda i: (i, 0))
    out = pl.pallas_call(
        _cast_kernel, grid=(rows // row_block,), in_specs=[spec], out_specs=spec,
        out_shape=jax.ShapeDtypeStruct((rows, cols), jnp.bfloat16),
        compiler_params=_params("parallel"),
    )(w.reshape(rows, cols))
    return out.reshape(w.shape)


def kernel(x_prompt, x_sample, state_hgrn, cache_win1_kv, cache_win2_kv, cache_win3_kv, a_norm, a_w_in, a_lb_logits, a_out_norm, a_w_out, kv_norm, w_kv, k_norm, b_norm, b_w_q, q_norm, b_w_o, mlp_norm, mlp_w_up, mlp_w_down):
    bp, tp, d = x_prompt.shape
    bs, ts, _ = x_sample.shape
    sm = jax.nn.softmax(a_lb_logits.astype(jnp.float32), axis=0)
    lbs = jnp.cumsum(sm, axis=0) - sm[0]
    weights = (a_norm, _to_bf16(a_w_in), a_out_norm, _to_bf16(a_w_out), kv_norm, _to_bf16(w_kv), k_norm,
               b_norm, _to_bf16(b_w_q), q_norm, _to_bf16(b_w_o), mlp_norm, _to_bf16(mlp_w_up),
               _to_bf16(mlp_w_down))

    zero_state = jnp.zeros((N_A_LAYERS, bp, A_HEADS, HEAD_DIM, HEAD_DIM), jnp.float32)
    y_p, st_p, kv_p = _trunk(x_prompt.reshape(bp * tp, d), bp, tp, jnp.arange(tp), zero_state, None, 128,
                             weights, lbs, chunk=128, gla_heads=8, tm=1024, tm_merge=1024, tm_proj=256)

    xs = jnp.pad(x_sample, ((0, 0), (0, SUBLANES - ts), (0, 0))).reshape(bs * SUBLANES, d)
    caches = (cache_win1_kv, cache_win2_kv, cache_win3_kv)
    y_s, st_s, kv_s = _trunk(xs, bs, SUBLANES, PAST_LEN + jnp.arange(SUBLANES), state_hgrn, caches, ts,
                             weights, lbs, chunk=SUBLANES, gla_heads=A_HEADS, tm=bs * SUBLANES,
                             tm_merge=bs * SUBLANES, tm_proj=bs * SUBLANES)

    kept, rec = kv_p.shape[1], 2 * KV_HEADS * HEAD_DIM
    kv_s = kv_s.reshape(bs, SUBLANES, N_GROUPS, 2, KV_HEADS, HEAD_DIM)[:, :ts]
    win_p = [kv_p[:, max(kept - WINDOWS[g], 0):, g * rec:(g + 1) * rec].reshape(bp, -1, 2, KV_HEADS, HEAD_DIM)
             for g in range(N_GROUPS)]
    win_s = [kv_s[:, :, g] for g in range(N_GROUPS)]
    return (y_p.reshape(bp, tp, d), y_s.reshape(bs, SUBLANES, d)[:, :ts], st_p, st_s,
            win_p[0], win_p[1], win_p[2], win_s[0], win_s[1], win_s[2])
```

```python
import functools
import math

import jax
import jax.numpy as jnp
import numpy as np
from jax import lax
from jax.experimental import pallas as pl
from jax.experimental.pallas import tpu as pltpu

D_MODEL = 1024
DEPTH = 4
N_A_LAYERS = DEPTH // 2
HEAD_DIM = 128
A_HEADS = D_MODEL // HEAD_DIM
Q_HEADS = D_MODEL // HEAD_DIM
KV_HEADS = 2
Q_PER_KV = Q_HEADS // KV_HEADS
N_GROUPS = 3
WINDOWS = (128, 512, 2048)
DILATIONS = (1, 4, 16)
DIL_KEYS = 128
ROT_DIM = HEAD_DIM // 4
ROT_HALF = ROT_DIM // 2
ROPE_THETA = 500000.0
D_FF = 4 * D_MODEL
EPS = 1e-6
NEG = -1e30
EXP_CLAMP = 80.0
PAST_LEN = 8192

KV_COLS = N_GROUPS * 2 * KV_HEADS * HEAD_DIM
Q_COLS = N_GROUPS * Q_HEADS * HEAD_DIM
SUBLANES = 8
MXU_WIDTH = 256
VMEM_LIMIT = 56 * 1024 * 1024

_NT = (((1,), (1,)), ((), ()))
_TN = (((0,), (0,)), ((), ()))


def _params(*sem):
    return pltpu.CompilerParams(dimension_semantics=sem, vmem_limit_bytes=VMEM_LIMIT)


def _bf(x):
    return x.astype(jnp.bfloat16)


def _rms_rows(x, gain):
    return x * lax.rsqrt(jnp.mean(x * x, axis=-1, keepdims=True) + EPS) * gain


def _norm_matmul_kernel(x_ref, g_ref, w_ref, *out_refs):
    xn = _bf(_rms_rows(x_ref[...], g_ref[...]))
    tn = w_ref.shape[1] // len(out_refs)
    for part, o_ref in enumerate(out_refs):
        o_ref[...] = jnp.dot(xn, w_ref[:, part * tn:(part + 1) * tn],
                             preferred_element_type=jnp.float32).astype(o_ref.dtype)


def _norm_matmul(x, gain, w, tm, out_dtypes):
    n, d = x.shape
    tn = w.shape[1] // len(out_dtypes)
    return pl.pallas_call(
        _norm_matmul_kernel,
        grid=(n // tm,),
        in_specs=[pl.BlockSpec((tm, d), lambda i: (i, 0)),
                  pl.BlockSpec((1, d), lambda i: (0, 0)),
                  pl.BlockSpec(w.shape, lambda i: (0, 0))],
        out_specs=[pl.BlockSpec((tm, tn), lambda i: (i, 0)) for _ in out_dtypes],
        out_shape=[jax.ShapeDtypeStruct((n, tn), dt) for dt in out_dtypes],
        compiler_params=_params("parallel"),
    )(x, gain.reshape(1, d), w)


PROJ_HEAD_GROUP = 4


def _proj_heads_kernel(x_ref, g_ref, w_ref, hg_ref, cos_ref, sin_fwd_ref, sin_bwd_ref, *out_refs, normed, outs):
    xn = _bf(_rms_rows(x_ref[...], g_ref[...]))
    cos, sin_fwd, sin_bwd = cos_ref[...], sin_fwd_ref[...], sin_bwd_ref[...]
    width = PROJ_HEAD_GROUP * HEAD_DIM
    for g0 in range(0, len(normed), PROJ_HEAD_GROUP):
        z = jnp.dot(xn, w_ref[:, g0 * HEAD_DIM:g0 * HEAD_DIM + width], preferred_element_type=jnp.float32)
        hs = range(g0, g0 + PROJ_HEAD_GROUP)
        zs = {h: z[:, (h - g0) * HEAD_DIM:(h - g0 + 1) * HEAD_DIM] for h in hs}
        normed_hs = [h for h in hs if normed[h]]
        ms = {h: jnp.mean(zs[h] * zs[h], axis=-1, keepdims=True) for h in normed_hs}
        for h in normed_hs:
            zs[h] = zs[h] * lax.rsqrt(ms[h] + EPS) * hg_ref[h:h + 1, :]
        fwd = {h: pltpu.roll(zs[h], ROT_HALF, 1) for h in normed_hs}
        bwd = {h: pltpu.roll(zs[h], HEAD_DIM - ROT_HALF, 1) for h in normed_hs}
        for h in normed_hs:
            zs[h] = zs[h] * cos + fwd[h] * sin_fwd + bwd[h] * sin_bwd
        for h in hs:
            for o_ref, (_, h0, h1) in zip(out_refs, outs):
                if h0 <= h < h1:
                    o_ref[:, (h - h0) * HEAD_DIM:(h - h0 + 1) * HEAD_DIM] = zs[h].astype(o_ref.dtype)


def _proj_heads(x, gain, w, head_gain, normed, rope, rope_period_blocks, tm, outs):
    n, d = x.shape
    m = w.shape[1]
    rope_spec = pl.BlockSpec((tm, HEAD_DIM), lambda i: (i % rope_period_blocks, 0))
    return pl.pallas_call(
        functools.partial(_proj_heads_kernel, normed=normed, outs=outs),
        grid=(n // tm,),
        in_specs=[pl.BlockSpec((tm, d), lambda i: (i, 0)),
                  pl.BlockSpec((1, d), lambda i: (0, 0)),
                  pl.BlockSpec((d, m), lambda i: (0, 0)),
                  pl.BlockSpec(head_gain.shape, lambda i: (0, 0)),
                  rope_spec, rope_spec, rope_spec],
        out_specs=[pl.BlockSpec((tm, (h1 - h0) * HEAD_DIM), lambda i: (i, 0)) for _, h0, h1 in outs],
        out_shape=[jax.ShapeDtypeStruct((n, (h1 - h0) * HEAD_DIM), dt) for dt, h0, h1 in outs],
        compiler_params=_params("parallel"),
    )(x, gain.reshape(1, d), w, head_gain, *rope)


LOG2E = 1.4426950408889634
MXU_SCAN_MIN_CHUNK = 128


TILE_LEVELS = int(math.log2(SUBLANES))


def _gla_tables(chunk):
    n_levels = int(math.log2(chunk))
    t, s = np.arange(chunk)[:, None], np.arange(chunk)[None, :]
    mats, masks, uppers = [], [t == s], []
    for lvl in range(1, n_levels + 1):
        half = 1 << (lvl - 1)
        same_half = (t >> (lvl - 1)) == (s >> (lvl - 1))
        upper = (t & half) != 0
        if lvl <= TILE_LEVELS:
            mats.append(np.where(upper, same_half & (s <= t), same_half & (s > t)))
        masks.append((t >> lvl) == (s >> lvl))
        uppers.append(np.broadcast_to(upper, (chunk, HEAD_DIM)))
    mats += [s <= t, s > t]
    scan = np.concatenate(mats)
    if chunk < MXU_SCAN_MIN_CHUNK:
        scan = np.zeros((2 * SUBLANES, HEAD_DIM))
    return scan.astype(np.float32), np.stack(masks).astype(np.float32), np.stack(uppers).astype(np.float32)


def _gla_kernel(zq_ref, zf_ref, zi_ref, zo_ref, lb_ref, og_ref, s0_ref, scan_ref, same_ref, upper_ref,
                a_ref, sfin_ref, st_ref, *, chunk, heads, t_valid):
    c = pl.program_id(2)
    n_levels = int(math.log2(chunk))
    mxu_scan = chunk >= MXU_SCAN_MIN_CHUNK
    row = lax.broadcasted_iota(jnp.int32, (chunk, HEAD_DIM), 0)

    @pl.when(c == 0)
    def _():
        for h in range(heads):
            st_ref[h] = s0_ref[0, 0, h].T

    hs = range(heads)
    cols = [slice(h * HEAD_DIM, (h + 1) * HEAD_DIM) for h in hs]
    qs, ks, vs, log_fs = [], [], [], []
    for h in hs:
        zq, zf = zq_ref[0, :, cols[h]].astype(jnp.float32), zf_ref[0, :, cols[h]]
        lb = lb_ref[:, cols[h]]
        q = zq / (1.0 + jnp.exp(-zq))
        log_f = (jnp.minimum(zf, 0.0) - jnp.log1p(jnp.exp(-jnp.abs(zf)))
                 + jnp.log1p(lb * jnp.exp(jnp.minimum(-zf, EXP_CLAMP))))
        k = (1.0 - lb) / (1.0 + jnp.exp(zf))
        if t_valid < chunk:
            live = row < t_valid
            log_f = jnp.where(live, log_f, 0.0)
            k = jnp.where(live, k, 0.0)
        qs.append(q), ks.append(k), vs.append(zi_ref[0, :, cols[h]]), log_fs.append(log_f)

    scores = [same_ref[0] * lax.dot_general(_bf(qs[h]), _bf(ks[h]), _NT, preferred_element_type=jnp.float32)
              for h in hs]
    q_ins, k_ends, decays = [], [], []
    if mxu_scan:
        scan = scan_ref[...]
        x_alls = []
        for h in hs:
            g2 = log_fs[h] * LOG2E
            hi = _bf(g2)
            rest = g2 - hi.astype(jnp.float32)
            mid = _bf(rest)
            lo = _bf(rest - mid.astype(jnp.float32))
            x_alls.append(jnp.dot(scan, hi, preferred_element_type=jnp.float32)
                          + jnp.dot(scan, mid, preferred_element_type=jnp.float32)
                          + jnp.dot(scan, lo, preferred_element_type=jnp.float32))
        prefixes = [x[TILE_LEVELS * chunk:(TILE_LEVELS + 1) * chunk] for x in x_alls]
        for lvl in range(1, n_levels + 1):
            half = 1 << (lvl - 1)
            for h in hs:
                q, k, pre = qs[h], ks[h], prefixes[h]
                if half >= SUBLANES:
                    zeros = jnp.zeros((half, HEAD_DIM), jnp.float32)
                    q_parts, k_parts = [], []
                    for start in range(0, chunk, 2 * half):
                        low, up = slice(start, start + half), slice(start + half, start + 2 * half)
                        edge = pre[start + half - 1:start + half]
                        k_parts += [k[low] * jnp.exp2(edge - pre[low]), zeros]
                        q_parts += [zeros, q[up] * jnp.exp2(pre[up] - edge)]
                    q_l, k_l = jnp.concatenate(q_parts, axis=0), jnp.concatenate(k_parts, axis=0)
                else:
                    e = jnp.exp2(x_alls[h][(lvl - 1) * chunk:lvl * chunk])
                    e_up = e * upper_ref[lvl - 1]
                    q_l, k_l = q * e_up, k * (e - e_up)
                s_l = lax.dot_general(_bf(q_l), _bf(k_l), _NT, preferred_element_type=jnp.float32)
                scores[h] = scores[h] + same_ref[lvl] * s_l
        for h in hs:
            q_ins.append(qs[h] * jnp.exp2(prefixes[h]))
            k_ends.append(ks[h] * jnp.exp2(x_alls[h][(TILE_LEVELS + 1) * chunk:]))
            decays.append(jnp.exp2(prefixes[h][chunk - 1:chunk]))
    else:
        for h in hs:
            q, k, pre, tot = qs[h], ks[h], log_fs[h], log_fs[h]
            for lvl in range(1, n_levels + 1):
                half = 1 << (lvl - 1)
                upper = (row & half) != 0
                e = jnp.exp(jnp.where(upper, pre, tot - pre))
                q_l = jnp.where(upper, q * e, 0.0)
                k_l = jnp.where(upper, 0.0, k * e)
                s_l = lax.dot_general(_bf(q_l), _bf(k_l), _NT, preferred_element_type=jnp.float32)
                scores[h] = scores[h] + same_ref[lvl] * s_l
                tot_sib = jnp.where(upper, pltpu.roll(tot, half, 0), pltpu.roll(tot, chunk - half, 0))
                pre = pre + jnp.where(upper, tot_sib, 0.0)
                tot = tot + tot_sib
            q_ins.append(q * jnp.exp(pre)), k_ends.append(k * jnp.exp(tot - pre)), decays.append(jnp.exp(tot[0:1, :]))

    for h in hs:
        st = st_ref[h]
        o = lax.dot_general(_bf(q_ins[h]), _bf(st), _NT, preferred_element_type=jnp.float32)
        o = o + jnp.dot(_bf(scores[h]), _bf(vs[h]), preferred_element_type=jnp.float32)
        st_ref[h] = st * decays[h] + lax.dot_general(_bf(vs[h]), _bf(k_ends[h]), _TN,
                                                     preferred_element_type=jnp.float32)
        zo = zo_ref[0, :, cols[h]].astype(jnp.float32)
        a = _rms_rows(o, og_ref[:, cols[h]]) / (1.0 + jnp.exp(-zo))
        a_ref[0, :, cols[h]] = a.astype(a_ref.dtype)

    @pl.when(c == pl.num_programs(2) - 1)
    def _():
        for h in range(heads):
            sfin_ref[0, 0, h] = st_ref[h].T


def _gla(zs, lb, out_gain, states, layer, chunk, heads, t_valid):
    b, t, _ = zs[0].shape
    hw = heads * HEAD_DIM
    nh = A_HEADS // heads
    scan, same, upper = _gla_tables(chunk)
    zspec = pl.BlockSpec((1, chunk, hw), lambda bi, hi, ci: (bi, ci, hi))
    vec_spec = pl.BlockSpec((1, hw), lambda bi, hi, ci: (0, hi))
    st_block = (1, 1, heads, HEAD_DIM, HEAD_DIM)
    whole = lambda a: pl.BlockSpec(a.shape, lambda bi, hi, ci: (0,) * a.ndim)
    return pl.pallas_call(
        functools.partial(_gla_kernel, chunk=chunk, heads=heads, t_valid=t_valid),
        grid=(b, nh, t // chunk),
        in_specs=[zspec, zspec, zspec, zspec, vec_spec, vec_spec,
                  pl.BlockSpec(st_block, lambda bi, hi, ci: (layer, bi, hi, 0, 0)),
                  whole(scan), whole(same), whole(upper)],
        out_specs=[pl.BlockSpec((1, chunk, hw), lambda bi, hi, ci: (bi, ci, hi)),
                   pl.BlockSpec(st_block, lambda bi, hi, ci: (0, bi, hi, 0, 0))],
        out_shape=[jax.ShapeDtypeStruct((b, t, D_MODEL), jnp.bfloat16),
                   jax.ShapeDtypeStruct((1,) + states.shape[1:], jnp.float32)],
        scratch_shapes=[pltpu.VMEM((heads, HEAD_DIM, HEAD_DIM), jnp.float32)],
        compiler_params=_params("parallel", "parallel", "arbitrary"),
    )(*zs, lb.reshape(1, D_MODEL), out_gain.reshape(1, D_MODEL), states,
      jnp.asarray(scan, jnp.bfloat16), jnp.asarray(same), jnp.asarray(upper))


def _post_mlp_kernel(x_ref, a_ref, wp_ref, g_ref, wu_ref, wd_ref, out_ref, xn_ref):
    @pl.when(pl.program_id(1) == 0)
    def _():
        x1 = x_ref[...] + jnp.dot(a_ref[...], wp_ref[...], preferred_element_type=jnp.float32)
        out_ref[...] = x1
        xn_ref[...] = _bf(_rms_rows(x1, g_ref[...]))

    hdn = jnp.maximum(jnp.dot(xn_ref[...], wu_ref[...], preferred_element_type=jnp.float32), 0.0)
    out_ref[...] += jnp.dot(_bf(hdn * hdn), wd_ref[...], preferred_element_type=jnp.float32)


def _post_mlp(x, a, wp, gain, wu, wd, tm, tf):
    n, d = x.shape
    f = wu.shape[1]
    row_spec = pl.BlockSpec((tm, d), lambda i, j: (i, 0))
    return pl.pallas_call(
        _post_mlp_kernel,
        grid=(n // tm, f // tf),
        in_specs=[row_spec, row_spec,
                  pl.BlockSpec((d, d), lambda i, j: (0, 0)),
                  pl.BlockSpec((1, d), lambda i, j: (0, 0)),
                  pl.BlockSpec((d, tf), lambda i, j: (0, j)),
                  pl.BlockSpec((tf, d), lambda i, j: (j, 0))],
        out_specs=row_spec,
        out_shape=jax.ShapeDtypeStruct((n, d), jnp.float32),
        scratch_shapes=[pltpu.VMEM((tm, d), jnp.bfloat16)],
        compiler_params=_params("parallel", "arbitrary"),
    )(x, a, wp, gain.reshape(1, d), wu, wd)


RES = DILATIONS[-1]


def _block_pieces(group):
    pieces = RES // DILATIONS[group]
    return pieces, DIL_KEYS // pieces


def _band_bias(group):
    pieces, per = _block_pieces(group)
    i = np.arange(DIL_KEYS)
    idx = pieces * (i % per) + i // per
    k_true = np.concatenate([idx, DIL_KEYS + idx])
    delta = (DIL_KEYS + idx)[:, None] - k_true[None, :]
    band = (delta >= 0) & (delta <= DIL_KEYS)
    first = band & (np.arange(2 * DIL_KEYS) >= DIL_KEYS)[None, :]
    return np.where(np.stack([first, band]), 0.0, NEG).astype(np.float32)


def _attn_prompt_kernel(bias_ref, q_ref, kp_ref, kc_ref, vp_ref, vc_ref, *refs, pieces, per, qb, n_merge):
    p = DIL_KEYS
    others, out_refs = refs[:2 * n_merge], refs[2 * n_merge:]

    def rows(ref, j, cols, dtype=jnp.bfloat16):
        return jnp.concatenate([ref[0, c, 0, j * per:(j + 1) * per, cols] for c in range(pieces)],
                               axis=0).astype(dtype)

    def put(ref, j, cols, val):
        for c in range(pieces):
            ref[0, c, 0, j * per:(j + 1) * per, cols] = val[c * per:(c + 1) * per].astype(ref.dtype)

    lane = lax.broadcasted_iota(jnp.int32, (p, HEAD_DIM), 1)
    every = slice(None)
    for j in range(qb):
        table = jnp.minimum(pl.program_id(2), 1) if j == 0 else 1
        bias = jnp.concatenate([bias_ref[table]] * Q_PER_KV, axis=0)
        lse_all = jnp.zeros((p, HEAD_DIM), jnp.float32)
        o_heads = {}
        for kvh in range(KV_HEADS):
            kcols = slice(kvh * HEAD_DIM, (kvh + 1) * HEAD_DIM)
            if j == 0:
                k_before, v_before = rows(kp_ref, 0, kcols), rows(vp_ref, 0, kcols)
            else:
                k_before, v_before = rows(kc_ref, j - 1, kcols), rows(vc_ref, j - 1, kcols)
            keys = jnp.concatenate([k_before, rows(kc_ref, j, kcols)], axis=0)
            vals = jnp.concatenate([v_before, rows(vc_ref, j, kcols)], axis=0)
            heads = [kvh * Q_PER_KV + r for r in range(Q_PER_KV)]
            q = jnp.concatenate([rows(q_ref, j, slice(h * HEAD_DIM, (h + 1) * HEAD_DIM)) for h in heads], axis=0)
            s2 = (lax.dot_general(q, keys, _NT, preferred_element_type=jnp.float32) * (HEAD_DIM ** -0.5 * LOG2E)
                  + bias)
            m2 = jnp.max(s2, axis=-1, keepdims=True)
            pr = jnp.exp2(s2 - m2)
            den = jnp.sum(pr, axis=-1, keepdims=True)
            o = jnp.dot(_bf(pr), vals, preferred_element_type=jnp.float32) / den
            lse = m2 * (1.0 / LOG2E) + jnp.log(den)
            for r, h in enumerate(heads):
                o_heads[h] = o[r * p:(r + 1) * p]
                lse_all = jnp.where(lane == h, lse[r * p:(r + 1) * p], lse_all)
        if n_merge == 0:
            o_ref, lse_ref = out_refs
            for h in range(Q_HEADS):
                put(o_ref, j, slice(h * HEAD_DIM, (h + 1) * HEAD_DIM), o_heads[h])
            put(lse_ref, j, every, lse_all)
        else:
            (a_ref,) = out_refs
            lses = [rows(others[2 * g + 1], j, every, jnp.float32) for g in range(n_merge)] + [lse_all]
            top = functools.reduce(jnp.maximum, lses)
            w = [jnp.exp(l - top) for l in lses]
            inv = 1.0 / sum(w)
            w = [wg * inv for wg in w]
            for h in range(Q_HEADS):
                cols = slice(h * HEAD_DIM, (h + 1) * HEAD_DIM)
                os_ = [rows(others[2 * g], j, cols, jnp.float32) for g in range(n_merge)] + [o_heads[h]]
                put(a_ref, j, cols, sum(wg[:, h:h + 1] * og for wg, og in zip(w, os_)))


def _attn_prompt(q, q_col, kv, group, b, s, merge_with=(), qb=2):
    dil = DILATIONS[group]
    pieces, per = _block_pieces(group)
    lr = s // RES
    view = lambda a: a.reshape(b, pieces, dil, lr, a.shape[-1])
    kw = KV_HEADS * HEAD_DIM

    def spec(width, col):
        return pl.BlockSpec((1, pieces, 1, qb * per, width), lambda bi, r, n: (bi, 0, r, n, col))

    def before_spec(col):
        return pl.BlockSpec((1, pieces, 1, per, kw), lambda bi, r, n: (bi, 0, r, jnp.maximum(n * qb - 1, 0), col))

    band = jnp.asarray(_band_bias(group))
    other_args = [view(a) for pair in merge_with for a in pair]
    other_specs = [spec(D_MODEL, 0), spec(HEAD_DIM, 0)] * len(merge_with)
    o_shape = jax.ShapeDtypeStruct((b, pieces, dil, lr, D_MODEL), jnp.bfloat16)
    if merge_with:
        out_specs, out_shape = [spec(D_MODEL, 0)], [o_shape]
    else:
        out_specs = [spec(D_MODEL, 0), spec(HEAD_DIM, 0)]
        out_shape = [o_shape, jax.ShapeDtypeStruct((b, pieces, dil, lr, HEAD_DIM), jnp.float32)]
    outs = pl.pallas_call(
        functools.partial(_attn_prompt_kernel, pieces=pieces, per=per, qb=qb, n_merge=len(merge_with)),
        grid=(b, dil, lr // (per * qb)),
        in_specs=[pl.BlockSpec(band.shape, lambda bi, r, n: (0, 0, 0)),
                  spec(D_MODEL, q_col),
                  before_spec(2 * group), spec(kw, 2 * group),
                  before_spec(2 * group + 1), spec(kw, 2 * group + 1),
                  *other_specs],
        out_specs=out_specs,
        out_shape=out_shape,
        compiler_params=_params("parallel", "parallel", "arbitrary"),
    )(band, view(q), view(kv), view(kv), view(kv), view(kv), *other_args)
    return tuple(a.reshape(b * s, a.shape[-1]) for a in outs)


def _sample_key_blocks(n_new):
    n_cache, tables = [], []
    i = np.arange(DIL_KEYS)
    for g, (win, dil) in enumerate(zip(WINDOWS, DILATIONS)):
        n_past = min(win, PAST_LEN)
        if n_past // DIL_KEYS <= SUBLANES:
            blocks = [DIL_KEYS * k + i for k in range(n_past // DIL_KEYS)]
        else:
            assert n_past % RES == 0 and n_new <= SUBLANES <= RES and dil == RES
            groups = DIL_KEYS // SUBLANES
            blocks = [RES * (groups * k + i // SUBLANES) + i % SUBLANES for k in range(n_past // RES // groups)]
        n_cache.append(len(blocks))
        blocks.append(np.where(i < SUBLANES, n_past + i, -10 ** 9))
        for pos in blocks:
            t = np.arange(SUBLANES)[:, None]
            delta = n_past + t - pos[None, :]
            ok = (delta >= 0) & (delta % dil == 0) & (delta // dil <= DIL_KEYS)
            ok = np.where(t < n_new, ok, True)
            tables.append(ok)
    return n_cache, np.stack(tables).astype(np.float32)


def _attn_sample_kernel(q_ref, kvn_ref, c1_ref, c2_ref, c3_ref, ok_ref, a_ref, *, n_cache):
    rec = 2 * KV_HEADS * HEAD_DIM
    caches = (c1_ref, c2_ref, c3_ref)
    pad = jnp.zeros((DIL_KEYS - SUBLANES, HEAD_DIM), jnp.float32)

    def cache_rows(g, k, is_v, kvh):
        ref = caches[g]
        if len(ref.shape) == 5:
            return ref[0, k * DIL_KEYS:(k + 1) * DIL_KEYS, is_v, kvh, :]
        groups = DIL_KEYS // SUBLANES
        return ref[0, k * groups:(k + 1) * groups, :, is_v, kvh, :].reshape(DIL_KEYS, HEAD_DIM)

    for kvh in range(KV_HEADS):
        heads = [kvh * Q_PER_KV + r for r in range(Q_PER_KV)]
        outs, lses = [], []
        blk = 0
        for g in range(N_GROUPS):
            q = jnp.concatenate(
                [q_ref[0, :, (g * Q_HEADS + h) * HEAD_DIM:(g * Q_HEADS + h + 1) * HEAD_DIM] for h in heads], axis=0)
            scores, values, oks = [], [], []
            for k in range(n_cache[g] + 1):
                if k < n_cache[g]:
                    keys, vals = _bf(cache_rows(g, k, 0, kvh)), _bf(cache_rows(g, k, 1, kvh))
                else:
                    base = g * rec + kvh * HEAD_DIM
                    keys = _bf(jnp.concatenate([kvn_ref[0, :, base:base + HEAD_DIM], pad], axis=0))
                    vals = _bf(jnp.concatenate(
                        [kvn_ref[0, :, base + KV_HEADS * HEAD_DIM:base + (KV_HEADS + 1) * HEAD_DIM], pad], axis=0))
                ok = jnp.concatenate([ok_ref[blk]] * Q_PER_KV, axis=0) > 0.5
                blk += 1
                s = lax.dot_general(q, keys, _NT, preferred_element_type=jnp.float32) * HEAD_DIM ** -0.5
                scores.append(jnp.where(ok, s, NEG))
                values.append(vals)
                oks.append(ok)
            m = functools.reduce(jnp.maximum, [jnp.max(s, axis=-1, keepdims=True) for s in scores])
            probs = [jnp.where(ok, jnp.exp(s - m), 0.0) for s, ok in zip(scores, oks)]
            den = sum(jnp.sum(pr, axis=-1, keepdims=True) for pr in probs)
            o = sum(jnp.dot(_bf(pr), vals, preferred_element_type=jnp.float32) for pr, vals in zip(probs, values))
            outs.append(o / den)
            lses.append(m + jnp.log(den))
        top = functools.reduce(jnp.maximum, lses)
        w = [jnp.exp(l - top) for l in lses]
        wsum = sum(w)
        merged = sum((wg / wsum) * og for wg, og in zip(w, outs))
        for r, h in enumerate(heads):
            a_ref[0, :, h * HEAD_DIM:(h + 1) * HEAD_DIM] = merged[r * SUBLANES:(r + 1) * SUBLANES].astype(a_ref.dtype)


def _attn_sample(q, kv_new, caches, n_new):
    b = q.shape[0]
    rec = 2 * KV_HEADS * HEAD_DIM
    n_cache, table = _sample_key_blocks(n_new)
    cache_views, cache_specs = [], []
    rec_shape = (2, KV_HEADS, HEAD_DIM)
    for cache in caches:
        n_past = cache.shape[1]
        if n_past // DIL_KEYS <= SUBLANES:
            cache_views.append(cache)
            cache_specs.append(pl.BlockSpec((1, n_past) + rec_shape, lambda bi: (bi, 0, 0, 0, 0)))
        else:
            cache_views.append(cache.reshape((b, n_past // RES, RES) + rec_shape))
            cache_specs.append(pl.BlockSpec((1, n_past // RES, SUBLANES) + rec_shape, lambda bi: (bi, 0, 0, 0, 0, 0)))
    return pl.pallas_call(
        functools.partial(_attn_sample_kernel, n_cache=n_cache),
        grid=(b,),
        in_specs=[pl.BlockSpec((1, SUBLANES, Q_COLS), lambda bi: (bi, 0, 0)),
                  pl.BlockSpec((1, SUBLANES, KV_COLS), lambda bi: (bi, 0, 0)),
                  *cache_specs,
                  pl.BlockSpec(table.shape, lambda bi: (0, 0, 0))],
        out_specs=pl.BlockSpec((1, SUBLANES, D_MODEL), lambda bi: (bi, 0, 0)),
        out_shape=jax.ShapeDtypeStruct((b, SUBLANES, D_MODEL), jnp.bfloat16),
        compiler_params=_params("parallel"),
    )(q, kv_new, *cache_views, jnp.asarray(table))


def _rope_tables(pos):
    inv = ROPE_THETA ** (-2.0 * jnp.arange(ROT_HALF, dtype=jnp.float32) / ROT_DIM)
    ang = pos.astype(jnp.float32)[:, None] * inv[None, :]
    cos, sin = jnp.cos(ang), jnp.sin(ang)
    rest = HEAD_DIM - ROT_DIM
    one, zero = jnp.ones((pos.shape[0], rest), jnp.float32), jnp.zeros((pos.shape[0], rest), jnp.float32)
    zh = jnp.zeros_like(sin)
    return (jnp.concatenate([cos, cos, one], axis=1),
            jnp.concatenate([zh, sin, zero], axis=1),
            jnp.concatenate([-sin, zh, zero], axis=1))


def _to_residue_major(a, b, t):
    return a.reshape(b, t // RES, RES, a.shape[-1]).swapaxes(1, 2).reshape(b * t, a.shape[-1])


def _from_residue_major(a, b, t):
    return a.reshape(b, RES, t // RES, a.shape[-1]).swapaxes(1, 2).reshape(b * t, a.shape[-1])


def _trunk(x, b, t, pos, state0, caches, n_new, weights, lbs, chunk, gla_heads, tm, tm_proj):
    (a_norm, a_w_in, a_out_norm, a_w_out, kv_norm, w_kv, k_norm,
     b_norm, b_w_q, q_norm, b_w_o, mlp_norm, mlp_w_up, mlp_w_down) = weights
    n = b * t
    tf = 512
    fresh = caches is None
    rope = _rope_tables(pos)
    if fresh:
        rope = tuple(_to_residue_major(r, 1, t) for r in rope)
        rope_period = t // tm_proj
    else:
        rope = tuple(jnp.tile(r, (tm_proj // t, 1)) for r in rope)
        rope_period = 1
    n_heads_q = Q_COLS // HEAD_DIM
    finals = []
    kv32 = kv16 = None
    for layer in range(DEPTH):
        if layer < N_A_LAYERS:
            zs = _norm_matmul(x, a_norm[layer], a_w_in[layer], tm,
                              (jnp.bfloat16, jnp.float32, jnp.bfloat16, jnp.bfloat16))
            a, s_fin = _gla([z.reshape(b, t, D_MODEL) for z in zs], lbs[layer], a_out_norm[layer], state0, layer,
                            chunk, gla_heads, n_new)
            finals.append(s_fin)
            x = _post_mlp(x, a.reshape(n, D_MODEL), a_w_out[layer], mlp_norm[layer], mlp_w_up[layer],
                          mlp_w_down[layer], tm, tf)
            continue
        j = layer - N_A_LAYERS
        if j == 0:
            if fresh:
                x = _to_residue_major(x, b, t)
            kv_gain = jnp.repeat(jnp.repeat(k_norm, KV_HEADS, axis=0), 2, axis=0)
            n_heads_kv = KV_COLS // HEAD_DIM
            normed = tuple(bool((hh // KV_HEADS) % 2 == 0) for hh in range(n_heads_kv))
            kv32, kv16 = _proj_heads(x, kv_norm, w_kv, kv_gain, normed, rope, rope_period, tm_proj,
                                     ((jnp.float32, 0, n_heads_kv), (jnp.bfloat16, 0, n_heads_kv)))
        q_gain = jnp.repeat(q_norm[j], Q_HEADS, axis=0)
        q_normed = (True,) * n_heads_q
        if fresh:
            q_near, q_far = _proj_heads(x, b_norm[j], b_w_q[j], q_gain, q_normed, rope, rope_period, tm_proj,
                                        ((jnp.float32, 0, Q_HEADS), (jnp.bfloat16, Q_HEADS, n_heads_q)))
            near = _attn_prompt(q_near, 0, kv32, 0, b, t)
            mid = _attn_prompt(q_far, 0, kv16, 1, b, t)
            (mixer_in,) = _attn_prompt(q_far, 1, kv16, 2, b, t, merge_with=(near, mid))
        else:
            (q,) = _proj_heads(x, b_norm[j], b_w_q[j], q_gain, q_normed, rope, rope_period, tm_proj,
                               ((jnp.bfloat16, 0, n_heads_q),))
            a = _attn_sample(q.reshape(b, t, Q_COLS), kv32.reshape(b, t, KV_COLS), caches, n_new)
            mixer_in = a.reshape(n, D_MODEL)
        x = _post_mlp(x, mixer_in, b_w_o[j], mlp_norm[layer], mlp_w_up[layer], mlp_w_down[layer], tm, tf)
    if fresh:
        x = _from_residue_major(x, b, t)
        keep = WINDOWS[-1] // RES
        kv32 = kv32.reshape(b, RES, t // RES, KV_COLS)[:, :, t // RES - keep:]
        kv32 = kv32.swapaxes(1, 2).reshape(b, keep * RES, KV_COLS)
    else:
        kv32 = kv32.reshape(b, t, KV_COLS)
    return x, jnp.concatenate(finals), kv32


def _cast_kernel(w_ref, o_ref):
    o_ref[...] = w_ref[...].astype(o_ref.dtype)


def _to_bf16(w, row_block=512):
    rows, cols = math.prod(w.shape[:-1]), w.shape[-1]
    spec = pl.BlockSpec((row_block, cols), lambda i: (i, 0))
    out = pl.pallas_call(
        _cast_kernel, grid=(rows // row_block,), in_specs=[spec], out_specs=spec,
        out_shape=jax.ShapeDtypeStruct((rows, cols), jnp.bfloat16),
        compiler_params=_params("parallel"),
    )(w.reshape(rows, cols))
    return out.reshape(w.shape)


def kernel(x_prompt, x_sample, state_hgrn, cache_win1_kv, cache_win2_kv, cache_win3_kv, a_norm, a_w_in, a_lb_logits, a_out_norm, a_w_out, kv_norm, w_kv, k_norm, b_norm, b_w_q, q_norm, b_w_o, mlp_norm, mlp_w_up, mlp_w_down):
    bp, tp, d = x_prompt.shape
    bs, ts, _ = x_sample.shape
    sm = jax.nn.softmax(a_lb_logits.astype(jnp.float32), axis=0)
    lbs = jnp.cumsum(sm, axis=0) - sm[0]
    weights = (a_norm, _to_bf16(a_w_in), a_out_norm, _to_bf16(a_w_out), kv_norm, _to_bf16(w_kv), k_norm,
               b_norm, _to_bf16(b_w_q), q_norm, _to_bf16(b_w_o), mlp_norm, _to_bf16(mlp_w_up),
               _to_bf16(mlp_w_down))

    zero_state = jnp.zeros((N_A_LAYERS, bp, A_HEADS, HEAD_DIM, HEAD_DIM), jnp.float32)
    y_p, st_p, kv_p = _trunk(x_prompt.reshape(bp * tp, d), bp, tp, jnp.arange(tp), zero_state, None, 128,
                             weights, lbs, chunk=128, gla_heads=8, tm=1024, tm_proj=256)

    xs = jnp.pad(x_sample, ((0, 0), (0, SUBLANES - ts), (0, 0))).reshape(bs * SUBLANES, d)
    caches = (cache_win1_kv, cache_win2_kv, cache_win3_kv)
    y_s, st_s, kv_s = _trunk(xs, bs, SUBLANES, PAST_LEN + jnp.arange(SUBLANES), state_hgrn, caches, ts,
                             weights, lbs, chunk=SUBLANES, gla_heads=A_HEADS, tm=bs * SUBLANES,
                             tm_proj=bs * SUBLANES)

    kept, rec = kv_p.shape[1], 2 * KV_HEADS * HEAD_DIM
    kv_s = kv_s.reshape(bs, SUBLANES, N_GROUPS, 2, KV_HEADS, HEAD_DIM)[:, :ts]
    win_p = [kv_p[:, max(kept - WINDOWS[g], 0):, g * rec:(g + 1) * rec].reshape(bp, -1, 2, KV_HEADS, HEAD_DIM)
             for g in range(N_GROUPS)]
    win_s = [kv_s[:, :, g] for g in range(N_GROUPS)]
    return (y_p.reshape(bp, tp, d), y_s.reshape(bs, SUBLANES, d)[:, :ts], st_p, st_s,
            win_p[0], win_p[1], win_p[2], win_s[0], win_s[1], win_s[2])
```

```python
import functools
import math

import jax
import jax.numpy as jnp
import numpy as np
from jax import lax
from jax.experimental import pallas as pl
from jax.experimental.pallas import tpu as pltpu

D_MODEL = 1024
DEPTH = 4
N_A_LAYERS = DEPTH // 2
HEAD_DIM = 128
A_HEADS = D_MODEL // HEAD_DIM
Q_HEADS = D_MODEL // HEAD_DIM
KV_HEADS = 2
Q_PER_KV = Q_HEADS // KV_HEADS
N_GROUPS = 3
WINDOWS = (128, 512, 2048)
DILATIONS = (1, 4, 16)
DIL_KEYS = 128
ROT_DIM = HEAD_DIM // 4
ROT_HALF = ROT_DIM // 2
ROPE_THETA = 500000.0
D_FF = 4 * D_MODEL
EPS = 1e-6
NEG = -1e30
EXP_CLAMP = 80.0
PAST_LEN = 8192

KV_COLS = N_GROUPS * 2 * KV_HEADS * HEAD_DIM
Q_COLS = N_GROUPS * Q_HEADS * HEAD_DIM
SUBLANES = 8
MXU_WIDTH = 256
VMEM_LIMIT = 56 * 1024 * 1024

_NT = (((1,), (1,)), ((), ()))
_TN = (((0,), (0,)), ((), ()))


def _params(*sem):
    return pltpu.CompilerParams(dimension_semantics=sem, vmem_limit_bytes=VMEM_LIMIT)


def _bf(x):
    return x.astype(jnp.bfloat16)


def _rms_rows(x, gain):
    return x * lax.rsqrt(jnp.mean(x * x, axis=-1, keepdims=True) + EPS) * gain


def _norm_matmul_kernel(x_ref, g_ref, w_ref, *out_refs):
    xn = _bf(_rms_rows(x_ref[...], g_ref[...]))
    tn = w_ref.shape[1] // len(out_refs)
    for part, o_ref in enumerate(out_refs):
        o_ref[...] = jnp.dot(xn, w_ref[:, part * tn:(part + 1) * tn],
                             preferred_element_type=jnp.float32).astype(o_ref.dtype)


def _norm_matmul(x, gain, w, tm, out_dtypes):
    n, d = x.shape
    w, layer = w
    tn = w.shape[2] // len(out_dtypes)
    return pl.pallas_call(
        _norm_matmul_kernel,
        grid=(n // tm,),
        in_specs=[pl.BlockSpec((tm, d), lambda i: (i, 0)),
                  pl.BlockSpec((1, d), lambda i: (0, 0)),
                  pl.BlockSpec((None,) + w.shape[1:], lambda i: (layer, 0, 0))],
        out_specs=[pl.BlockSpec((tm, tn), lambda i: (i, 0)) for _ in out_dtypes],
        out_shape=[jax.ShapeDtypeStruct((n, tn), dt) for dt in out_dtypes],
        compiler_params=_params("parallel"),
    )(x, gain.reshape(1, d), w)


PROJ_HEAD_GROUP = 4


def _proj_heads_kernel(x_ref, g_ref, w_ref, hg_ref, cos_ref, sin_fwd_ref, sin_bwd_ref, *out_refs, normed, outs):
    xn = _bf(_rms_rows(x_ref[...], g_ref[...]))
    cos, sin_fwd, sin_bwd = cos_ref[...], sin_fwd_ref[...], sin_bwd_ref[...]
    width = PROJ_HEAD_GROUP * HEAD_DIM
    for g0 in range(0, len(normed), PROJ_HEAD_GROUP):
        z = jnp.dot(xn, w_ref[:, g0 * HEAD_DIM:g0 * HEAD_DIM + width], preferred_element_type=jnp.float32)
        hs = range(g0, g0 + PROJ_HEAD_GROUP)
        zs = {h: z[:, (h - g0) * HEAD_DIM:(h - g0 + 1) * HEAD_DIM] for h in hs}
        normed_hs = [h for h in hs if normed[h]]
        ms = {h: jnp.mean(zs[h] * zs[h], axis=-1, keepdims=True) for h in normed_hs}
        for h in normed_hs:
            zs[h] = zs[h] * lax.rsqrt(ms[h] + EPS) * hg_ref[h:h + 1, :]
        fwd = {h: pltpu.roll(zs[h], ROT_HALF, 1) for h in normed_hs}
        bwd = {h: pltpu.roll(zs[h], HEAD_DIM - ROT_HALF, 1) for h in normed_hs}
        for h in normed_hs:
            zs[h] = zs[h] * cos + fwd[h] * sin_fwd + bwd[h] * sin_bwd
        for h in hs:
            for o_ref, (_, h0, h1) in zip(out_refs, outs):
                if h0 <= h < h1:
                    o_ref[:, (h - h0) * HEAD_DIM:(h - h0 + 1) * HEAD_DIM] = zs[h].astype(o_ref.dtype)


def _proj_heads(x, gain, w, head_gain, normed, rope, rope_period_blocks, tm, outs):
    n, d = x.shape
    w, layer = w
    m = w.shape[2]
    rope_spec = pl.BlockSpec((tm, HEAD_DIM), lambda i: (i % rope_period_blocks, 0))
    return pl.pallas_call(
        functools.partial(_proj_heads_kernel, normed=normed, outs=outs),
        grid=(n // tm,),
        in_specs=[pl.BlockSpec((tm, d), lambda i: (i, 0)),
                  pl.BlockSpec((1, d), lambda i: (0, 0)),
                  pl.BlockSpec((None, d, m), lambda i: (layer, 0, 0)),
                  pl.BlockSpec(head_gain.shape, lambda i: (0, 0)),
                  rope_spec, rope_spec, rope_spec],
        out_specs=[pl.BlockSpec((tm, (h1 - h0) * HEAD_DIM), lambda i: (i, 0)) for _, h0, h1 in outs],
        out_shape=[jax.ShapeDtypeStruct((n, (h1 - h0) * HEAD_DIM), dt) for dt, h0, h1 in outs],
        compiler_params=_params("parallel"),
    )(x, gain.reshape(1, d), w, head_gain, *rope)


LOG2E = 1.4426950408889634
MXU_SCAN_MIN_CHUNK = 128


TILE_LEVELS = int(math.log2(SUBLANES))


def _gla_tables(chunk):
    n_levels = int(math.log2(chunk))
    t, s = np.arange(chunk)[:, None], np.arange(chunk)[None, :]
    mats, masks, uppers = [], [t == s], []
    for lvl in range(1, n_levels + 1):
        half = 1 << (lvl - 1)
        same_half = (t >> (lvl - 1)) == (s >> (lvl - 1))
        upper = (t & half) != 0
        if lvl <= TILE_LEVELS:
            mats.append(np.where(upper, same_half & (s <= t), same_half & (s > t)))
        masks.append((t >> lvl) == (s >> lvl))
        uppers.append(np.broadcast_to(upper, (chunk, HEAD_DIM)))
    mats += [s <= t, s > t]
    scan = np.concatenate(mats)
    if chunk < MXU_SCAN_MIN_CHUNK:
        scan = np.zeros((2 * SUBLANES, HEAD_DIM))
    return scan.astype(np.float32), np.stack(masks).astype(np.float32), np.stack(uppers).astype(np.float32)


def _gla_kernel(zq_ref, zf_ref, zi_ref, zo_ref, lb_ref, og_ref, s0_ref, scan_ref, same_ref, upper_ref,
                a_ref, sfin_ref, st_ref, *, chunk, heads, t_valid):
    c = pl.program_id(2)
    n_levels = int(math.log2(chunk))
    mxu_scan = chunk >= MXU_SCAN_MIN_CHUNK
    row = lax.broadcasted_iota(jnp.int32, (chunk, HEAD_DIM), 0)

    @pl.when(c == 0)
    def _():
        for h in range(heads):
            st_ref[h] = s0_ref[0, 0, h].T

    hs = range(heads)
    cols = [slice(h * HEAD_DIM, (h + 1) * HEAD_DIM) for h in hs]
    qs, ks, vs, log_fs = [], [], [], []
    for h in hs:
        zq, zf = zq_ref[0, :, cols[h]].astype(jnp.float32), zf_ref[0, :, cols[h]]
        lb = lb_ref[:, cols[h]]
        q = zq / (1.0 + jnp.exp(-zq))
        log_f = (jnp.minimum(zf, 0.0) - jnp.log1p(jnp.exp(-jnp.abs(zf)))
                 + jnp.log1p(lb * jnp.exp(jnp.minimum(-zf, EXP_CLAMP))))
        k = (1.0 - lb) / (1.0 + jnp.exp(zf))
        if t_valid < chunk:
            live = row < t_valid
            log_f = jnp.where(live, log_f, 0.0)
            k = jnp.where(live, k, 0.0)
        qs.append(q), ks.append(k), vs.append(zi_ref[0, :, cols[h]]), log_fs.append(log_f)

    scores = [same_ref[0] * lax.dot_general(_bf(qs[h]), _bf(ks[h]), _NT, preferred_element_type=jnp.float32)
              for h in hs]
    q_ins, k_ends, decays = [], [], []
    if mxu_scan:
        scan = scan_ref[...]
        x_alls = []
        for h in hs:
            g2 = log_fs[h] * LOG2E
            hi = _bf(g2)
            rest = g2 - hi.astype(jnp.float32)
            mid = _bf(rest)
            lo = _bf(rest - mid.astype(jnp.float32))
            x_alls.append(jnp.dot(scan, hi, preferred_element_type=jnp.float32)
                          + jnp.dot(scan, mid, preferred_element_type=jnp.float32)
                          + jnp.dot(scan, lo, preferred_element_type=jnp.float32))
        prefixes = [x[TILE_LEVELS * chunk:(TILE_LEVELS + 1) * chunk] for x in x_alls]
        for lvl in range(1, n_levels + 1):
            half = 1 << (lvl - 1)
            for h in hs:
                q, k, pre = qs[h], ks[h], prefixes[h]
                if half >= SUBLANES:
                    zeros = jnp.zeros((half, HEAD_DIM), jnp.float32)
                    q_parts, k_parts = [], []
                    for start in range(0, chunk, 2 * half):
                        low, up = slice(start, start + half), slice(start + half, start + 2 * half)
                        edge = pre[start + half - 1:start + half]
                        k_parts += [k[low] * jnp.exp2(edge - pre[low]), zeros]
                        q_parts += [zeros, q[up] * jnp.exp2(pre[up] - edge)]
                    q_l, k_l = jnp.concatenate(q_parts, axis=0), jnp.concatenate(k_parts, axis=0)
                else:
                    e = jnp.exp2(x_alls[h][(lvl - 1) * chunk:lvl * chunk])
                    e_up = e * upper_ref[lvl - 1]
                    q_l, k_l = q * e_up, k * (e - e_up)
                s_l = lax.dot_general(_bf(q_l), _bf(k_l), _NT, preferred_element_type=jnp.float32)
                scores[h] = scores[h] + same_ref[lvl] * s_l
        for h in hs:
            q_ins.append(qs[h] * jnp.exp2(prefixes[h]))
            k_ends.append(ks[h] * jnp.exp2(x_alls[h][(TILE_LEVELS + 1) * chunk:]))
            decays.append(jnp.exp2(prefixes[h][chunk - 1:chunk]))
    else:
        for h in hs:
            q, k, pre, tot = qs[h], ks[h], log_fs[h], log_fs[h]
            for lvl in range(1, n_levels + 1):
                half = 1 << (lvl - 1)
                upper = (row & half) != 0
                e = jnp.exp(jnp.where(upper, pre, tot - pre))
                q_l = jnp.where(upper, q * e, 0.0)
                k_l = jnp.where(upper, 0.0, k * e)
                s_l = lax.dot_general(_bf(q_l), _bf(k_l), _NT, preferred_element_type=jnp.float32)
                scores[h] = scores[h] + same_ref[lvl] * s_l
                tot_sib = jnp.where(upper, pltpu.roll(tot, half, 0), pltpu.roll(tot, chunk - half, 0))
                pre = pre + jnp.where(upper, tot_sib, 0.0)
                tot = tot + tot_sib
            q_ins.append(q * jnp.exp(pre)), k_ends.append(k * jnp.exp(tot - pre)), decays.append(jnp.exp(tot[0:1, :]))

    for h in hs:
        st = st_ref[h]
        o = lax.dot_general(_bf(q_ins[h]), _bf(st), _NT, preferred_element_type=jnp.float32)
        o = o + jnp.dot(_bf(scores[h]), _bf(vs[h]), preferred_element_type=jnp.float32)
        st_ref[h] = st * decays[h] + lax.dot_general(_bf(vs[h]), _bf(k_ends[h]), _TN,
                                                     preferred_element_type=jnp.float32)
        zo = zo_ref[0, :, cols[h]].astype(jnp.float32)
        a = _rms_rows(o, og_ref[:, cols[h]]) / (1.0 + jnp.exp(-zo))
        a_ref[0, :, cols[h]] = a.astype(a_ref.dtype)

    @pl.when(c == pl.num_programs(2) - 1)
    def _():
        for h in range(heads):
            sfin_ref[0, 0, h] = st_ref[h].T


def _gla(zs, lb, out_gain, states, layer, chunk, heads, t_valid):
    b, t, _ = zs[0].shape
    hw = heads * HEAD_DIM
    nh = A_HEADS // heads
    scan, same, upper = _gla_tables(chunk)
    zspec = pl.BlockSpec((1, chunk, hw), lambda bi, hi, ci: (bi, ci, hi))
    vec_spec = pl.BlockSpec((1, hw), lambda bi, hi, ci: (0, hi))
    st_block = (1, 1, heads, HEAD_DIM, HEAD_DIM)
    whole = lambda a: pl.BlockSpec(a.shape, lambda bi, hi, ci: (0,) * a.ndim)
    return pl.pallas_call(
        functools.partial(_gla_kernel, chunk=chunk, heads=heads, t_valid=t_valid),
        grid=(b, nh, t // chunk),
        in_specs=[zspec, zspec, zspec, zspec, vec_spec, vec_spec,
                  pl.BlockSpec(st_block, lambda bi, hi, ci: (layer, bi, hi, 0, 0)),
                  whole(scan), whole(same), whole(upper)],
        out_specs=[pl.BlockSpec((1, chunk, hw), lambda bi, hi, ci: (bi, ci, hi)),
                   pl.BlockSpec(st_block, lambda bi, hi, ci: (0, bi, hi, 0, 0))],
        out_shape=[jax.ShapeDtypeStruct((b, t, D_MODEL), jnp.bfloat16),
                   jax.ShapeDtypeStruct((1,) + states.shape[1:], jnp.float32)],
        scratch_shapes=[pltpu.VMEM((heads, HEAD_DIM, HEAD_DIM), jnp.float32)],
        compiler_params=_params("parallel", "parallel", "arbitrary"),
    )(*zs, lb.reshape(1, D_MODEL), out_gain.reshape(1, D_MODEL), states,
      jnp.asarray(scan, jnp.bfloat16), jnp.asarray(same), jnp.asarray(upper))


def _post_mlp_kernel(x_ref, a_ref, wp_ref, g_ref, wu_ref, wd_ref, out_ref, xn_ref):
    @pl.when(pl.program_id(1) == 0)
    def _():
        x1 = x_ref[...] + jnp.dot(a_ref[...], wp_ref[...], preferred_element_type=jnp.float32)
        out_ref[...] = x1
        xn_ref[...] = _bf(_rms_rows(x1, g_ref[...]))

    hdn = jnp.maximum(jnp.dot(xn_ref[...], wu_ref[...], preferred_element_type=jnp.float32), 0.0)
    out_ref[...] += jnp.dot(_bf(hdn * hdn), wd_ref[...], preferred_element_type=jnp.float32)


def _post_mlp(x, a, wp, gain, wu, wd, tm, tf):
    n, d = x.shape
    (wp, lp), (wu, lu), (wd, ld) = wp, wu, wd
    f = wu.shape[2]
    row_spec = pl.BlockSpec((tm, d), lambda i, j: (i, 0))
    return pl.pallas_call(
        _post_mlp_kernel,
        grid=(n // tm, f // tf),
        in_specs=[row_spec, row_spec,
                  pl.BlockSpec((None, d, d), lambda i, j: (lp, 0, 0)),
                  pl.BlockSpec((1, d), lambda i, j: (0, 0)),
                  pl.BlockSpec((None, d, tf), lambda i, j: (lu, 0, j)),
                  pl.BlockSpec((None, tf, d), lambda i, j: (ld, j, 0))],
        out_specs=row_spec,
        out_shape=jax.ShapeDtypeStruct((n, d), jnp.float32),
        scratch_shapes=[pltpu.VMEM((tm, d), jnp.bfloat16)],
        compiler_params=_params("parallel", "arbitrary"),
    )(x, a, wp, gain.reshape(1, d), wu, wd)


RES = DILATIONS[-1]


def _block_pieces(group):
    pieces = RES // DILATIONS[group]
    return pieces, DIL_KEYS // pieces


def _band_bias(group):
    pieces, per = _block_pieces(group)
    i = np.arange(DIL_KEYS)
    idx = pieces * (i % per) + i // per
    k_true = np.concatenate([idx, DIL_KEYS + idx])
    delta = (DIL_KEYS + idx)[:, None] - k_true[None, :]
    band = (delta >= 0) & (delta <= DIL_KEYS)
    first = band & (np.arange(2 * DIL_KEYS) >= DIL_KEYS)[None, :]
    return np.where(np.stack([first, band]), 0.0, NEG).astype(np.float32)


def _attn_prompt_kernel(bias_ref, q_ref, kp_ref, kc_ref, vp_ref, vc_ref, *refs, pieces, per, qb, n_merge):
    p = DIL_KEYS
    others, out_refs = refs[:2 * n_merge], refs[2 * n_merge:]

    def rows(ref, j, cols, dtype=jnp.bfloat16):
        return jnp.concatenate([ref[0, c, 0, j * per:(j + 1) * per, cols] for c in range(pieces)],
                               axis=0).astype(dtype)

    def put(ref, j, cols, val):
        for c in range(pieces):
            ref[0, c, 0, j * per:(j + 1) * per, cols] = val[c * per:(c + 1) * per].astype(ref.dtype)

    lane = lax.broadcasted_iota(jnp.int32, (p, HEAD_DIM), 1)
    every = slice(None)
    for j in range(qb):
        table = jnp.minimum(pl.program_id(2), 1) if j == 0 else 1
        bias = jnp.concatenate([bias_ref[table]] * Q_PER_KV, axis=0)
        lse_all = jnp.zeros((p, HEAD_DIM), jnp.float32)
        o_heads = {}
        for kvh in range(KV_HEADS):
            kcols = slice(kvh * HEAD_DIM, (kvh + 1) * HEAD_DIM)
            if j == 0:
                k_before, v_before = rows(kp_ref, 0, kcols), rows(vp_ref, 0, kcols)
            else:
                k_before, v_before = rows(kc_ref, j - 1, kcols), rows(vc_ref, j - 1, kcols)
            keys = jnp.concatenate([k_before, rows(kc_ref, j, kcols)], axis=0)
            vals = jnp.concatenate([v_before, rows(vc_ref, j, kcols)], axis=0)
            heads = [kvh * Q_PER_KV + r for r in range(Q_PER_KV)]
            q = jnp.concatenate([rows(q_ref, j, slice(h * HEAD_DIM, (h + 1) * HEAD_DIM)) for h in heads], axis=0)
            s2 = (lax.dot_general(q, keys, _NT, preferred_element_type=jnp.float32) * (HEAD_DIM ** -0.5 * LOG2E)
                  + bias)
            m2 = jnp.max(s2, axis=-1, keepdims=True)
            pr = jnp.exp2(s2 - m2)
            den = jnp.sum(pr, axis=-1, keepdims=True)
            o = jnp.dot(_bf(pr), vals, preferred_element_type=jnp.float32) / den
            lse = m2 * (1.0 / LOG2E) + jnp.log(den)
            for r, h in enumerate(heads):
                o_heads[h] = o[r * p:(r + 1) * p]
                lse_all = jnp.where(lane == h, lse[r * p:(r + 1) * p], lse_all)
        if n_merge == 0:
            o_ref, lse_ref = out_refs
            for h in range(Q_HEADS):
                put(o_ref, j, slice(h * HEAD_DIM, (h + 1) * HEAD_DIM), o_heads[h])
            put(lse_ref, j, every, lse_all)
        else:
            (a_ref,) = out_refs
            lses = [rows(others[2 * g + 1], j, every, jnp.float32) for g in range(n_merge)] + [lse_all]
            top = functools.reduce(jnp.maximum, lses)
            w = [jnp.exp(l - top) for l in lses]
            inv = 1.0 / sum(w)
            w = [wg * inv for wg in w]
            for h in range(Q_HEADS):
                cols = slice(h * HEAD_DIM, (h + 1) * HEAD_DIM)
                os_ = [rows(others[2 * g], j, cols, jnp.float32) for g in range(n_merge)] + [o_heads[h]]
                put(a_ref, j, cols, sum(wg[:, h:h + 1] * og for wg, og in zip(w, os_)))


def _attn_prompt(q, q_col, kv, group, b, s, merge_with=(), qb=4):
    dil = DILATIONS[group]
    pieces, per = _block_pieces(group)
    lr = s // RES
    qb = min(qb, lr // per)
    view = lambda a: a.reshape(b, pieces, dil, lr, a.shape[-1])
    kw = KV_HEADS * HEAD_DIM

    def spec(width, col):
        return pl.BlockSpec((1, pieces, 1, qb * per, width), lambda bi, r, n: (bi, 0, r, n, col))

    def before_spec(col):
        return pl.BlockSpec((1, pieces, 1, per, kw), lambda bi, r, n: (bi, 0, r, jnp.maximum(n * qb - 1, 0), col))

    band = jnp.asarray(_band_bias(group))
    other_args = [view(a) for pair in merge_with for a in pair]
    other_specs = [spec(D_MODEL, 0), spec(HEAD_DIM, 0)] * len(merge_with)
    o_shape = jax.ShapeDtypeStruct((b, pieces, dil, lr, D_MODEL), jnp.bfloat16)
    if merge_with:
        out_specs, out_shape = [spec(D_MODEL, 0)], [o_shape]
    else:
        out_specs = [spec(D_MODEL, 0), spec(HEAD_DIM, 0)]
        out_shape = [o_shape, jax.ShapeDtypeStruct((b, pieces, dil, lr, HEAD_DIM), jnp.float32)]
    outs = pl.pallas_call(
        functools.partial(_attn_prompt_kernel, pieces=pieces, per=per, qb=qb, n_merge=len(merge_with)),
        grid=(b, dil, lr // (per * qb)),
        in_specs=[pl.BlockSpec(band.shape, lambda bi, r, n: (0, 0, 0)),
                  spec(D_MODEL, q_col),
                  before_spec(2 * group), spec(kw, 2 * group),
                  before_spec(2 * group + 1), spec(kw, 2 * group + 1),
                  *other_specs],
        out_specs=out_specs,
        out_shape=out_shape,
        compiler_params=_params("parallel", "parallel", "arbitrary"),
    )(band, view(q), view(kv), view(kv), view(kv), view(kv), *other_args)
    return tuple(a.reshape(b * s, a.shape[-1]) for a in outs)


def _sample_key_blocks(n_new):
    n_cache, tables = [], []
    i = np.arange(DIL_KEYS)
    for g, (win, dil) in enumerate(zip(WINDOWS, DILATIONS)):
        n_past = min(win, PAST_LEN)
        if n_past // DIL_KEYS <= SUBLANES:
            blocks = [DIL_KEYS * k + i for k in range(n_past // DIL_KEYS)]
        else:
            assert n_past % RES == 0 and n_new <= SUBLANES <= RES and dil == RES
            groups = DIL_KEYS // SUBLANES
            blocks = [RES * (groups * k + i // SUBLANES) + i % SUBLANES for k in range(n_past // RES // groups)]
        n_cache.append(len(blocks))
        blocks.append(np.where(i < SUBLANES, n_past + i, -10 ** 9))
        for pos in blocks:
            t = np.arange(SUBLANES)[:, None]
            delta = n_past + t - pos[None, :]
            ok = (delta >= 0) & (delta % dil == 0) & (delta // dil <= DIL_KEYS)
            ok = np.where(t < n_new, ok, True)
            tables.append(ok)
    return n_cache, np.stack(tables).astype(np.float32)


def _attn_sample_kernel(q_ref, kvn_ref, c1_ref, c2_ref, c3_ref, ok_ref, a_ref, *, n_cache):
    rec = 2 * KV_HEADS * HEAD_DIM
    caches = (c1_ref, c2_ref, c3_ref)
    pad = jnp.zeros((DIL_KEYS - SUBLANES, HEAD_DIM), jnp.float32)

    def cache_rows(g, k, is_v, kvh):
        ref = caches[g]
        if len(ref.shape) == 5:
            return ref[0, k * DIL_KEYS:(k + 1) * DIL_KEYS, is_v, kvh, :]
        groups = DIL_KEYS // SUBLANES
        return ref[0, k * groups:(k + 1) * groups, :, is_v, kvh, :].reshape(DIL_KEYS, HEAD_DIM)

    for kvh in range(KV_HEADS):
        heads = [kvh * Q_PER_KV + r for r in range(Q_PER_KV)]
        outs, lses = [], []
        blk = 0
        for g in range(N_GROUPS):
            q = jnp.concatenate(
                [q_ref[0, :, (g * Q_HEADS + h) * HEAD_DIM:(g * Q_HEADS + h + 1) * HEAD_DIM] for h in heads], axis=0)
            scores, values, oks = [], [], []
            for k in range(n_cache[g] + 1):
                if k < n_cache[g]:
                    keys, vals = _bf(cache_rows(g, k, 0, kvh)), _bf(cache_rows(g, k, 1, kvh))
                else:
                    base = g * rec + kvh * HEAD_DIM
                    keys = _bf(jnp.concatenate([kvn_ref[0, :, base:base + HEAD_DIM], pad], axis=0))
                    vals = _bf(jnp.concatenate(
                        [kvn_ref[0, :, base + KV_HEADS * HEAD_DIM:base + (KV_HEADS + 1) * HEAD_DIM], pad], axis=0))
                ok = jnp.concatenate([ok_ref[blk]] * Q_PER_KV, axis=0) > 0.5
                blk += 1
                s = lax.dot_general(q, keys, _NT, preferred_element_type=jnp.float32) * HEAD_DIM ** -0.5
                scores.append(jnp.where(ok, s, NEG))
                values.append(vals)
                oks.append(ok)
            m = functools.reduce(jnp.maximum, [jnp.max(s, axis=-1, keepdims=True) for s in scores])
            probs = [jnp.where(ok, jnp.exp(s - m), 0.0) for s, ok in zip(scores, oks)]
            den = sum(jnp.sum(pr, axis=-1, keepdims=True) for pr in probs)
            o = sum(jnp.dot(_bf(pr), vals, preferred_element_type=jnp.float32) for pr, vals in zip(probs, values))
            outs.append(o / den)
            lses.append(m + jnp.log(den))
        top = functools.reduce(jnp.maximum, lses)
        w = [jnp.exp(l - top) for l in lses]
        wsum = sum(w)
        merged = sum((wg / wsum) * og for wg, og in zip(w, outs))
        for r, h in enumerate(heads):
            a_ref[0, :, h * HEAD_DIM:(h + 1) * HEAD_DIM] = merged[r * SUBLANES:(r + 1) * SUBLANES].astype(a_ref.dtype)


def _attn_sample(q, kv_new, caches, n_new):
    b = q.shape[0]
    rec = 2 * KV_HEADS * HEAD_DIM
    n_cache, table = _sample_key_blocks(n_new)
    cache_views, cache_specs = [], []
    rec_shape = (2, KV_HEADS, HEAD_DIM)
    for cache in caches:
        n_past = cache.shape[1]
        if n_past // DIL_KEYS <= SUBLANES:
            cache_views.append(cache)
            cache_specs.append(pl.BlockSpec((1, n_past) + rec_shape, lambda bi: (bi, 0, 0, 0, 0)))
        else:
            cache_views.append(cache.reshape((b, n_past // RES, RES) + rec_shape))
            cache_specs.append(pl.BlockSpec((1, n_past // RES, SUBLANES) + rec_shape, lambda bi: (bi, 0, 0, 0, 0, 0)))
    return pl.pallas_call(
        functools.partial(_attn_sample_kernel, n_cache=n_cache),
        grid=(b,),
        in_specs=[pl.BlockSpec((1, SUBLANES, Q_COLS), lambda bi: (bi, 0, 0)),
                  pl.BlockSpec((1, SUBLANES, KV_COLS), lambda bi: (bi, 0, 0)),
                  *cache_specs,
                  pl.BlockSpec(table.shape, lambda bi: (0, 0, 0))],
        out_specs=pl.BlockSpec((1, SUBLANES, D_MODEL), lambda bi: (bi, 0, 0)),
        out_shape=jax.ShapeDtypeStruct((b, SUBLANES, D_MODEL), jnp.bfloat16),
        compiler_params=_params("parallel"),
    )(q, kv_new, *cache_views, jnp.asarray(table))


def _rope_tables(pos):
    inv = ROPE_THETA ** (-2.0 * jnp.arange(ROT_HALF, dtype=jnp.float32) / ROT_DIM)
    ang = pos.astype(jnp.float32)[:, None] * inv[None, :]
    cos, sin = jnp.cos(ang), jnp.sin(ang)
    rest = HEAD_DIM - ROT_DIM
    one, zero = jnp.ones((pos.shape[0], rest), jnp.float32), jnp.zeros((pos.shape[0], rest), jnp.float32)
    zh = jnp.zeros_like(sin)
    return (jnp.concatenate([cos, cos, one], axis=1),
            jnp.concatenate([zh, sin, zero], axis=1),
            jnp.concatenate([-sin, zh, zero], axis=1))


def _to_residue_major(a, b, t):
    return a.reshape(b, t // RES, RES, a.shape[-1]).swapaxes(1, 2).reshape(b * t, a.shape[-1])


def _from_residue_major(a, b, t):
    return a.reshape(b, RES, t // RES, a.shape[-1]).swapaxes(1, 2).reshape(b * t, a.shape[-1])


def _trunk(x, b, t, pos, state0, caches, n_new, weights, lbs, chunk, gla_heads, tm, tm_proj):
    (a_norm, a_w_in, a_out_norm, a_w_out, kv_norm, w_kv, k_norm,
     b_norm, b_w_q, q_norm, b_w_o, mlp_norm, mlp_w_up, mlp_w_down) = weights
    n = b * t
    tf = 1024
    fresh = caches is None
    rope = _rope_tables(pos)
    if fresh:
        rope = tuple(_to_residue_major(r, 1, t) for r in rope)
        rope_period = t // tm_proj
    else:
        rope = tuple(jnp.tile(r, (tm_proj // t, 1)) for r in rope)
        rope_period = 1
    n_heads_q = Q_COLS // HEAD_DIM
    finals = []
    kv32 = kv16 = None
    for layer in range(DEPTH):
        if layer < N_A_LAYERS:
            zs = _norm_matmul(x, a_norm[layer], (a_w_in, layer), tm,
                              (jnp.bfloat16, jnp.float32, jnp.bfloat16, jnp.bfloat16))
            a, s_fin = _gla([z.reshape(b, t, D_MODEL) for z in zs], lbs[layer], a_out_norm[layer], state0, layer,
                            chunk, gla_heads, n_new)
            finals.append(s_fin)
            x = _post_mlp(x, a.reshape(n, D_MODEL), (a_w_out, layer), mlp_norm[layer], (mlp_w_up, layer),
                          (mlp_w_down, layer), tm, tf)
            continue
        j = layer - N_A_LAYERS
        if j == 0:
            if fresh:
                x = _to_residue_major(x, b, t)
            kv_gain = jnp.repeat(jnp.repeat(k_norm, KV_HEADS, axis=0), 2, axis=0)
            n_heads_kv = KV_COLS // HEAD_DIM
            normed = tuple(bool((hh // KV_HEADS) % 2 == 0) for hh in range(n_heads_kv))
            kv32, kv16 = _proj_heads(x, kv_norm, (w_kv[None], 0), kv_gain, normed, rope, rope_period, tm_proj,
                                     ((jnp.float32, 0, n_heads_kv), (jnp.bfloat16, 0, n_heads_kv)))
        q_gain = jnp.repeat(q_norm[j], Q_HEADS, axis=0)
        q_normed = (True,) * n_heads_q
        if fresh:
            q_near, q_far = _proj_heads(x, b_norm[j], (b_w_q, j), q_gain, q_normed, rope, rope_period, tm_proj,
                                        ((jnp.float32, 0, Q_HEADS), (jnp.bfloat16, Q_HEADS, n_heads_q)))
            near = _attn_prompt(q_near, 0, kv32, 0, b, t)
            mid = _attn_prompt(q_far, 0, kv16, 1, b, t)
            (mixer_in,) = _attn_prompt(q_far, 1, kv16, 2, b, t, merge_with=(near, mid))
        else:
            (q,) = _proj_heads(x, b_norm[j], (b_w_q, j), q_gain, q_normed, rope, rope_period, tm_proj,
                               ((jnp.bfloat16, 0, n_heads_q),))
            a = _attn_sample(q.reshape(b, t, Q_COLS), kv32.reshape(b, t, KV_COLS), caches, n_new)
            mixer_in = a.reshape(n, D_MODEL)
        x = _post_mlp(x, mixer_in, (b_w_o, j), mlp_norm[layer], (mlp_w_up, layer), (mlp_w_down, layer), tm, tf)
    if fresh:
        x = _from_residue_major(x, b, t)
        keep = WINDOWS[-1] // RES
        kv32 = kv32.reshape(b, RES, t // RES, KV_COLS)[:, :, t // RES - keep:]
        kv32 = kv32.swapaxes(1, 2).reshape(b, keep * RES, KV_COLS)
    else:
        kv32 = kv32.reshape(b, t, KV_COLS)
    return x, jnp.concatenate(finals), kv32


def _cast_kernel(w_ref, o_ref):
    o_ref[...] = w_ref[...].astype(o_ref.dtype)


def _to_bf16(w, row_block=512):
    rows, cols = math.prod(w.shape[:-1]), w.shape[-1]
    spec = pl.BlockSpec((row_block, cols), lambda i: (i, 0))
    out = pl.pallas_call(
        _cast_kernel, grid=(rows // row_block,), in_specs=[spec], out_specs=spec,
        out_shape=jax.ShapeDtypeStruct((rows, cols), jnp.bfloat16),
        compiler_params=_params("parallel"),
    )(w.reshape(rows, cols))
    return out.reshape(w.shape)


def kernel(x_prompt, x_sample, state_hgrn, cache_win1_kv, cache_win2_kv, cache_win3_kv, a_norm, a_w_in, a_lb_logits, a_out_norm, a_w_out, kv_norm, w_kv, k_norm, b_norm, b_w_q, q_norm, b_w_o, mlp_norm, mlp_w_up, mlp_w_down):
    bp, tp, d = x_prompt.shape
    bs, ts, _ = x_sample.shape
    sm = jax.nn.softmax(a_lb_logits.astype(jnp.float32), axis=0)
    lbs = jnp.cumsum(sm, axis=0) - sm[0]
    weights = (a_norm, _to_bf16(a_w_in), a_out_norm, _to_bf16(a_w_out), kv_norm, _to_bf16(w_kv), k_norm,
               b_norm, _to_bf16(b_w_q), q_norm, _to_bf16(b_w_o), mlp_norm, _to_bf16(mlp_w_up),
               _to_bf16(mlp_w_down))

    zero_state = jnp.zeros((N_A_LAYERS, bp, A_HEADS, HEAD_DIM, HEAD_DIM), jnp.float32)
    y_p, st_p, kv_p = _trunk(x_prompt.reshape(bp * tp, d), bp, tp, jnp.arange(tp), zero_state, None, 128,
                             weights, lbs, chunk=128, gla_heads=8, tm=1024, tm_proj=256)

    xs = jnp.pad(x_sample, ((0, 0), (0, SUBLANES - ts), (0, 0))).reshape(bs * SUBLANES, d)
    caches = (cache_win1_kv, cache_win2_kv, cache_win3_kv)
    y_s, st_s, kv_s = _trunk(xs, bs, SUBLANES, PAST_LEN + jnp.arange(SUBLANES), state_hgrn, caches, ts,
                             weights, lbs, chunk=SUBLANES, gla_heads=A_HEADS, tm=bs * SUBLANES,
                             tm_proj=bs * SUBLANES)

    kept, rec = kv_p.shape[1], 2 * KV_HEADS * HEAD_DIM
    kv_s = kv_s.reshape(bs, SUBLANES, N_GROUPS, 2, KV_HEADS, HEAD_DIM)[:, :ts]
    win_p = [kv_p[:, max(kept - WINDOWS[g], 0):, g * rec:(g + 1) * rec].reshape(bp, -1, 2, KV_HEADS, HEAD_DIM)
             for g in range(N_GROUPS)]
    win_s = [kv_s[:, :, g] for g in range(N_GROUPS)]
    return (y_p.reshape(bp, tp, d), y_s.reshape(bs, SUBLANES, d)[:, :ts], st_p, st_s,
            win_p[0], win_p[1], win_p[2], win_s[0], win_s[1], win_s[2])
```

```python
import functools
import math

import jax
import jax.numpy as jnp
import numpy as np
from jax import lax
from jax.experimental import pallas as pl
from jax.experimental.pallas import tpu as pltpu

D_MODEL = 1024
DEPTH = 4
N_A_LAYERS = DEPTH // 2
HEAD_DIM = 128
A_HEADS = D_MODEL // HEAD_DIM
Q_HEADS = D_MODEL // HEAD_DIM
KV_HEADS = 2
Q_PER_KV = Q_HEADS // KV_HEADS
N_GROUPS = 3
WINDOWS = (128, 512, 2048)
DILATIONS = (1, 4, 16)
DIL_KEYS = 128
ROT_DIM = HEAD_DIM // 4
ROT_HALF = ROT_DIM // 2
ROPE_THETA = 500000.0
D_FF = 4 * D_MODEL
EPS = 1e-6
NEG = -1e30
EXP_CLAMP = 80.0
PAST_LEN = 8192

KV_COLS = N_GROUPS * 2 * KV_HEADS * HEAD_DIM
Q_COLS = N_GROUPS * Q_HEADS * HEAD_DIM
SUBLANES = 8
MXU_WIDTH = 256
VMEM_LIMIT = 56 * 1024 * 1024

_NT = (((1,), (1,)), ((), ()))
_TN = (((0,), (0,)), ((), ()))


def _params(*sem):
    return pltpu.CompilerParams(dimension_semantics=sem, vmem_limit_bytes=VMEM_LIMIT)


def _bf(x):
    return x.astype(jnp.bfloat16)


def _rms_rows(x, gain):
    return x * lax.rsqrt(jnp.mean(x * x, axis=-1, keepdims=True) + EPS) * gain


def _norm_matmul_kernel(x_ref, g_ref, w_ref, *out_refs):
    xn = _bf(_rms_rows(x_ref[...], g_ref[...]))
    tn = w_ref.shape[1] // len(out_refs)
    for part, o_ref in enumerate(out_refs):
        o_ref[...] = jnp.dot(xn, w_ref[:, part * tn:(part + 1) * tn],
                             preferred_element_type=jnp.float32).astype(o_ref.dtype)


def _norm_matmul(x, gain, w, tm, out_dtypes):
    n, d = x.shape
    w, layer = w
    tn = w.shape[2] // len(out_dtypes)
    return pl.pallas_call(
        _norm_matmul_kernel,
        grid=(n // tm,),
        in_specs=[pl.BlockSpec((tm, d), lambda i: (i, 0)),
                  pl.BlockSpec((1, d), lambda i: (0, 0)),
                  pl.BlockSpec((None,) + w.shape[1:], lambda i: (layer, 0, 0))],
        out_specs=[pl.BlockSpec((tm, tn), lambda i: (i, 0)) for _ in out_dtypes],
        out_shape=[jax.ShapeDtypeStruct((n, tn), dt) for dt in out_dtypes],
        compiler_params=_params("parallel"),
    )(x, gain.reshape(1, d), w)


PROJ_HEAD_GROUP = 4


def _proj_heads_kernel(x_ref, g_ref, w_ref, hg_ref, *refs, normed, outs):
    rope, out_refs = [r[...] for r in refs[:len(refs) - len(outs)]], refs[len(refs) - len(outs):]
    xn = _bf(_rms_rows(x_ref[...], g_ref[...]))
    width = PROJ_HEAD_GROUP * HEAD_DIM
    for g0 in range(0, len(normed), PROJ_HEAD_GROUP):
        z = jnp.dot(xn, w_ref[:, g0 * HEAD_DIM:g0 * HEAD_DIM + width], preferred_element_type=jnp.float32)
        hs = range(g0, g0 + PROJ_HEAD_GROUP)
        zs = {h: z[:, (h - g0) * HEAD_DIM:(h - g0 + 1) * HEAD_DIM] for h in hs}
        normed_hs = [h for h in hs if normed[h]]
        ms = {h: jnp.mean(zs[h] * zs[h], axis=-1, keepdims=True) for h in normed_hs}
        for h in normed_hs:
            zs[h] = zs[h] * lax.rsqrt(ms[h] + EPS) * hg_ref[h:h + 1, :]
        if len(rope) == 3:
            cos, sin_fwd, sin_bwd = rope
            fwd = {h: pltpu.roll(zs[h], ROT_HALF, 1) for h in normed_hs}
            bwd = {h: pltpu.roll(zs[h], HEAD_DIM - ROT_HALF, 1) for h in normed_hs}
            for h in normed_hs:
                zs[h] = zs[h] * cos + fwd[h] * sin_fwd + bwd[h] * sin_bwd
        else:
            cos, sin = rope
            partner = {h: pltpu.roll(zs[h], HEAD_DIM // 2, 1) for h in normed_hs}
            for h in normed_hs:
                zs[h] = zs[h] * cos + partner[h] * sin
        for h in hs:
            for o_ref, (_, h0, h1) in zip(out_refs, outs):
                if h0 <= h < h1:
                    o_ref[:, (h - h0) * HEAD_DIM:(h - h0 + 1) * HEAD_DIM] = zs[h].astype(o_ref.dtype)


def _proj_heads(x, gain, w, head_gain, normed, rope, rope_period_blocks, tm, outs):
    n, d = x.shape
    w, layer = w
    m = w.shape[2]
    rope_spec = pl.BlockSpec((tm, HEAD_DIM), lambda i: (i % rope_period_blocks, 0))
    return pl.pallas_call(
        functools.partial(_proj_heads_kernel, normed=normed, outs=outs),
        grid=(n // tm,),
        in_specs=[pl.BlockSpec((tm, d), lambda i: (i, 0)),
                  pl.BlockSpec((1, d), lambda i: (0, 0)),
                  pl.BlockSpec((None, d, m), lambda i: (layer, 0, 0)),
                  pl.BlockSpec(head_gain.shape, lambda i: (0, 0)),
                  *[rope_spec] * len(rope)],
        out_specs=[pl.BlockSpec((tm, (h1 - h0) * HEAD_DIM), lambda i: (i, 0)) for _, h0, h1 in outs],
        out_shape=[jax.ShapeDtypeStruct((n, (h1 - h0) * HEAD_DIM), dt) for dt, h0, h1 in outs],
        compiler_params=_params("parallel"),
    )(x, gain.reshape(1, d), w, head_gain, *rope)


LOG2E = 1.4426950408889634
MXU_SCAN_MIN_CHUNK = 128


TILE_LEVELS = int(math.log2(SUBLANES))


def _gla_tables(chunk):
    n_levels = int(math.log2(chunk))
    t, s = np.arange(chunk)[:, None], np.arange(chunk)[None, :]
    mats, masks, uppers = [], [t == s], []
    for lvl in range(1, n_levels + 1):
        half = 1 << (lvl - 1)
        same_half = (t >> (lvl - 1)) == (s >> (lvl - 1))
        upper = (t & half) != 0
        if lvl <= TILE_LEVELS:
            mats.append(np.where(upper, same_half & (s <= t), same_half & (s > t)))
        masks.append((t >> lvl) == (s >> lvl))
        uppers.append(np.broadcast_to(upper, (chunk, HEAD_DIM)))
    mats += [s <= t, s > t]
    scan = np.concatenate(mats)
    if chunk < MXU_SCAN_MIN_CHUNK:
        scan = np.zeros((2 * SUBLANES, HEAD_DIM))
    return scan.astype(np.float32), np.stack(masks).astype(np.float32), np.stack(uppers).astype(np.float32)


def _gla_kernel(zq_ref, zf_ref, zi_ref, zo_ref, lb_ref, og_ref, s0_ref, scan_ref, same_ref, upper_ref,
                a_ref, sfin_ref, st_ref, *, chunk, heads, t_valid):
    c = pl.program_id(2)
    n_levels = int(math.log2(chunk))
    mxu_scan = chunk >= MXU_SCAN_MIN_CHUNK
    row = lax.broadcasted_iota(jnp.int32, (chunk, HEAD_DIM), 0)

    @pl.when(c == 0)
    def _():
        for h in range(heads):
            st_ref[h] = s0_ref[0, 0, h].T

    hs = range(heads)
    cols = [slice(h * HEAD_DIM, (h + 1) * HEAD_DIM) for h in hs]
    qs, ks, vs, log_fs = [], [], [], []
    for h in hs:
        zq, zf = zq_ref[0, :, cols[h]].astype(jnp.float32), zf_ref[0, :, cols[h]]
        lb = lb_ref[:, cols[h]]
        q = zq / (1.0 + jnp.exp(-zq))
        log_f = (jnp.minimum(zf, 0.0) - jnp.log1p(jnp.exp(-jnp.abs(zf)))
                 + jnp.log1p(lb * jnp.exp(jnp.minimum(-zf, EXP_CLAMP))))
        k = (1.0 - lb) / (1.0 + jnp.exp(zf))
        if t_valid < chunk:
            live = row < t_valid
            log_f = jnp.where(live, log_f, 0.0)
            k = jnp.where(live, k, 0.0)
        qs.append(q), ks.append(k), vs.append(zi_ref[0, :, cols[h]]), log_fs.append(log_f)

    scores = [same_ref[0] * lax.dot_general(_bf(qs[h]), _bf(ks[h]), _NT, preferred_element_type=jnp.float32)
              for h in hs]
    q_ins, k_ends, decays = [], [], []
    if mxu_scan:
        scan = scan_ref[...]
        x_alls = []
        for h in hs:
            g2 = log_fs[h] * LOG2E
            hi = _bf(g2)
            rest = g2 - hi.astype(jnp.float32)
            mid = _bf(rest)
            lo = _bf(rest - mid.astype(jnp.float32))
            x_alls.append(jnp.dot(scan, hi, preferred_element_type=jnp.float32)
                          + jnp.dot(scan, mid, preferred_element_type=jnp.float32)
                          + jnp.dot(scan, lo, preferred_element_type=jnp.float32))
        prefixes = [x[TILE_LEVELS * chunk:(TILE_LEVELS + 1) * chunk] for x in x_alls]
        for lvl in range(1, n_levels + 1):
            half = 1 << (lvl - 1)
            for h in hs:
                q, k, pre = qs[h], ks[h], prefixes[h]
                if half >= SUBLANES:
                    zeros = jnp.zeros((half, HEAD_DIM), jnp.float32)
                    q_parts, k_parts = [], []
                    for start in range(0, chunk, 2 * half):
                        low, up = slice(start, start + half), slice(start + half, start + 2 * half)
                        edge = pre[start + half - 1:start + half]
                        k_parts += [k[low] * jnp.exp2(edge - pre[low]), zeros]
                        q_parts += [zeros, q[up] * jnp.exp2(pre[up] - edge)]
                    q_l, k_l = jnp.concatenate(q_parts, axis=0), jnp.concatenate(k_parts, axis=0)
                else:
                    e = jnp.exp2(x_alls[h][(lvl - 1) * chunk:lvl * chunk])
                    e_up = e * upper_ref[lvl - 1]
                    q_l, k_l = q * e_up, k * (e - e_up)
                s_l = lax.dot_general(_bf(q_l), _bf(k_l), _NT, preferred_element_type=jnp.float32)
                scores[h] = scores[h] + same_ref[lvl] * s_l
        for h in hs:
            q_ins.append(qs[h] * jnp.exp2(prefixes[h]))
            k_ends.append(ks[h] * jnp.exp2(x_alls[h][(TILE_LEVELS + 1) * chunk:]))
            decays.append(jnp.exp2(prefixes[h][chunk - 1:chunk]))
    else:
        for h in hs:
            q, k, pre, tot = qs[h], ks[h], log_fs[h], log_fs[h]
            for lvl in range(1, n_levels + 1):
                half = 1 << (lvl - 1)
                upper = (row & half) != 0
                e = jnp.exp(jnp.where(upper, pre, tot - pre))
                q_l = jnp.where(upper, q * e, 0.0)
                k_l = jnp.where(upper, 0.0, k * e)
                s_l = lax.dot_general(_bf(q_l), _bf(k_l), _NT, preferred_element_type=jnp.float32)
                scores[h] = scores[h] + same_ref[lvl] * s_l
                tot_sib = jnp.where(upper, pltpu.roll(tot, half, 0), pltpu.roll(tot, chunk - half, 0))
                pre = pre + jnp.where(upper, tot_sib, 0.0)
                tot = tot + tot_sib
            q_ins.append(q * jnp.exp(pre)), k_ends.append(k * jnp.exp(tot - pre)), decays.append(jnp.exp(tot[0:1, :]))

    for h in hs:
        st = st_ref[h]
        o = lax.dot_general(_bf(q_ins[h]), _bf(st), _NT, preferred_element_type=jnp.float32)
        o = o + jnp.dot(_bf(scores[h]), _bf(vs[h]), preferred_element_type=jnp.float32)
        st_ref[h] = st * decays[h] + lax.dot_general(_bf(vs[h]), _bf(k_ends[h]), _TN,
                                                     preferred_element_type=jnp.float32)
        zo = zo_ref[0, :, cols[h]].astype(jnp.float32)
        a = _rms_rows(o, og_ref[:, cols[h]]) / (1.0 + jnp.exp(-zo))
        a_ref[0, :, cols[h]] = a.astype(a_ref.dtype)

    @pl.when(c == pl.num_programs(2) - 1)
    def _():
        for h in range(heads):
            sfin_ref[0, 0, h] = st_ref[h].T


def _gla(zs, lb, out_gain, states, layer, chunk, heads, t_valid):
    b, t, _ = zs[0].shape
    hw = heads * HEAD_DIM
    nh = A_HEADS // heads
    scan, same, upper = _gla_tables(chunk)
    zspec = pl.BlockSpec((1, chunk, hw), lambda bi, hi, ci: (bi, ci, hi))
    vec_spec = pl.BlockSpec((1, hw), lambda bi, hi, ci: (0, hi))
    st_block = (1, 1, heads, HEAD_DIM, HEAD_DIM)
    whole = lambda a: pl.BlockSpec(a.shape, lambda bi, hi, ci: (0,) * a.ndim)
    return pl.pallas_call(
        functools.partial(_gla_kernel, chunk=chunk, heads=heads, t_valid=t_valid),
        grid=(b, nh, t // chunk),
        in_specs=[zspec, zspec, zspec, zspec, vec_spec, vec_spec,
                  pl.BlockSpec(st_block, lambda bi, hi, ci: (layer, bi, hi, 0, 0)),
                  whole(scan), whole(same), whole(upper)],
        out_specs=[pl.BlockSpec((1, chunk, hw), lambda bi, hi, ci: (bi, ci, hi)),
                   pl.BlockSpec(st_block, lambda bi, hi, ci: (0, bi, hi, 0, 0))],
        out_shape=[jax.ShapeDtypeStruct((b, t, D_MODEL), jnp.bfloat16),
                   jax.ShapeDtypeStruct((1,) + states.shape[1:], jnp.float32)],
        scratch_shapes=[pltpu.VMEM((heads, HEAD_DIM, HEAD_DIM), jnp.float32)],
        compiler_params=_params("parallel", "parallel", "arbitrary"),
    )(*zs, lb.reshape(1, D_MODEL), out_gain.reshape(1, D_MODEL), states,
      jnp.asarray(scan, jnp.bfloat16), jnp.asarray(same), jnp.asarray(upper))


def _post_mlp_kernel(x_ref, a_ref, wp_ref, g_ref, wu_ref, wd_ref, out_ref, xn_ref):
    @pl.when(pl.program_id(1) == 0)
    def _():
        x1 = x_ref[...] + jnp.dot(a_ref[...], wp_ref[...], preferred_element_type=jnp.float32)
        out_ref[...] = x1
        xn_ref[...] = _bf(_rms_rows(x1, g_ref[...]))

    hdn = jnp.maximum(jnp.dot(xn_ref[...], wu_ref[...], preferred_element_type=jnp.float32), 0.0)
    out_ref[...] += jnp.dot(_bf(hdn * hdn), wd_ref[...], preferred_element_type=jnp.float32)


def _post_mlp(x, a, wp, gain, wu, wd, tm, tf):
    n, d = x.shape
    (wp, lp), (wu, lu), (wd, ld) = wp, wu, wd
    f = wu.shape[2]
    row_spec = pl.BlockSpec((tm, d), lambda i, j: (i, 0))
    return pl.pallas_call(
        _post_mlp_kernel,
        grid=(n // tm, f // tf),
        in_specs=[row_spec, row_spec,
                  pl.BlockSpec((None, d, d), lambda i, j: (lp, 0, 0)),
                  pl.BlockSpec((1, d), lambda i, j: (0, 0)),
                  pl.BlockSpec((None, d, tf), lambda i, j: (lu, 0, j)),
                  pl.BlockSpec((None, tf, d), lambda i, j: (ld, j, 0))],
        out_specs=row_spec,
        out_shape=jax.ShapeDtypeStruct((n, d), jnp.float32),
        scratch_shapes=[pltpu.VMEM((tm, d), jnp.bfloat16)],
        compiler_params=_params("parallel", "arbitrary"),
    )(x, a, wp, gain.reshape(1, d), wu, wd)


RES = DILATIONS[-1]


def _block_pieces(group):
    pieces = RES // DILATIONS[group]
    return pieces, DIL_KEYS // pieces


def _band_bias(group):
    pieces, per = _block_pieces(group)
    i = np.arange(DIL_KEYS)
    idx = pieces * (i % per) + i // per
    k_true = np.concatenate([idx, DIL_KEYS + idx])
    delta = (DIL_KEYS + idx)[:, None] - k_true[None, :]
    band = (delta >= 0) & (delta <= DIL_KEYS)
    first = band & (np.arange(2 * DIL_KEYS) >= DIL_KEYS)[None, :]
    return np.where(np.stack([first, band]), 0.0, NEG).astype(np.float32)


def _attn_prompt_kernel(bias_ref, q_ref, kp_ref, kc_ref, vp_ref, vc_ref, *refs, pieces, per, qb, n_merge):
    p = DIL_KEYS
    others, out_refs = refs[:2 * n_merge], refs[2 * n_merge:]

    def rows(ref, j, cols, dtype=jnp.bfloat16):
        return jnp.concatenate([ref[0, c, 0, j * per:(j + 1) * per, cols] for c in range(pieces)],
                               axis=0).astype(dtype)

    def put(ref, j, cols, val):
        for c in range(pieces):
            ref[0, c, 0, j * per:(j + 1) * per, cols] = val[c * per:(c + 1) * per].astype(ref.dtype)

    lane = lax.broadcasted_iota(jnp.int32, (p, HEAD_DIM), 1)
    every = slice(None)
    for j in range(qb):
        table = jnp.minimum(pl.program_id(2), 1) if j == 0 else 1
        bias = jnp.concatenate([bias_ref[table]] * Q_PER_KV, axis=0)
        lse_all = jnp.zeros((p, HEAD_DIM), jnp.float32)
        o_heads = {}
        for kvh in range(KV_HEADS):
            kcols = slice(kvh * HEAD_DIM, (kvh + 1) * HEAD_DIM)
            if j == 0:
                k_before, v_before = rows(kp_ref, 0, kcols), rows(vp_ref, 0, kcols)
            else:
                k_before, v_before = rows(kc_ref, j - 1, kcols), rows(vc_ref, j - 1, kcols)
            keys = jnp.concatenate([k_before, rows(kc_ref, j, kcols)], axis=0)
            vals = jnp.concatenate([v_before, rows(vc_ref, j, kcols)], axis=0)
            heads = [kvh * Q_PER_KV + r for r in range(Q_PER_KV)]
            q = jnp.concatenate([rows(q_ref, j, slice(h * HEAD_DIM, (h + 1) * HEAD_DIM)) for h in heads], axis=0)
            s2 = (lax.dot_general(q, keys, _NT, preferred_element_type=jnp.float32) * (HEAD_DIM ** -0.5 * LOG2E)
                  + bias)
            m2 = jnp.max(s2, axis=-1, keepdims=True)
            pr = jnp.exp2(s2 - m2)
            den = jnp.sum(pr, axis=-1, keepdims=True)
            o = jnp.dot(_bf(pr), vals, preferred_element_type=jnp.float32) / den
            lse = m2 * (1.0 / LOG2E) + jnp.log(den)
            for r, h in enumerate(heads):
                o_heads[h] = o[r * p:(r + 1) * p]
                lse_all = jnp.where(lane == h, lse[r * p:(r + 1) * p], lse_all)
        if n_merge == 0:
            o_ref, lse_ref = out_refs
            for h in range(Q_HEADS):
                put(o_ref, j, slice(h * HEAD_DIM, (h + 1) * HEAD_DIM), o_heads[h])
            put(lse_ref, j, every, lse_all)
        else:
            (a_ref,) = out_refs
            lses = [rows(others[2 * g + 1], j, every, jnp.float32) for g in range(n_merge)] + [lse_all]
            top = functools.reduce(jnp.maximum, lses)
            w = [jnp.exp(l - top) for l in lses]
            inv = 1.0 / sum(w)
            w = [wg * inv for wg in w[:-1]]
            for h in range(Q_HEADS):
                cols = slice(h * HEAD_DIM, (h + 1) * HEAD_DIM)
                own = o_heads[h]
                put(a_ref, j, cols, own + sum(w[g][:, h:h + 1] * (rows(others[2 * g], j, cols, jnp.float32) - own)
                                              for g in range(n_merge)))


def _attn_prompt(q, q_col, kv, group, b, s, merge_with=(), qb=4):
    dil = DILATIONS[group]
    pieces, per = _block_pieces(group)
    lr = s // RES
    qb = min(qb, lr // per)
    view = lambda a: a.reshape(b, pieces, dil, lr, a.shape[-1])
    kw = KV_HEADS * HEAD_DIM

    def spec(width, col):
        return pl.BlockSpec((1, pieces, 1, qb * per, width), lambda bi, r, n: (bi, 0, r, n, col))

    def before_spec(col):
        return pl.BlockSpec((1, pieces, 1, per, kw), lambda bi, r, n: (bi, 0, r, jnp.maximum(n * qb - 1, 0), col))

    band = jnp.asarray(_band_bias(group))
    other_args = [view(a) for pair in merge_with for a in pair]
    other_specs = [spec(D_MODEL, 0), spec(HEAD_DIM, 0)] * len(merge_with)
    o_shape = jax.ShapeDtypeStruct((b, pieces, dil, lr, D_MODEL), jnp.bfloat16)
    if merge_with:
        out_specs, out_shape = [spec(D_MODEL, 0)], [o_shape]
    else:
        out_specs = [spec(D_MODEL, 0), spec(HEAD_DIM, 0)]
        out_shape = [o_shape, jax.ShapeDtypeStruct((b, pieces, dil, lr, HEAD_DIM), jnp.float32)]
    outs = pl.pallas_call(
        functools.partial(_attn_prompt_kernel, pieces=pieces, per=per, qb=qb, n_merge=len(merge_with)),
        grid=(b, dil, lr // (per * qb)),
        in_specs=[pl.BlockSpec(band.shape, lambda bi, r, n: (0, 0, 0)),
                  spec(D_MODEL, q_col),
                  before_spec(2 * group), spec(kw, 2 * group),
                  before_spec(2 * group + 1), spec(kw, 2 * group + 1),
                  *other_specs],
        out_specs=out_specs,
        out_shape=out_shape,
        compiler_params=_params("parallel", "parallel", "arbitrary"),
    )(band, view(q), view(kv), view(kv), view(kv), view(kv), *other_args)
    return tuple(a.reshape(b * s, a.shape[-1]) for a in outs)


def _sample_key_blocks(n_new):
    n_cache, tables = [], []
    i = np.arange(DIL_KEYS)
    for g, (win, dil) in enumerate(zip(WINDOWS, DILATIONS)):
        n_past = min(win, PAST_LEN)
        if n_past // DIL_KEYS <= SUBLANES:
            blocks = [DIL_KEYS * k + i for k in range(n_past // DIL_KEYS)]
        else:
            assert n_past % RES == 0 and n_new <= SUBLANES <= RES and dil == RES
            groups = DIL_KEYS // SUBLANES
            blocks = [RES * (groups * k + i // SUBLANES) + i % SUBLANES for k in range(n_past // RES // groups)]
        n_cache.append(len(blocks))
        blocks.append(np.where(i < SUBLANES, n_past + i, -10 ** 9))
        for pos in blocks:
            t = np.arange(SUBLANES)[:, None]
            delta = n_past + t - pos[None, :]
            ok = (delta >= 0) & (delta % dil == 0) & (delta // dil <= DIL_KEYS)
            ok = np.where(t < n_new, ok, True)
            tables.append(ok)
    return n_cache, np.stack(tables).astype(np.float32)


def _attn_sample_kernel(q_ref, kvn_ref, c1_ref, c2_ref, c3_ref, ok_ref, a_ref, *, n_cache):
    rec = 2 * KV_HEADS * HEAD_DIM
    caches = (c1_ref, c2_ref, c3_ref)
    pad = jnp.zeros((DIL_KEYS - SUBLANES, HEAD_DIM), jnp.float32)

    def cache_rows(g, k, is_v, kvh):
        ref = caches[g]
        if len(ref.shape) == 5:
            return ref[0, k * DIL_KEYS:(k + 1) * DIL_KEYS, is_v, kvh, :]
        groups = DIL_KEYS // SUBLANES
        return ref[0, k * groups:(k + 1) * groups, :, is_v, kvh, :].reshape(DIL_KEYS, HEAD_DIM)

    for kvh in range(KV_HEADS):
        heads = [kvh * Q_PER_KV + r for r in range(Q_PER_KV)]
        outs, lses = [], []
        blk = 0
        for g in range(N_GROUPS):
            q = jnp.concatenate(
                [q_ref[0, :, (g * Q_HEADS + h) * HEAD_DIM:(g * Q_HEADS + h + 1) * HEAD_DIM] for h in heads], axis=0)
            scores, values, oks = [], [], []
            for k in range(n_cache[g] + 1):
                if k < n_cache[g]:
                    keys, vals = _bf(cache_rows(g, k, 0, kvh)), _bf(cache_rows(g, k, 1, kvh))
                else:
                    base = g * rec + kvh * HEAD_DIM
                    keys = _bf(jnp.concatenate([kvn_ref[0, :, base:base + HEAD_DIM], pad], axis=0))
                    vals = _bf(jnp.concatenate(
                        [kvn_ref[0, :, base + KV_HEADS * HEAD_DIM:base + (KV_HEADS + 1) * HEAD_DIM], pad], axis=0))
                ok = jnp.concatenate([ok_ref[blk]] * Q_PER_KV, axis=0) > 0.5
                blk += 1
                s = lax.dot_general(q, keys, _NT, preferred_element_type=jnp.float32) * HEAD_DIM ** -0.5
                scores.append(jnp.where(ok, s, NEG))
                values.append(vals)
                oks.append(ok)
            m = functools.reduce(jnp.maximum, [jnp.max(s, axis=-1, keepdims=True) for s in scores])
            probs = [jnp.where(ok, jnp.exp(s - m), 0.0) for s, ok in zip(scores, oks)]
            den = sum(jnp.sum(pr, axis=-1, keepdims=True) for pr in probs)
            o = sum(jnp.dot(_bf(pr), vals, preferred_element_type=jnp.float32) for pr, vals in zip(probs, values))
            outs.append(o / den)
            lses.append(m + jnp.log(den))
        top = functools.reduce(jnp.maximum, lses)
        w = [jnp.exp(l - top) for l in lses]
        wsum = sum(w)
        merged = sum((wg / wsum) * og for wg, og in zip(w, outs))
        for r, h in enumerate(heads):
            a_ref[0, :, h * HEAD_DIM:(h + 1) * HEAD_DIM] = merged[r * SUBLANES:(r + 1) * SUBLANES].astype(a_ref.dtype)


def _attn_sample(q, kv_new, caches, n_new):
    b = q.shape[0]
    rec = 2 * KV_HEADS * HEAD_DIM
    n_cache, table = _sample_key_blocks(n_new)
    cache_views, cache_specs = [], []
    rec_shape = (2, KV_HEADS, HEAD_DIM)
    for cache in caches:
        n_past = cache.shape[1]
        if n_past // DIL_KEYS <= SUBLANES:
            cache_views.append(cache)
            cache_specs.append(pl.BlockSpec((1, n_past) + rec_shape, lambda bi: (bi, 0, 0, 0, 0)))
        else:
            cache_views.append(cache.reshape((b, n_past // RES, RES) + rec_shape))
            cache_specs.append(pl.BlockSpec((1, n_past // RES, SUBLANES) + rec_shape, lambda bi: (bi, 0, 0, 0, 0, 0)))
    return pl.pallas_call(
        functools.partial(_attn_sample_kernel, n_cache=n_cache),
        grid=(b,),
        in_specs=[pl.BlockSpec((1, SUBLANES, Q_COLS), lambda bi: (bi, 0, 0)),
                  pl.BlockSpec((1, SUBLANES, KV_COLS), lambda bi: (bi, 0, 0)),
                  *cache_specs,
                  pl.BlockSpec(table.shape, lambda bi: (0, 0, 0))],
        out_specs=pl.BlockSpec((1, SUBLANES, D_MODEL), lambda bi: (bi, 0, 0)),
        out_shape=jax.ShapeDtypeStruct((b, SUBLANES, D_MODEL), jnp.bfloat16),
        compiler_params=_params("parallel"),
    )(q, kv_new, *cache_views, jnp.asarray(table))


def _rope_tables(pos):
    inv = ROPE_THETA ** (-2.0 * jnp.arange(ROT_HALF, dtype=jnp.float32) / ROT_DIM)
    ang = pos.astype(jnp.float32)[:, None] * inv[None, :]
    cos, sin = jnp.cos(ang), jnp.sin(ang)
    rest = HEAD_DIM - ROT_DIM
    one, zero = jnp.ones((pos.shape[0], rest), jnp.float32), jnp.zeros((pos.shape[0], rest), jnp.float32)
    zh = jnp.zeros_like(sin)
    return (jnp.concatenate([cos, cos, one], axis=1),
            jnp.concatenate([zh, sin, zero], axis=1),
            jnp.concatenate([-sin, zh, zero], axis=1))


SPREAD = np.concatenate([np.arange(ROT_HALF), np.arange(ROT_DIM, HEAD_DIM // 2 + ROT_HALF),
                         np.arange(ROT_HALF, ROT_DIM), np.arange(HEAD_DIM // 2 + ROT_HALF, HEAD_DIM)])
UNSPREAD = np.argsort(SPREAD)


def _spread_heads(w, flags):
    heads = w.reshape(w.shape[:-1] + (len(flags), HEAD_DIM))
    heads = jnp.where(jnp.asarray(flags)[:, None], heads[..., SPREAD], heads)
    return heads.reshape(w.shape)


def _to_residue_major(a, b, t):
    return a.reshape(b, t // RES, RES, a.shape[-1]).swapaxes(1, 2).reshape(b * t, a.shape[-1])


def _from_residue_major(a, b, t):
    return a.reshape(b, RES, t // RES, a.shape[-1]).swapaxes(1, 2).reshape(b * t, a.shape[-1])


def _trunk(x, b, t, pos, state0, caches, n_new, weights, lbs, chunk, gla_heads, tm, tm_proj):
    (a_norm, a_w_in, a_out_norm, a_w_out, kv_norm, w_kv, k_norm,
     b_norm, b_w_q, q_norm, b_w_o, mlp_norm, mlp_w_up, mlp_w_down) = weights
    n = b * t
    tf = 1024
    fresh = caches is None
    rope = _rope_tables(pos)
    n_heads_q, n_heads_kv = Q_COLS // HEAD_DIM, KV_COLS // HEAD_DIM
    kv_normed = tuple(bool((hh // KV_HEADS) % 2 == 0) for hh in range(n_heads_kv))
    kv_gain = jnp.repeat(jnp.repeat(k_norm, KV_HEADS, axis=0), 2, axis=0)
    q_gains = [jnp.repeat(q_norm[j], Q_HEADS, axis=0) for j in range(DEPTH - N_A_LAYERS)]
    if fresh:
        cos, sin_fwd, sin_bwd = rope
        rope = tuple(_to_residue_major(r[:, SPREAD], 1, t) for r in (cos, sin_fwd + sin_bwd))
        b_w_q = _spread_heads(b_w_q, (True,) * n_heads_q)
        w_kv = _spread_heads(w_kv, kv_normed)
        kv_gain = kv_gain[:, SPREAD]
        q_gains = [g[:, SPREAD] for g in q_gains]
        rope_period = t // tm_proj
    else:
        rope = tuple(jnp.tile(r, (tm_proj // t, 1)) for r in rope)
        rope_period = 1
    finals = []
    kv32 = kv16 = None
    for layer in range(DEPTH):
        if layer < N_A_LAYERS:
            zs = _norm_matmul(x, a_norm[layer], (a_w_in, layer), tm,
                              (jnp.bfloat16, jnp.float32, jnp.bfloat16, jnp.bfloat16))
            a, s_fin = _gla([z.reshape(b, t, D_MODEL) for z in zs], lbs[layer], a_out_norm[layer], state0, layer,
                            chunk, gla_heads, n_new)
            finals.append(s_fin)
            x = _post_mlp(x, a.reshape(n, D_MODEL), (a_w_out, layer), mlp_norm[layer], (mlp_w_up, layer),
                          (mlp_w_down, layer), tm, tf)
            continue
        j = layer - N_A_LAYERS
        if j == 0:
            if fresh:
                x = _to_residue_major(x, b, t)
            kv32, kv16 = _proj_heads(x, kv_norm, (w_kv[None], 0), kv_gain, kv_normed, rope, rope_period, tm_proj,
                                     ((jnp.float32, 0, n_heads_kv), (jnp.bfloat16, 0, n_heads_kv)))
        q_gain = q_gains[j]
        q_normed = (True,) * n_heads_q
        if fresh:
            q_near, q_far = _proj_heads(x, b_norm[j], (b_w_q, j), q_gain, q_normed, rope, rope_period, tm_proj,
                                        ((jnp.float32, 0, Q_HEADS), (jnp.bfloat16, Q_HEADS, n_heads_q)))
            near = _attn_prompt(q_near, 0, kv32, 0, b, t)
            mid = _attn_prompt(q_far, 0, kv16, 1, b, t)
            (mixer_in,) = _attn_prompt(q_far, 1, kv16, 2, b, t, merge_with=(near, mid))
        else:
            (q,) = _proj_heads(x, b_norm[j], (b_w_q, j), q_gain, q_normed, rope, rope_period, tm_proj,
                               ((jnp.bfloat16, 0, n_heads_q),))
            a = _attn_sample(q.reshape(b, t, Q_COLS), kv32.reshape(b, t, KV_COLS), caches, n_new)
            mixer_in = a.reshape(n, D_MODEL)
        x = _post_mlp(x, mixer_in, (b_w_o, j), mlp_norm[layer], (mlp_w_up, layer), (mlp_w_down, layer), tm, tf)
    if fresh:
        x = _from_residue_major(x, b, t)
    return x, jnp.concatenate(finals), kv32


def _fresh_window(kv, b, t, group):
    rows, rec = min(WINDOWS[group], t), 2 * KV_HEADS * HEAD_DIM
    a = kv.reshape(b, RES, t // RES, KV_COLS)[:, :, (t - rows) // RES:, group * rec:(group + 1) * rec]
    a = a.swapaxes(1, 2).reshape(b, rows, 2, KV_HEADS, HEAD_DIM)
    return jnp.concatenate([a[:, :, :1][..., UNSPREAD], a[:, :, 1:]], axis=2)


def _cast_kernel(w_ref, o_ref):
    o_ref[...] = w_ref[...].astype(o_ref.dtype)


def _to_bf16(w, row_block=512):
    rows, cols = math.prod(w.shape[:-1]), w.shape[-1]
    spec = pl.BlockSpec((row_block, cols), lambda i: (i, 0))
    out = pl.pallas_call(
        _cast_kernel, grid=(rows // row_block,), in_specs=[spec], out_specs=spec,
        out_shape=jax.ShapeDtypeStruct((rows, cols), jnp.bfloat16),
        compiler_params=_params("parallel"),
    )(w.reshape(rows, cols))
    return out.reshape(w.shape)


def kernel(x_prompt, x_sample, state_hgrn, cache_win1_kv, cache_win2_kv, cache_win3_kv, a_norm, a_w_in, a_lb_logits, a_out_norm, a_w_out, kv_norm, w_kv, k_norm, b_norm, b_w_q, q_norm, b_w_o, mlp_norm, mlp_w_up, mlp_w_down):
    bp, tp, d = x_prompt.shape
    bs, ts, _ = x_sample.shape
    sm = jax.nn.softmax(a_lb_logits.astype(jnp.float32), axis=0)
    lbs = jnp.cumsum(sm, axis=0) - sm[0]
    weights = (a_norm, _to_bf16(a_w_in), a_out_norm, _to_bf16(a_w_out), kv_norm, _to_bf16(w_kv), k_norm,
               b_norm, _to_bf16(b_w_q), q_norm, _to_bf16(b_w_o), mlp_norm, _to_bf16(mlp_w_up),
               _to_bf16(mlp_w_down))

    zero_state = jnp.zeros((N_A_LAYERS, bp, A_HEADS, HEAD_DIM, HEAD_DIM), jnp.float32)
    y_p, st_p, kv_p = _trunk(x_prompt.reshape(bp * tp, d), bp, tp, jnp.arange(tp), zero_state, None, 128,
                             weights, lbs, chunk=128, gla_heads=8, tm=1024, tm_proj=256)

    xs = jnp.pad(x_sample, ((0, 0), (0, SUBLANES - ts), (0, 0))).reshape(bs * SUBLANES, d)
    caches = (cache_win1_kv, cache_win2_kv, cache_win3_kv)
    y_s, st_s, kv_s = _trunk(xs, bs, SUBLANES, PAST_LEN + jnp.arange(SUBLANES), state_hgrn, caches, ts,
                             weights, lbs, chunk=SUBLANES, gla_heads=A_HEADS, tm=bs * SUBLANES,
                             tm_proj=bs * SUBLANES)

    kv_s = kv_s.reshape(bs, SUBLANES, N_GROUPS, 2, KV_HEADS, HEAD_DIM)[:, :ts]
    win_p = [_fresh_window(kv_p, bp, tp, g) for g in range(N_GROUPS)]
    win_s = [kv_s[:, :, g] for g in range(N_GROUPS)]
    return (y_p.reshape(bp, tp, d), y_s.reshape(bs, SUBLANES, d)[:, :ts], st_p, st_s,
            win_p[0], win_p[1], win_p[2], win_s[0], win_s[1], win_s[2])
```

```python
import functools
import math

import jax
import jax.numpy as jnp
import numpy as np
from jax import lax
from jax.experimental import pallas as pl
from jax.experimental.pallas import tpu as pltpu

D_MODEL = 1024
DEPTH = 4
N_A_LAYERS = DEPTH // 2
HEAD_DIM = 128
A_HEADS = D_MODEL // HEAD_DIM
Q_HEADS = D_MODEL // HEAD_DIM
KV_HEADS = 2
Q_PER_KV = Q_HEADS // KV_HEADS
N_GROUPS = 3
WINDOWS = (128, 512, 2048)
DILATIONS = (1, 4, 16)
DIL_KEYS = 128
ROT_DIM = HEAD_DIM // 4
ROT_HALF = ROT_DIM // 2
ROPE_THETA = 500000.0
D_FF = 4 * D_MODEL
EPS = 1e-6
NEG = -1e30
EXP_CLAMP = 80.0
PAST_LEN = 8192

KV_COLS = N_GROUPS * 2 * KV_HEADS * HEAD_DIM
Q_COLS = N_GROUPS * Q_HEADS * HEAD_DIM
SUBLANES = 8
MXU_WIDTH = 256
VMEM_LIMIT = 56 * 1024 * 1024

_NT = (((1,), (1,)), ((), ()))
_TN = (((0,), (0,)), ((), ()))


def _params(*sem):
    return pltpu.CompilerParams(dimension_semantics=sem, vmem_limit_bytes=VMEM_LIMIT)


def _bf(x):
    return x.astype(jnp.bfloat16)


def _rms_rows(x, gain):
    return x * lax.rsqrt(jnp.mean(x * x, axis=-1, keepdims=True) + EPS) * gain


def _norm_matmul_kernel(x_ref, g_ref, w_ref, *out_refs):
    xn = _bf(_rms_rows(x_ref[...], g_ref[...]))
    tn = w_ref.shape[1] // len(out_refs)
    for part, o_ref in enumerate(out_refs):
        o_ref[...] = jnp.dot(xn, w_ref[:, part * tn:(part + 1) * tn],
                             preferred_element_type=jnp.float32).astype(o_ref.dtype)


def _norm_matmul(x, gain, w, tm, out_dtypes):
    n, d = x.shape
    w, layer = w
    tn = w.shape[2] // len(out_dtypes)
    return pl.pallas_call(
        _norm_matmul_kernel,
        grid=(n // tm,),
        in_specs=[pl.BlockSpec((tm, d), lambda i: (i, 0)),
                  pl.BlockSpec((1, d), lambda i: (0, 0)),
                  pl.BlockSpec((None,) + w.shape[1:], lambda i: (layer, 0, 0))],
        out_specs=[pl.BlockSpec((tm, tn), lambda i: (i, 0)) for _ in out_dtypes],
        out_shape=[jax.ShapeDtypeStruct((n, tn), dt) for dt in out_dtypes],
        compiler_params=_params("parallel"),
    )(x, gain.reshape(1, d), w)


PROJ_HEAD_GROUP = 4


def _proj_heads_kernel(x_ref, g_ref, w_ref, hg_ref, *refs, normed, outs):
    rope, out_refs = [r[...] for r in refs[:len(refs) - len(outs)]], refs[len(refs) - len(outs):]
    xn = _bf(_rms_rows(x_ref[...], g_ref[...]))
    width = PROJ_HEAD_GROUP * HEAD_DIM
    for g0 in range(0, len(normed), PROJ_HEAD_GROUP):
        z = jnp.dot(xn, w_ref[:, g0 * HEAD_DIM:g0 * HEAD_DIM + width], preferred_element_type=jnp.float32)
        hs = range(g0, g0 + PROJ_HEAD_GROUP)
        zs = {h: z[:, (h - g0) * HEAD_DIM:(h - g0 + 1) * HEAD_DIM] for h in hs}
        normed_hs = [h for h in hs if normed[h]]
        ms = {h: jnp.mean(zs[h] * zs[h], axis=-1, keepdims=True) for h in normed_hs}
        for h in normed_hs:
            zs[h] = zs[h] * lax.rsqrt(ms[h] + EPS) * hg_ref[h:h + 1, :]
        if len(rope) == 3:
            cos, sin_fwd, sin_bwd = rope
            fwd = {h: pltpu.roll(zs[h], ROT_HALF, 1) for h in normed_hs}
            bwd = {h: pltpu.roll(zs[h], HEAD_DIM - ROT_HALF, 1) for h in normed_hs}
            for h in normed_hs:
                zs[h] = zs[h] * cos + fwd[h] * sin_fwd + bwd[h] * sin_bwd
        else:
            cos, sin = rope
            partner = {h: pltpu.roll(zs[h], HEAD_DIM // 2, 1) for h in normed_hs}
            for h in normed_hs:
                zs[h] = zs[h] * cos + partner[h] * sin
        for h in hs:
            for o_ref, (_, h0, h1) in zip(out_refs, outs):
                if h0 <= h < h1:
                    o_ref[:, (h - h0) * HEAD_DIM:(h - h0 + 1) * HEAD_DIM] = zs[h].astype(o_ref.dtype)


def _proj_heads(x, gain, w, head_gain, normed, rope, rope_period_blocks, tm, outs):
    n, d = x.shape
    w, layer = w
    m = w.shape[2]
    rope_spec = pl.BlockSpec((tm, HEAD_DIM), lambda i: (i % rope_period_blocks, 0))
    return pl.pallas_call(
        functools.partial(_proj_heads_kernel, normed=normed, outs=outs),
        grid=(n // tm,),
        in_specs=[pl.BlockSpec((tm, d), lambda i: (i, 0)),
                  pl.BlockSpec((1, d), lambda i: (0, 0)),
                  pl.BlockSpec((None, d, m), lambda i: (layer, 0, 0)),
                  pl.BlockSpec(head_gain.shape, lambda i: (0, 0)),
                  *[rope_spec] * len(rope)],
        out_specs=[pl.BlockSpec((tm, (h1 - h0) * HEAD_DIM), lambda i: (i, 0)) for _, h0, h1 in outs],
        out_shape=[jax.ShapeDtypeStruct((n, (h1 - h0) * HEAD_DIM), dt) for dt, h0, h1 in outs],
        compiler_params=_params("parallel"),
    )(x, gain.reshape(1, d), w, head_gain, *rope)


LOG2E = 1.4426950408889634
MXU_SCAN_MIN_CHUNK = 128


TILE_LEVELS = int(math.log2(SUBLANES))


def _gla_tables(chunk):
    n_levels = int(math.log2(chunk))
    t, s = np.arange(chunk)[:, None], np.arange(chunk)[None, :]
    mats, masks, uppers = [], [t == s], []
    for lvl in range(1, n_levels + 1):
        half = 1 << (lvl - 1)
        same_half = (t >> (lvl - 1)) == (s >> (lvl - 1))
        upper = (t & half) != 0
        if lvl <= TILE_LEVELS:
            mats.append(np.where(upper, same_half & (s <= t), same_half & (s > t)))
        masks.append((t >> lvl) == (s >> lvl))
        uppers.append(np.broadcast_to(upper, (chunk, HEAD_DIM)))
    mats += [s <= t, s > t]
    scan = np.concatenate(mats)
    if chunk < MXU_SCAN_MIN_CHUNK:
        scan = np.zeros((2 * SUBLANES, HEAD_DIM))
    return scan.astype(np.float32), np.stack(masks).astype(np.float32), np.stack(uppers).astype(np.float32)


def _gla_kernel(zq_ref, zf_ref, zi_ref, zo_ref, lb_ref, og_ref, s0_ref, scan_ref, same_ref, upper_ref,
                a_ref, sfin_ref, st_ref, *, chunk, heads, t_valid):
    c = pl.program_id(2)
    n_levels = int(math.log2(chunk))
    mxu_scan = chunk >= MXU_SCAN_MIN_CHUNK
    row = lax.broadcasted_iota(jnp.int32, (chunk, HEAD_DIM), 0)

    @pl.when(c == 0)
    def _():
        for h in range(heads):
            st_ref[h] = s0_ref[0, 0, h].T

    hs = range(heads)
    cols = [slice(h * HEAD_DIM, (h + 1) * HEAD_DIM) for h in hs]
    qs, ks, vs, log_fs = [], [], [], []
    for h in hs:
        zq, zf = zq_ref[0, :, cols[h]].astype(jnp.float32), zf_ref[0, :, cols[h]]
        lb = lb_ref[:, cols[h]]
        q = zq / (1.0 + jnp.exp(-zq))
        log_f = LOG2E * (jnp.minimum(zf, 0.0) - jnp.log(1.0 + jnp.exp(-jnp.abs(zf)))
                         + jnp.log(1.0 + lb * jnp.exp(jnp.minimum(-zf, EXP_CLAMP))))
        k = (1.0 - lb) / (1.0 + jnp.exp(zf))
        if t_valid < chunk:
            live = row < t_valid
            log_f = jnp.where(live, log_f, 0.0)
            k = jnp.where(live, k, 0.0)
        qs.append(q), ks.append(k), vs.append(zi_ref[0, :, cols[h]]), log_fs.append(log_f)

    scores = [same_ref[0] * lax.dot_general(_bf(qs[h]), _bf(ks[h]), _NT, preferred_element_type=jnp.float32)
              for h in hs]
    q_ins, k_ends, decays = [], [], []
    if mxu_scan:
        scan = scan_ref[...]
        x_alls = []
        for h in hs:
            g2 = log_fs[h]
            hi = _bf(g2)
            rest = g2 - hi.astype(jnp.float32)
            mid = _bf(rest)
            lo = _bf(rest - mid.astype(jnp.float32))
            x_alls.append(jnp.dot(scan, hi, preferred_element_type=jnp.float32)
                          + jnp.dot(scan, mid, preferred_element_type=jnp.float32)
                          + jnp.dot(scan, lo, preferred_element_type=jnp.float32))
        prefixes = [x[TILE_LEVELS * chunk:(TILE_LEVELS + 1) * chunk] for x in x_alls]
        for lvl in range(1, n_levels + 1):
            half = 1 << (lvl - 1)
            for h in hs:
                q, k, pre = qs[h], ks[h], prefixes[h]
                if half >= SUBLANES:
                    zeros = jnp.zeros((half, HEAD_DIM), jnp.float32)
                    q_parts, k_parts = [], []
                    for start in range(0, chunk, 2 * half):
                        low, up = slice(start, start + half), slice(start + half, start + 2 * half)
                        edge = pre[start + half - 1:start + half]
                        k_parts += [k[low] * jnp.exp2(edge - pre[low]), zeros]
                        q_parts += [zeros, q[up] * jnp.exp2(pre[up] - edge)]
                    q_l, k_l = jnp.concatenate(q_parts, axis=0), jnp.concatenate(k_parts, axis=0)
                else:
                    e = jnp.exp2(x_alls[h][(lvl - 1) * chunk:lvl * chunk])
                    e_up = e * upper_ref[lvl - 1]
                    q_l, k_l = q * e_up, k * (e - e_up)
                s_l = lax.dot_general(_bf(q_l), _bf(k_l), _NT, preferred_element_type=jnp.float32)
                scores[h] = scores[h] + same_ref[lvl] * s_l
        for h in hs:
            q_ins.append(qs[h] * jnp.exp2(prefixes[h]))
            k_ends.append(ks[h] * jnp.exp2(x_alls[h][(TILE_LEVELS + 1) * chunk:]))
            decays.append(jnp.exp2(prefixes[h][chunk - 1:chunk]))
    else:
        for h in hs:
            q, k, pre, tot = qs[h], ks[h], log_fs[h], log_fs[h]
            for lvl in range(1, n_levels + 1):
                half = 1 << (lvl - 1)
                upper = (row & half) != 0
                e = jnp.exp2(jnp.where(upper, pre, tot - pre))
                q_l = jnp.where(upper, q * e, 0.0)
                k_l = jnp.where(upper, 0.0, k * e)
                s_l = lax.dot_general(_bf(q_l), _bf(k_l), _NT, preferred_element_type=jnp.float32)
                scores[h] = scores[h] + same_ref[lvl] * s_l
                tot_sib = jnp.where(upper, pltpu.roll(tot, half, 0), pltpu.roll(tot, chunk - half, 0))
                pre = pre + jnp.where(upper, tot_sib, 0.0)
                tot = tot + tot_sib
            q_ins.append(q * jnp.exp2(pre)), k_ends.append(k * jnp.exp2(tot - pre))
            decays.append(jnp.exp2(tot[0:1, :]))

    for h in hs:
        st = st_ref[h]
        o = lax.dot_general(_bf(q_ins[h]), _bf(st), _NT, preferred_element_type=jnp.float32)
        o = o + jnp.dot(_bf(scores[h]), _bf(vs[h]), preferred_element_type=jnp.float32)
        st_ref[h] = st * decays[h] + lax.dot_general(_bf(vs[h]), _bf(k_ends[h]), _TN,
                                                     preferred_element_type=jnp.float32)
        zo = zo_ref[0, :, cols[h]].astype(jnp.float32)
        a = _rms_rows(o, og_ref[:, cols[h]]) / (1.0 + jnp.exp(-zo))
        a_ref[0, :, cols[h]] = a.astype(a_ref.dtype)

    @pl.when(c == pl.num_programs(2) - 1)
    def _():
        for h in range(heads):
            sfin_ref[0, 0, h] = st_ref[h].T


def _gla(zs, lb, out_gain, states, layer, chunk, heads, t_valid):
    b, t, _ = zs[0].shape
    hw = heads * HEAD_DIM
    nh = A_HEADS // heads
    scan, same, upper = _gla_tables(chunk)
    zspec = pl.BlockSpec((1, chunk, hw), lambda bi, hi, ci: (bi, ci, hi))
    vec_spec = pl.BlockSpec((1, hw), lambda bi, hi, ci: (0, hi))
    st_block = (1, 1, heads, HEAD_DIM, HEAD_DIM)
    whole = lambda a: pl.BlockSpec(a.shape, lambda bi, hi, ci: (0,) * a.ndim)
    return pl.pallas_call(
        functools.partial(_gla_kernel, chunk=chunk, heads=heads, t_valid=t_valid),
        grid=(b, nh, t // chunk),
        in_specs=[zspec, zspec, zspec, zspec, vec_spec, vec_spec,
                  pl.BlockSpec(st_block, lambda bi, hi, ci: (layer, bi, hi, 0, 0)),
                  whole(scan), whole(same), whole(upper)],
        out_specs=[pl.BlockSpec((1, chunk, hw), lambda bi, hi, ci: (bi, ci, hi)),
                   pl.BlockSpec(st_block, lambda bi, hi, ci: (0, bi, hi, 0, 0))],
        out_shape=[jax.ShapeDtypeStruct((b, t, D_MODEL), jnp.bfloat16),
                   jax.ShapeDtypeStruct((1,) + states.shape[1:], jnp.float32)],
        scratch_shapes=[pltpu.VMEM((heads, HEAD_DIM, HEAD_DIM), jnp.float32)],
        compiler_params=_params("parallel", "parallel", "arbitrary"),
    )(*zs, lb.reshape(1, D_MODEL), out_gain.reshape(1, D_MODEL), states,
      jnp.asarray(scan, jnp.bfloat16), jnp.asarray(same), jnp.asarray(upper))


def _post_mlp_kernel(x_ref, a_ref, wp_ref, g_ref, wu_ref, wd_ref, out_ref, xn_ref):
    @pl.when(pl.program_id(1) == 0)
    def _():
        x1 = x_ref[...] + jnp.dot(a_ref[...], wp_ref[...], preferred_element_type=jnp.float32)
        out_ref[...] = x1
        xn_ref[...] = _bf(_rms_rows(x1, g_ref[...]))

    hdn = jnp.maximum(jnp.dot(xn_ref[...], wu_ref[...], preferred_element_type=jnp.float32), 0.0)
    out_ref[...] += jnp.dot(_bf(hdn * hdn), wd_ref[...], preferred_element_type=jnp.float32)


def _post_mlp(x, a, wp, gain, wu, wd, tm, tf):
    n, d = x.shape
    (wp, lp), (wu, lu), (wd, ld) = wp, wu, wd
    f = wu.shape[2]
    row_spec = pl.BlockSpec((tm, d), lambda i, j: (i, 0))
    return pl.pallas_call(
        _post_mlp_kernel,
        grid=(n // tm, f // tf),
        in_specs=[row_spec, row_spec,
                  pl.BlockSpec((None, d, d), lambda i, j: (lp, 0, 0)),
                  pl.BlockSpec((1, d), lambda i, j: (0, 0)),
                  pl.BlockSpec((None, d, tf), lambda i, j: (lu, 0, j)),
                  pl.BlockSpec((None, tf, d), lambda i, j: (ld, j, 0))],
        out_specs=row_spec,
        out_shape=jax.ShapeDtypeStruct((n, d), jnp.float32),
        scratch_shapes=[pltpu.VMEM((tm, d), jnp.bfloat16)],
        compiler_params=_params("parallel", "arbitrary"),
    )(x, a, wp, gain.reshape(1, d), wu, wd)


RES = DILATIONS[-1]


def _block_pieces(group):
    pieces = RES // DILATIONS[group]
    return pieces, DIL_KEYS // pieces


def _band_bias(group):
    pieces, per = _block_pieces(group)
    i = np.arange(DIL_KEYS)
    idx = pieces * (i % per) + i // per
    k_true = np.concatenate([idx, DIL_KEYS + idx])
    delta = (DIL_KEYS + idx)[:, None] - k_true[None, :]
    band = (delta >= 0) & (delta <= DIL_KEYS)
    first = band & (np.arange(2 * DIL_KEYS) >= DIL_KEYS)[None, :]
    return np.where(np.stack([first, band]), 0.0, NEG).astype(np.float32)


def _attn_prompt_kernel(bias_ref, q_ref, kp_ref, kc_ref, vp_ref, vc_ref, *refs, pieces, per, qb, n_merge):
    p = DIL_KEYS
    others, out_refs = refs[:2 * n_merge], refs[2 * n_merge:]

    def rows(ref, j, cols, dtype=jnp.bfloat16):
        return jnp.concatenate([ref[0, c, 0, j * per:(j + 1) * per, cols] for c in range(pieces)],
                               axis=0).astype(dtype)

    def put(ref, j, cols, val):
        for c in range(pieces):
            ref[0, c, 0, j * per:(j + 1) * per, cols] = val[c * per:(c + 1) * per].astype(ref.dtype)

    lane = lax.broadcasted_iota(jnp.int32, (p, HEAD_DIM), 1)
    every = slice(None)
    for j in range(qb):
        table = jnp.minimum(pl.program_id(2), 1) if j == 0 else 1
        bias = jnp.concatenate([bias_ref[table]] * Q_PER_KV, axis=0)
        lse_all = jnp.zeros((p, HEAD_DIM), jnp.float32)
        o_heads = {}
        for kvh in range(KV_HEADS):
            kcols = slice(kvh * HEAD_DIM, (kvh + 1) * HEAD_DIM)
            if j == 0:
                k_before, v_before = rows(kp_ref, 0, kcols), rows(vp_ref, 0, kcols)
            else:
                k_before, v_before = rows(kc_ref, j - 1, kcols), rows(vc_ref, j - 1, kcols)
            keys = jnp.concatenate([k_before, rows(kc_ref, j, kcols)], axis=0)
            vals = jnp.concatenate([v_before, rows(vc_ref, j, kcols)], axis=0)
            heads = [kvh * Q_PER_KV + r for r in range(Q_PER_KV)]
            q = jnp.concatenate([rows(q_ref, j, slice(h * HEAD_DIM, (h + 1) * HEAD_DIM)) for h in heads], axis=0)
            s2 = (lax.dot_general(q, keys, _NT, preferred_element_type=jnp.float32) * (HEAD_DIM ** -0.5 * LOG2E)
                  + bias)
            m2 = jnp.max(s2, axis=-1, keepdims=True)
            pr = jnp.exp2(s2 - m2)
            den = jnp.sum(pr, axis=-1, keepdims=True)
            o = jnp.dot(_bf(pr), vals, preferred_element_type=jnp.float32) / den
            lse = m2 * (1.0 / LOG2E) + jnp.log(den)
            for r, h in enumerate(heads):
                o_heads[h] = o[r * p:(r + 1) * p]
                lse_all = jnp.where(lane == h, lse[r * p:(r + 1) * p], lse_all)
        if n_merge == 0:
            o_ref, lse_ref = out_refs
            for h in range(Q_HEADS):
                put(o_ref, j, slice(h * HEAD_DIM, (h + 1) * HEAD_DIM), o_heads[h])
            put(lse_ref, j, every, lse_all)
        else:
            (a_ref,) = out_refs
            lses = [rows(others[2 * g + 1], j, every, jnp.float32) for g in range(n_merge)] + [lse_all]
            top = functools.reduce(jnp.maximum, lses)
            w = [jnp.exp(l - top) for l in lses]
            inv = 1.0 / sum(w)
            w = [wg * inv for wg in w[:-1]]
            for h in range(Q_HEADS):
                cols = slice(h * HEAD_DIM, (h + 1) * HEAD_DIM)
                own = o_heads[h]
                put(a_ref, j, cols, own + sum(w[g][:, h:h + 1] * (rows(others[2 * g], j, cols, jnp.float32) - own)
                                              for g in range(n_merge)))


def _attn_prompt(q, q_col, kv, group, b, s, merge_with=(), qb=4):
    dil = DILATIONS[group]
    pieces, per = _block_pieces(group)
    lr = s // RES
    qb = min(qb, lr // per)
    view = lambda a: a.reshape(b, pieces, dil, lr, a.shape[-1])
    kw = KV_HEADS * HEAD_DIM

    def spec(width, col):
        return pl.BlockSpec((1, pieces, 1, qb * per, width), lambda bi, r, n: (bi, 0, r, n, col))

    def before_spec(col):
        return pl.BlockSpec((1, pieces, 1, per, kw), lambda bi, r, n: (bi, 0, r, jnp.maximum(n * qb - 1, 0), col))

    band = jnp.asarray(_band_bias(group))
    other_args = [view(a) for pair in merge_with for a in pair]
    other_specs = [spec(D_MODEL, 0), spec(HEAD_DIM, 0)] * len(merge_with)
    o_shape = jax.ShapeDtypeStruct((b, pieces, dil, lr, D_MODEL), jnp.bfloat16)
    if merge_with:
        out_specs, out_shape = [spec(D_MODEL, 0)], [o_shape]
    else:
        out_specs = [spec(D_MODEL, 0), spec(HEAD_DIM, 0)]
        out_shape = [o_shape, jax.ShapeDtypeStruct((b, pieces, dil, lr, HEAD_DIM), jnp.float32)]
    outs = pl.pallas_call(
        functools.partial(_attn_prompt_kernel, pieces=pieces, per=per, qb=qb, n_merge=len(merge_with)),
        grid=(b, dil, lr // (per * qb)),
        in_specs=[pl.BlockSpec(band.shape, lambda bi, r, n: (0, 0, 0)),
                  spec(D_MODEL, q_col),
                  before_spec(2 * group), spec(kw, 2 * group),
                  before_spec(2 * group + 1), spec(kw, 2 * group + 1),
                  *other_specs],
        out_specs=out_specs,
        out_shape=out_shape,
        compiler_params=_params("parallel", "parallel", "arbitrary"),
    )(band, view(q), view(kv), view(kv), view(kv), view(kv), *other_args)
    return tuple(a.reshape(b * s, a.shape[-1]) for a in outs)


def _sample_key_blocks(n_new):
    n_cache, tables = [], []
    i = np.arange(DIL_KEYS)
    for g, (win, dil) in enumerate(zip(WINDOWS, DILATIONS)):
        n_past = min(win, PAST_LEN)
        if n_past // DIL_KEYS <= SUBLANES:
            blocks = [DIL_KEYS * k + i for k in range(n_past // DIL_KEYS)]
        else:
            assert n_past == RES * DIL_KEYS and n_new <= RES and dil == RES
            blocks = [RES * i + r for r in range(n_new)]
        n_cache.append(len(blocks))
        blocks.append(np.where(i < SUBLANES, n_past + i, -10 ** 9))
        for pos in blocks:
            t = np.arange(SUBLANES)[:, None]
            delta = n_past + t - pos[None, :]
            ok = (delta >= 0) & (delta % dil == 0) & (delta // dil <= DIL_KEYS)
            ok = np.where(t < n_new, ok, True)
            tables.append(ok)
    return n_cache, np.stack(tables).astype(np.float32)


def _attn_sample_kernel(q_ref, kvn_ref, c1_ref, c2_ref, c3_ref, ok_ref, a_ref, *, n_cache):
    rec = 2 * KV_HEADS * HEAD_DIM
    caches = (c1_ref, c2_ref, c3_ref)
    pad = jnp.zeros((DIL_KEYS - SUBLANES, HEAD_DIM), jnp.float32)

    def cache_rows(g, k, is_v, kvh):
        ref = caches[g]
        if len(ref.shape) == 5:
            return ref[0, k * DIL_KEYS:(k + 1) * DIL_KEYS, is_v, kvh, :]
        return ref[0, :, k, is_v, kvh, :]

    for kvh in range(KV_HEADS):
        heads = [kvh * Q_PER_KV + r for r in range(Q_PER_KV)]
        outs, lses = [], []
        blk = 0
        for g in range(N_GROUPS):
            q = jnp.concatenate(
                [q_ref[0, :, (g * Q_HEADS + h) * HEAD_DIM:(g * Q_HEADS + h + 1) * HEAD_DIM] for h in heads], axis=0)
            scores, values, oks = [], [], []
            for k in range(n_cache[g] + 1):
                if k < n_cache[g]:
                    keys, vals = _bf(cache_rows(g, k, 0, kvh)), _bf(cache_rows(g, k, 1, kvh))
                else:
                    base = g * rec + kvh * HEAD_DIM
                    keys = _bf(jnp.concatenate([kvn_ref[0, :, base:base + HEAD_DIM], pad], axis=0))
                    vals = _bf(jnp.concatenate(
                        [kvn_ref[0, :, base + KV_HEADS * HEAD_DIM:base + (KV_HEADS + 1) * HEAD_DIM], pad], axis=0))
                ok = jnp.concatenate([ok_ref[blk]] * Q_PER_KV, axis=0) > 0.5
                blk += 1
                s = lax.dot_general(q, keys, _NT, preferred_element_type=jnp.float32) * HEAD_DIM ** -0.5
                scores.append(jnp.where(ok, s, NEG))
                values.append(vals)
                oks.append(ok)
            m = functools.reduce(jnp.maximum, [jnp.max(s, axis=-1, keepdims=True) for s in scores])
            probs = [jnp.where(ok, jnp.exp(s - m), 0.0) for s, ok in zip(scores, oks)]
            den = sum(jnp.sum(pr, axis=-1, keepdims=True) for pr in probs)
            o = sum(jnp.dot(_bf(pr), vals, preferred_element_type=jnp.float32) for pr, vals in zip(probs, values))
            outs.append(o / den)
            lses.append(m + jnp.log(den))
        top = functools.reduce(jnp.maximum, lses)
        w = [jnp.exp(l - top) for l in lses]
        wsum = sum(w)
        merged = sum((wg / wsum) * og for wg, og in zip(w, outs))
        for r, h in enumerate(heads):
            a_ref[0, :, h * HEAD_DIM:(h + 1) * HEAD_DIM] = merged[r * SUBLANES:(r + 1) * SUBLANES].astype(a_ref.dtype)


def _attn_sample(q, kv_new, caches, n_new):
    b = q.shape[0]
    rec = 2 * KV_HEADS * HEAD_DIM
    n_cache, table = _sample_key_blocks(n_new)
    cache_views, cache_specs = [], []
    rec_shape = (2, KV_HEADS, HEAD_DIM)
    for cache in caches:
        n_past = cache.shape[1]
        if n_past // DIL_KEYS <= SUBLANES:
            cache_views.append(cache)
            cache_specs.append(pl.BlockSpec((1, n_past) + rec_shape, lambda bi: (bi, 0, 0, 0, 0)))
        else:
            cache_views.append(cache.reshape((b, n_past // RES, RES) + rec_shape))
            cache_specs.append(pl.BlockSpec((1, n_past // RES, n_new) + rec_shape, lambda bi: (bi, 0, 0, 0, 0, 0)))
    return pl.pallas_call(
        functools.partial(_attn_sample_kernel, n_cache=n_cache),
        grid=(b,),
        in_specs=[pl.BlockSpec((1, SUBLANES, Q_COLS), lambda bi: (bi, 0, 0)),
                  pl.BlockSpec((1, SUBLANES, KV_COLS), lambda bi: (bi, 0, 0)),
                  *cache_specs,
                  pl.BlockSpec(table.shape, lambda bi: (0, 0, 0))],
        out_specs=pl.BlockSpec((1, SUBLANES, D_MODEL), lambda bi: (bi, 0, 0)),
        out_shape=jax.ShapeDtypeStruct((b, SUBLANES, D_MODEL), jnp.bfloat16),
        compiler_params=_params("parallel"),
    )(q, kv_new, *cache_views, jnp.asarray(table))


def _rope_tables(pos):
    inv = ROPE_THETA ** (-2.0 * jnp.arange(ROT_HALF, dtype=jnp.float32) / ROT_DIM)
    ang = pos.astype(jnp.float32)[:, None] * inv[None, :]
    cos, sin = jnp.cos(ang), jnp.sin(ang)
    rest = HEAD_DIM - ROT_DIM
    one, zero = jnp.ones((pos.shape[0], rest), jnp.float32), jnp.zeros((pos.shape[0], rest), jnp.float32)
    zh = jnp.zeros_like(sin)
    return (jnp.concatenate([cos, cos, one], axis=1),
            jnp.concatenate([zh, sin, zero], axis=1),
            jnp.concatenate([-sin, zh, zero], axis=1))


SPREAD = np.concatenate([np.arange(ROT_HALF), np.arange(ROT_DIM, HEAD_DIM // 2 + ROT_HALF),
                         np.arange(ROT_HALF, ROT_DIM), np.arange(HEAD_DIM // 2 + ROT_HALF, HEAD_DIM)])
UNSPREAD = np.argsort(SPREAD)


def _spread_heads(w, flags):
    heads = w.reshape(w.shape[:-1] + (len(flags), HEAD_DIM))
    heads = jnp.where(jnp.asarray(flags)[:, None], heads[..., SPREAD], heads)
    return heads.reshape(w.shape)


def _to_residue_major(a, b, t):
    return a.reshape(b, t // RES, RES, a.shape[-1]).swapaxes(1, 2).reshape(b * t, a.shape[-1])


def _from_residue_major(a, b, t):
    return a.reshape(b, RES, t // RES, a.shape[-1]).swapaxes(1, 2).reshape(b * t, a.shape[-1])


def _trunk(x, b, t, pos, state0, caches, n_new, weights, lbs, chunk, gla_heads, tm, tm_proj):
    (a_norm, a_w_in, a_out_norm, a_w_out, kv_norm, w_kv, k_norm,
     b_norm, b_w_q, q_norm, b_w_o, mlp_norm, mlp_w_up, mlp_w_down) = weights
    n = b * t
    tf = 1024
    fresh = caches is None
    rope = _rope_tables(pos)
    n_heads_q, n_heads_kv = Q_COLS // HEAD_DIM, KV_COLS // HEAD_DIM
    kv_normed = tuple(bool((hh // KV_HEADS) % 2 == 0) for hh in range(n_heads_kv))
    kv_gain = jnp.repeat(jnp.repeat(k_norm, KV_HEADS, axis=0), 2, axis=0)
    q_gains = [jnp.repeat(q_norm[j], Q_HEADS, axis=0) for j in range(DEPTH - N_A_LAYERS)]
    if fresh:
        cos, sin_fwd, sin_bwd = rope
        rope = tuple(_to_residue_major(r[:, SPREAD], 1, t) for r in (cos, sin_fwd + sin_bwd))
        b_w_q = _spread_heads(b_w_q, (True,) * n_heads_q)
        w_kv = _spread_heads(w_kv, kv_normed)
        kv_gain = kv_gain[:, SPREAD]
        q_gains = [g[:, SPREAD] for g in q_gains]
        rope_period = t // tm_proj
    else:
        rope = tuple(jnp.tile(r, (tm_proj // t, 1)) for r in rope)
        rope_period = 1
    finals = []
    kv32 = kv16 = None
    for layer in range(DEPTH):
        if layer < N_A_LAYERS:
            zs = _norm_matmul(x, a_norm[layer], (a_w_in, layer), tm,
                              (jnp.bfloat16, jnp.float32, jnp.bfloat16, jnp.bfloat16))
            a, s_fin = _gla([z.reshape(b, t, D_MODEL) for z in zs], lbs[layer], a_out_norm[layer], state0, layer,
                            chunk, gla_heads, n_new)
            finals.append(s_fin)
            x = _post_mlp(x, a.reshape(n, D_MODEL), (a_w_out, layer), mlp_norm[layer], (mlp_w_up, layer),
                          (mlp_w_down, layer), tm, tf)
            continue
        j = layer - N_A_LAYERS
        if j == 0:
            if fresh:
                x = _to_residue_major(x, b, t)
            kv32, kv16 = _proj_heads(x, kv_norm, (w_kv[None], 0), kv_gain, kv_normed, rope, rope_period, tm_proj,
                                     ((jnp.float32, 0, n_heads_kv), (jnp.bfloat16, 0, n_heads_kv)))
        q_gain = q_gains[j]
        q_normed = (True,) * n_heads_q
        if fresh:
            q_near, q_far = _proj_heads(x, b_norm[j], (b_w_q, j), q_gain, q_normed, rope, rope_period, tm_proj,
                                        ((jnp.float32, 0, Q_HEADS), (jnp.bfloat16, Q_HEADS, n_heads_q)))
            near = _attn_prompt(q_near, 0, kv32, 0, b, t)
            mid = _attn_prompt(q_far, 0, kv16, 1, b, t)
            (mixer_in,) = _attn_prompt(q_far, 1, kv16, 2, b, t, merge_with=(near, mid))
        else:
            (q,) = _proj_heads(x, b_norm[j], (b_w_q, j), q_gain, q_normed, rope, rope_period, tm_proj,
                               ((jnp.bfloat16, 0, n_heads_q),))
            a = _attn_sample(q.reshape(b, t, Q_COLS), kv32.reshape(b, t, KV_COLS), caches, n_new)
            mixer_in = a.reshape(n, D_MODEL)
        x = _post_mlp(x, mixer_in, (b_w_o, j), mlp_norm[layer], (mlp_w_up, layer), (mlp_w_down, layer), tm, tf)
    if fresh:
        x = _from_residue_major(x, b, t)
    return x, jnp.concatenate(finals), kv32


def _fresh_window(kv, b, t, group):
    rows, rec = min(WINDOWS[group], t), 2 * KV_HEADS * HEAD_DIM
    a = kv.reshape(b, RES, t // RES, KV_COLS)[:, :, (t - rows) // RES:, group * rec:(group + 1) * rec]
    a = a.swapaxes(1, 2).reshape(b, rows, 2, KV_HEADS, HEAD_DIM)
    return jnp.concatenate([a[:, :, :1][..., UNSPREAD], a[:, :, 1:]], axis=2)


def _cast_kernel(w_ref, o_ref):
    o_ref[...] = w_ref[...].astype(o_ref.dtype)


def _to_bf16(w, row_block=512):
    rows, cols = math.prod(w.shape[:-1]), w.shape[-1]
    spec = pl.BlockSpec((row_block, cols), lambda i: (i, 0))
    out = pl.pallas_call(
        _cast_kernel, grid=(rows // row_block,), in_specs=[spec], out_specs=spec,
        out_shape=jax.ShapeDtypeStruct((rows, cols), jnp.bfloat16),
        compiler_params=_params("parallel"),
    )(w.reshape(rows, cols))
    return out.reshape(w.shape)


def kernel(x_prompt, x_sample, state_hgrn, cache_win1_kv, cache_win2_kv, cache_win3_kv, a_norm, a_w_in, a_lb_logits, a_out_norm, a_w_out, kv_norm, w_kv, k_norm, b_norm, b_w_q, q_norm, b_w_o, mlp_norm, mlp_w_up, mlp_w_down):
    bp, tp, d = x_prompt.shape
    bs, ts, _ = x_sample.shape
    sm = jax.nn.softmax(a_lb_logits.astype(jnp.float32), axis=0)
    lbs = jnp.cumsum(sm, axis=0) - sm[0]
    weights = (a_norm, _to_bf16(a_w_in), a_out_norm, _to_bf16(a_w_out), kv_norm, _to_bf16(w_kv), k_norm,
               b_norm, _to_bf16(b_w_q), q_norm, _to_bf16(b_w_o), mlp_norm, _to_bf16(mlp_w_up),
               _to_bf16(mlp_w_down))

    zero_state = jnp.zeros((N_A_LAYERS, bp, A_HEADS, HEAD_DIM, HEAD_DIM), jnp.float32)
    y_p, st_p, kv_p = _trunk(x_prompt.reshape(bp * tp, d), bp, tp, jnp.arange(tp), zero_state, None, 128,
                             weights, lbs, chunk=128, gla_heads=8, tm=1024, tm_proj=256)

    xs = jnp.pad(x_sample, ((0, 0), (0, SUBLANES - ts), (0, 0))).reshape(bs * SUBLANES, d)
    caches = (cache_win1_kv, cache_win2_kv, cache_win3_kv)
    y_s, st_s, kv_s = _trunk(xs, bs, SUBLANES, PAST_LEN + jnp.arange(SUBLANES), state_hgrn, caches, ts,
                             weights, lbs, chunk=SUBLANES, gla_heads=A_HEADS, tm=bs * SUBLANES,
                             tm_proj=bs * SUBLANES)

    kv_s = kv_s.reshape(bs, SUBLANES, N_GROUPS, 2, KV_HEADS, HEAD_DIM)[:, :ts]
    win_p = [_fresh_window(kv_p, bp, tp, g) for g in range(N_GROUPS)]
    win_s = [kv_s[:, :, g] for g in range(N_GROUPS)]
    return (y_p.reshape(bp, tp, d), y_s.reshape(bs, SUBLANES, d)[:, :ts], st_p, st_s,
            win_p[0], win_p[1], win_p[2], win_s[0], win_s[1], win_s[2])
```

```python
import functools
import math

import jax
import jax.numpy as jnp
import numpy as np
from jax import lax
from jax.experimental import pallas as pl
from jax.experimental.pallas import tpu as pltpu

D_MODEL = 1024
DEPTH = 4
N_A_LAYERS = DEPTH // 2
HEAD_DIM = 128
A_HEADS = D_MODEL // HEAD_DIM
Q_HEADS = D_MODEL // HEAD_DIM
KV_HEADS = 2
Q_PER_KV = Q_HEADS // KV_HEADS
N_GROUPS = 3
WINDOWS = (128, 512, 2048)
DILATIONS = (1, 4, 16)
DIL_KEYS = 128
ROT_DIM = HEAD_DIM // 4
ROT_HALF = ROT_DIM // 2
ROPE_THETA = 500000.0
D_FF = 4 * D_MODEL
EPS = 1e-6
NEG = -1e30
EXP_CLAMP = 80.0
PAST_LEN = 8192

KV_COLS = N_GROUPS * 2 * KV_HEADS * HEAD_DIM
Q_COLS = N_GROUPS * Q_HEADS * HEAD_DIM
SUBLANES = 8
MXU_WIDTH = 256
VMEM_LIMIT = 56 * 1024 * 1024

_NT = (((1,), (1,)), ((), ()))
_TN = (((0,), (0,)), ((), ()))


def _params(*sem):
    return pltpu.CompilerParams(dimension_semantics=sem, vmem_limit_bytes=VMEM_LIMIT)


def _bf(x):
    return x.astype(jnp.bfloat16)


def _rms_rows(x, gain):
    return x * lax.rsqrt(jnp.mean(x * x, axis=-1, keepdims=True) + EPS) * gain


def _norm_matmul_kernel(x_ref, g_ref, w_ref, *out_refs):
    xn = _bf(_rms_rows(x_ref[...], g_ref[...]))
    tn = w_ref.shape[1] // len(out_refs)
    for part, o_ref in enumerate(out_refs):
        o_ref[...] = jnp.dot(xn, w_ref[:, part * tn:(part + 1) * tn],
                             preferred_element_type=jnp.float32).astype(o_ref.dtype)


def _norm_matmul(x, gain, w, tm, out_dtypes):
    n, d = x.shape
    w, layer = w
    tn = w.shape[2] // len(out_dtypes)
    return pl.pallas_call(
        _norm_matmul_kernel,
        grid=(n // tm,),
        in_specs=[pl.BlockSpec((tm, d), lambda i: (i, 0)),
                  pl.BlockSpec((1, d), lambda i: (0, 0)),
                  pl.BlockSpec((None,) + w.shape[1:], lambda i: (layer, 0, 0))],
        out_specs=[pl.BlockSpec((tm, tn), lambda i: (i, 0)) for _ in out_dtypes],
        out_shape=[jax.ShapeDtypeStruct((n, tn), dt) for dt in out_dtypes],
        compiler_params=_params("parallel"),
    )(x, gain.reshape(1, d), w)


PROJ_HEAD_GROUP = 4


def _proj_heads_kernel(x_ref, g_ref, w_ref, hg_ref, *refs, normed, outs):
    rope, out_refs = [r[...] for r in refs[:len(refs) - len(outs)]], refs[len(refs) - len(outs):]
    xn = _bf(_rms_rows(x_ref[...], g_ref[...]))
    width = PROJ_HEAD_GROUP * HEAD_DIM
    for g0 in range(0, len(normed), PROJ_HEAD_GROUP):
        z = jnp.dot(xn, w_ref[:, g0 * HEAD_DIM:g0 * HEAD_DIM + width], preferred_element_type=jnp.float32)
        hs = range(g0, g0 + PROJ_HEAD_GROUP)
        zs = {h: z[:, (h - g0) * HEAD_DIM:(h - g0 + 1) * HEAD_DIM] for h in hs}
        normed_hs = [h for h in hs if normed[h]]
        ms = {h: jnp.mean(zs[h] * zs[h], axis=-1, keepdims=True) for h in normed_hs}
        for h in normed_hs:
            zs[h] = zs[h] * lax.rsqrt(ms[h] + EPS) * hg_ref[h:h + 1, :]
        if len(rope) == 3:
            cos, sin_fwd, sin_bwd = rope
            fwd = {h: pltpu.roll(zs[h], ROT_HALF, 1) for h in normed_hs}
            bwd = {h: pltpu.roll(zs[h], HEAD_DIM - ROT_HALF, 1) for h in normed_hs}
            for h in normed_hs:
                zs[h] = zs[h] * cos + fwd[h] * sin_fwd + bwd[h] * sin_bwd
        else:
            cos, sin = rope
            partner = {h: pltpu.roll(zs[h], HEAD_DIM // 2, 1) for h in normed_hs}
            for h in normed_hs:
                zs[h] = zs[h] * cos + partner[h] * sin
        for h in hs:
            for o_ref, (_, h0, h1) in zip(out_refs, outs):
                if h0 <= h < h1:
                    o_ref[:, (h - h0) * HEAD_DIM:(h - h0 + 1) * HEAD_DIM] = zs[h].astype(o_ref.dtype)


def _proj_heads(x, gain, w, head_gain, normed, rope, rope_period_blocks, tm, outs):
    n, d = x.shape
    w, layer = w
    m = w.shape[2]
    rope_spec = pl.BlockSpec((tm, HEAD_DIM), lambda i: (i % rope_period_blocks, 0))
    return pl.pallas_call(
        functools.partial(_proj_heads_kernel, normed=normed, outs=outs),
        grid=(n // tm,),
        in_specs=[pl.BlockSpec((tm, d), lambda i: (i, 0)),
                  pl.BlockSpec((1, d), lambda i: (0, 0)),
                  pl.BlockSpec((None, d, m), lambda i: (layer, 0, 0)),
                  pl.BlockSpec(head_gain.shape, lambda i: (0, 0)),
                  *[rope_spec] * len(rope)],
        out_specs=[pl.BlockSpec((tm, (h1 - h0) * HEAD_DIM), lambda i: (i, 0)) for _, h0, h1 in outs],
        out_shape=[jax.ShapeDtypeStruct((n, (h1 - h0) * HEAD_DIM), dt) for dt, h0, h1 in outs],
        compiler_params=_params("parallel"),
    )(x, gain.reshape(1, d), w, head_gain, *rope)


LOG2E = 1.4426950408889634
MXU_SCAN_MIN_CHUNK = 128


TILE_LEVELS = int(math.log2(SUBLANES))


def _gla_tables(chunk):
    n_levels = int(math.log2(chunk))
    t, s = np.arange(chunk)[:, None], np.arange(chunk)[None, :]
    mats, masks, uppers = [], [t == s], []
    for lvl in range(1, n_levels + 1):
        half = 1 << (lvl - 1)
        same_half = (t >> (lvl - 1)) == (s >> (lvl - 1))
        upper = (t & half) != 0
        if lvl <= TILE_LEVELS:
            mats.append(np.where(upper, same_half & (s <= t), same_half & (s > t)))
        masks.append((t >> lvl) == (s >> lvl))
        uppers.append(np.broadcast_to(upper, (chunk, HEAD_DIM)))
    mats += [s <= t, s > t]
    scan = np.concatenate(mats)
    if chunk < MXU_SCAN_MIN_CHUNK:
        scan = np.zeros((2 * SUBLANES, HEAD_DIM))
    return scan.astype(np.float32), np.stack(masks).astype(np.float32), np.stack(uppers).astype(np.float32)


def _gla_kernel(zq_ref, zf_ref, zi_ref, zo_ref, lb_ref, og_ref, s0_ref, scan_ref, same_ref, upper_ref,
                a_ref, sfin_ref, st_ref, *, chunk, heads, t_valid):
    c = pl.program_id(2)
    n_levels = int(math.log2(chunk))
    mxu_scan = chunk >= MXU_SCAN_MIN_CHUNK
    row = lax.broadcasted_iota(jnp.int32, (chunk, HEAD_DIM), 0)

    @pl.when(c == 0)
    def _():
        for h in range(heads):
            st_ref[h] = s0_ref[0, 0, h].T

    hs = range(heads)
    cols = [slice(h * HEAD_DIM, (h + 1) * HEAD_DIM) for h in hs]
    qs, ks, vs, log_fs = [], [], [], []
    for h in hs:
        zq, zf = zq_ref[0, :, cols[h]].astype(jnp.float32), zf_ref[0, :, cols[h]]
        lb = lb_ref[:, cols[h]]
        q = zq / (1.0 + jnp.exp(-zq))
        log_f = LOG2E * (jnp.minimum(zf, 0.0) - jnp.log(1.0 + jnp.exp(-jnp.abs(zf)))
                         + jnp.log(1.0 + lb * jnp.exp(jnp.minimum(-zf, EXP_CLAMP))))
        k = (1.0 - lb) / (1.0 + jnp.exp(zf))
        if t_valid < chunk:
            live = row < t_valid
            log_f = jnp.where(live, log_f, 0.0)
            k = jnp.where(live, k, 0.0)
        qs.append(q), ks.append(k), vs.append(zi_ref[0, :, cols[h]]), log_fs.append(log_f)

    scores = [same_ref[0] * lax.dot_general(_bf(qs[h]), _bf(ks[h]), _NT, preferred_element_type=jnp.float32)
              for h in hs]
    q_ins, k_ends, decays = [], [], []
    if mxu_scan:
        scan = scan_ref[...]
        x_alls = []
        for h in hs:
            g2 = log_fs[h]
            hi = _bf(g2)
            rest = g2 - hi.astype(jnp.float32)
            mid = _bf(rest)
            lo = _bf(rest - mid.astype(jnp.float32))
            x_alls.append(jnp.dot(scan, hi, preferred_element_type=jnp.float32)
                          + jnp.dot(scan, mid, preferred_element_type=jnp.float32)
                          + jnp.dot(scan, lo, preferred_element_type=jnp.float32))
        prefixes = [x[TILE_LEVELS * chunk:(TILE_LEVELS + 1) * chunk] for x in x_alls]
        for lvl in range(1, n_levels + 1):
            half = 1 << (lvl - 1)
            for h in hs:
                q, k, pre = qs[h], ks[h], prefixes[h]
                if half >= SUBLANES:
                    zeros = jnp.zeros((half, HEAD_DIM), jnp.float32)
                    q_parts, k_parts = [], []
                    for start in range(0, chunk, 2 * half):
                        low, up = slice(start, start + half), slice(start + half, start + 2 * half)
                        edge = pre[start + half - 1:start + half]
                        k_parts += [k[low] * jnp.exp2(edge - pre[low]), zeros]
                        q_parts += [zeros, q[up] * jnp.exp2(pre[up] - edge)]
                    q_l, k_l = jnp.concatenate(q_parts, axis=0), jnp.concatenate(k_parts, axis=0)
                else:
                    e = jnp.exp2(x_alls[h][(lvl - 1) * chunk:lvl * chunk])
                    e_up = e * upper_ref[lvl - 1]
                    q_l, k_l = q * e_up, k * (e - e_up)
                s_l = lax.dot_general(_bf(q_l), _bf(k_l), _NT, preferred_element_type=jnp.float32)
                scores[h] = scores[h] + same_ref[lvl] * s_l
        for h in hs:
            q_ins.append(qs[h] * jnp.exp2(prefixes[h]))
            k_ends.append(ks[h] * jnp.exp2(x_alls[h][(TILE_LEVELS + 1) * chunk:]))
            decays.append(jnp.exp2(prefixes[h][chunk - 1:chunk]))
    else:
        for h in hs:
            q, k, pre, tot = qs[h], ks[h], log_fs[h], log_fs[h]
            for lvl in range(1, n_levels + 1):
                half = 1 << (lvl - 1)
                upper = (row & half) != 0
                e = jnp.exp2(jnp.where(upper, pre, tot - pre))
                q_l = jnp.where(upper, q * e, 0.0)
                k_l = jnp.where(upper, 0.0, k * e)
                s_l = lax.dot_general(_bf(q_l), _bf(k_l), _NT, preferred_element_type=jnp.float32)
                scores[h] = scores[h] + same_ref[lvl] * s_l
                tot_sib = jnp.where(upper, pltpu.roll(tot, half, 0), pltpu.roll(tot, chunk - half, 0))
                pre = pre + jnp.where(upper, tot_sib, 0.0)
                tot = tot + tot_sib
            q_ins.append(q * jnp.exp2(pre)), k_ends.append(k * jnp.exp2(tot - pre))
            decays.append(jnp.exp2(tot[0:1, :]))

    for h in hs:
        st = st_ref[h]
        o = lax.dot_general(_bf(q_ins[h]), _bf(st), _NT, preferred_element_type=jnp.float32)
        o = o + jnp.dot(_bf(scores[h]), _bf(vs[h]), preferred_element_type=jnp.float32)
        st_ref[h] = st * decays[h] + lax.dot_general(_bf(vs[h]), _bf(k_ends[h]), _TN,
                                                     preferred_element_type=jnp.float32)
        zo = zo_ref[0, :, cols[h]].astype(jnp.float32)
        a = _rms_rows(o, og_ref[:, cols[h]]) / (1.0 + jnp.exp(-zo))
        a_ref[0, :, cols[h]] = a.astype(a_ref.dtype)

    @pl.when(c == pl.num_programs(2) - 1)
    def _():
        for h in range(heads):
            sfin_ref[0, 0, h] = st_ref[h].T


def _gla(zs, lb, out_gain, states, layer, chunk, heads, t_valid):
    b, t, _ = zs[0].shape
    hw = heads * HEAD_DIM
    nh = A_HEADS // heads
    scan, same, upper = _gla_tables(chunk)
    zspec = pl.BlockSpec((1, chunk, hw), lambda bi, hi, ci: (bi, ci, hi))
    vec_spec = pl.BlockSpec((1, hw), lambda bi, hi, ci: (0, hi))
    st_block = (1, 1, heads, HEAD_DIM, HEAD_DIM)
    whole = lambda a: pl.BlockSpec(a.shape, lambda bi, hi, ci: (0,) * a.ndim)
    return pl.pallas_call(
        functools.partial(_gla_kernel, chunk=chunk, heads=heads, t_valid=t_valid),
        grid=(b, nh, t // chunk),
        in_specs=[zspec, zspec, zspec, zspec, vec_spec, vec_spec,
                  pl.BlockSpec(st_block, lambda bi, hi, ci: (layer, bi, hi, 0, 0)),
                  whole(scan), whole(same), whole(upper)],
        out_specs=[pl.BlockSpec((1, chunk, hw), lambda bi, hi, ci: (bi, ci, hi)),
                   pl.BlockSpec(st_block, lambda bi, hi, ci: (0, bi, hi, 0, 0))],
        out_shape=[jax.ShapeDtypeStruct((b, t, D_MODEL), jnp.bfloat16),
                   jax.ShapeDtypeStruct((1,) + states.shape[1:], jnp.float32)],
        scratch_shapes=[pltpu.VMEM((heads, HEAD_DIM, HEAD_DIM), jnp.float32)],
        compiler_params=_params("parallel", "parallel", "arbitrary"),
    )(*zs, lb.reshape(1, D_MODEL), out_gain.reshape(1, D_MODEL), states,
      jnp.asarray(scan, jnp.bfloat16), jnp.asarray(same), jnp.asarray(upper))


def _post_mlp_kernel(x_ref, a_ref, wp_ref, g_ref, wu_ref, wd_ref, out_ref, xn_ref):
    @pl.when(pl.program_id(1) == 0)
    def _():
        x1 = x_ref[...] + jnp.dot(a_ref[...], wp_ref[...], preferred_element_type=jnp.float32)
        out_ref[...] = x1
        xn_ref[...] = _bf(_rms_rows(x1, g_ref[...]))

    hdn = jnp.maximum(jnp.dot(xn_ref[...], wu_ref[...], preferred_element_type=jnp.float32), 0.0)
    out_ref[...] += jnp.dot(_bf(hdn * hdn), wd_ref[...], preferred_element_type=jnp.float32)


def _post_mlp(x, a, wp, gain, wu, wd, tm, tf):
    n, d = x.shape
    (wp, lp), (wu, lu), (wd, ld) = wp, wu, wd
    f = wu.shape[2]
    row_spec = pl.BlockSpec((tm, d), lambda i, j: (i, 0))
    return pl.pallas_call(
        _post_mlp_kernel,
        grid=(n // tm, f // tf),
        in_specs=[row_spec, row_spec,
                  pl.BlockSpec((None, d, d), lambda i, j: (lp, 0, 0)),
                  pl.BlockSpec((1, d), lambda i, j: (0, 0)),
                  pl.BlockSpec((None, d, tf), lambda i, j: (lu, 0, j)),
                  pl.BlockSpec((None, tf, d), lambda i, j: (ld, j, 0))],
        out_specs=row_spec,
        out_shape=jax.ShapeDtypeStruct((n, d), jnp.float32),
        scratch_shapes=[pltpu.VMEM((tm, d), jnp.bfloat16)],
        compiler_params=_params("parallel", "arbitrary"),
    )(x, a, wp, gain.reshape(1, d), wu, wd)


RES = DILATIONS[-1]


def _block_pieces(group):
    pieces = RES // DILATIONS[group]
    return pieces, DIL_KEYS // pieces


def _band_bias(group):
    pieces, per = _block_pieces(group)
    i = np.arange(DIL_KEYS)
    idx = pieces * (i % per) + i // per
    k_true = np.concatenate([idx, DIL_KEYS + idx])
    delta = (DIL_KEYS + idx)[:, None] - k_true[None, :]
    band = (delta >= 0) & (delta <= DIL_KEYS)
    first = band & (np.arange(2 * DIL_KEYS) >= DIL_KEYS)[None, :]
    return np.where(np.stack([first, band]), 0.0, NEG).astype(np.float32)


def _attn_prompt_kernel(bias_ref, q_ref, kp_ref, kc_ref, vp_ref, vc_ref, *refs, pieces, per, qb, n_merge):
    p = DIL_KEYS
    others, out_refs = refs[:2 * n_merge], refs[2 * n_merge:]

    def rows(ref, j, cols, dtype=jnp.bfloat16):
        return jnp.concatenate([ref[0, c, 0, j * per:(j + 1) * per, cols] for c in range(pieces)],
                               axis=0).astype(dtype)

    def put(ref, j, cols, val):
        for c in range(pieces):
            ref[0, c, 0, j * per:(j + 1) * per, cols] = val[c * per:(c + 1) * per].astype(ref.dtype)

    lane = lax.broadcasted_iota(jnp.int32, (p, HEAD_DIM), 1)
    every = slice(None)
    for j in range(qb):
        table = jnp.minimum(pl.program_id(2), 1) if j == 0 else 1
        bias = jnp.concatenate([bias_ref[table]] * Q_PER_KV, axis=0)
        lse_all = jnp.zeros((p, HEAD_DIM), jnp.float32)
        o_heads = {}
        for kvh in range(KV_HEADS):
            kcols = slice(kvh * HEAD_DIM, (kvh + 1) * HEAD_DIM)
            if j == 0:
                k_before, v_before = rows(kp_ref, 0, kcols), rows(vp_ref, 0, kcols)
            else:
                k_before, v_before = rows(kc_ref, j - 1, kcols), rows(vc_ref, j - 1, kcols)
            keys = jnp.concatenate([k_before, rows(kc_ref, j, kcols)], axis=0)
            vals = jnp.concatenate([v_before, rows(vc_ref, j, kcols)], axis=0)
            heads = [kvh * Q_PER_KV + r for r in range(Q_PER_KV)]
            q = jnp.concatenate([rows(q_ref, j, slice(h * HEAD_DIM, (h + 1) * HEAD_DIM)) for h in heads], axis=0)
            s2 = (lax.dot_general(q, keys, _NT, preferred_element_type=jnp.float32) * (HEAD_DIM ** -0.5 * LOG2E)
                  + bias)
            m2 = jnp.max(s2, axis=-1, keepdims=True)
            pr = jnp.exp2(s2 - m2)
            den = jnp.sum(pr, axis=-1, keepdims=True)
            o = jnp.dot(_bf(pr), vals, preferred_element_type=jnp.float32) / den
            lse = m2 * (1.0 / LOG2E) + jnp.log(den)
            for r, h in enumerate(heads):
                o_heads[h] = o[r * p:(r + 1) * p]
                lse_all = jnp.where(lane == h, lse[r * p:(r + 1) * p], lse_all)
        if n_merge == 0:
            o_ref, lse_ref = out_refs
            for h in range(Q_HEADS):
                put(o_ref, j, slice(h * HEAD_DIM, (h + 1) * HEAD_DIM), o_heads[h])
            put(lse_ref, j, every, lse_all)
        else:
            (a_ref,) = out_refs
            lses = [rows(others[2 * g + 1], j, every, jnp.float32) for g in range(n_merge)] + [lse_all]
            top = functools.reduce(jnp.maximum, lses)
            w = [jnp.exp(l - top) for l in lses]
            inv = 1.0 / sum(w)
            w = [wg * inv for wg in w[:-1]]
            for h in range(Q_HEADS):
                cols = slice(h * HEAD_DIM, (h + 1) * HEAD_DIM)
                own = o_heads[h]
                put(a_ref, j, cols, own + sum(w[g][:, h:h + 1] * (rows(others[2 * g], j, cols, jnp.float32) - own)
                                              for g in range(n_merge)))


def _attn_prompt(q, q_col, kv, group, b, s, merge_with=(), qb=4):
    dil = DILATIONS[group]
    pieces, per = _block_pieces(group)
    lr = s // RES
    qb = min(qb, lr // per)
    view = lambda a: a.reshape(b, pieces, dil, lr, a.shape[-1])
    kw = KV_HEADS * HEAD_DIM

    def spec(width, col):
        return pl.BlockSpec((1, pieces, 1, qb * per, width), lambda bi, r, n: (bi, 0, r, n, col))

    def before_spec(col):
        return pl.BlockSpec((1, pieces, 1, per, kw), lambda bi, r, n: (bi, 0, r, jnp.maximum(n * qb - 1, 0), col))

    band = jnp.asarray(_band_bias(group))
    other_args = [view(a) for pair in merge_with for a in pair]
    other_specs = [spec(D_MODEL, 0), spec(HEAD_DIM, 0)] * len(merge_with)
    o_shape = jax.ShapeDtypeStruct((b, pieces, dil, lr, D_MODEL), jnp.bfloat16)
    if merge_with:
        out_specs, out_shape = [spec(D_MODEL, 0)], [o_shape]
    else:
        out_specs = [spec(D_MODEL, 0), spec(HEAD_DIM, 0)]
        out_shape = [o_shape, jax.ShapeDtypeStruct((b, pieces, dil, lr, HEAD_DIM), jnp.float32)]
    outs = pl.pallas_call(
        functools.partial(_attn_prompt_kernel, pieces=pieces, per=per, qb=qb, n_merge=len(merge_with)),
        grid=(b, dil, lr // (per * qb)),
        in_specs=[pl.BlockSpec(band.shape, lambda bi, r, n: (0, 0, 0)),
                  spec(D_MODEL, q_col),
                  before_spec(2 * group), spec(kw, 2 * group),
                  before_spec(2 * group + 1), spec(kw, 2 * group + 1),
                  *other_specs],
        out_specs=out_specs,
        out_shape=out_shape,
        compiler_params=_params("parallel", "parallel", "arbitrary"),
    )(band, view(q), view(kv), view(kv), view(kv), view(kv), *other_args)
    return tuple(a.reshape(b * s, a.shape[-1]) for a in outs)


def _sample_key_blocks(n_new):
    n_cache, tables = [], []
    i = np.arange(DIL_KEYS)
    for g, (win, dil) in enumerate(zip(WINDOWS, DILATIONS)):
        n_past = min(win, PAST_LEN)
        if n_past // DIL_KEYS <= SUBLANES:
            blocks = [DIL_KEYS * k + i for k in range(n_past // DIL_KEYS)]
        else:
            assert n_past == RES * DIL_KEYS and n_new <= RES and dil == RES
            blocks = [RES * i + r for r in range(n_new)]
        n_cache.append(len(blocks))
        blocks.append(np.where(i < SUBLANES, n_past + i, -10 ** 9))
        for pos in blocks:
            t = np.arange(SUBLANES)[:, None]
            delta = n_past + t - pos[None, :]
            ok = (delta >= 0) & (delta % dil == 0) & (delta // dil <= DIL_KEYS)
            ok = np.where(t < n_new, ok, True)
            tables.append(ok)
    return n_cache, np.stack(tables).astype(np.float32)


def _attn_sample_kernel(q_ref, kvn_ref, c1_ref, c2_ref, c3_ref, ok_ref, a_ref, *, n_cache):
    rec = 2 * KV_HEADS * HEAD_DIM
    caches = (c1_ref, c2_ref, c3_ref)
    pad = jnp.zeros((DIL_KEYS - SUBLANES, HEAD_DIM), jnp.float32)

    def cache_rows(g, k, is_v, kvh):
        ref = caches[g]
        if len(ref.shape) == 5:
            return ref[0, k * DIL_KEYS:(k + 1) * DIL_KEYS, is_v, kvh, :]
        return ref[0, :, k, is_v, kvh, :]

    for kvh in range(KV_HEADS):
        heads = [kvh * Q_PER_KV + r for r in range(Q_PER_KV)]
        outs, lses = [], []
        blk = 0
        for g in range(N_GROUPS):
            q = jnp.concatenate(
                [q_ref[0, :, (g * Q_HEADS + h) * HEAD_DIM:(g * Q_HEADS + h + 1) * HEAD_DIM] for h in heads], axis=0)
            scores, values, oks = [], [], []
            for k in range(n_cache[g] + 1):
                if k < n_cache[g]:
                    keys, vals = _bf(cache_rows(g, k, 0, kvh)), _bf(cache_rows(g, k, 1, kvh))
                else:
                    base = g * rec + kvh * HEAD_DIM
                    keys = _bf(jnp.concatenate([kvn_ref[0, :, base:base + HEAD_DIM], pad], axis=0))
                    vals = _bf(jnp.concatenate(
                        [kvn_ref[0, :, base + KV_HEADS * HEAD_DIM:base + (KV_HEADS + 1) * HEAD_DIM], pad], axis=0))
                ok = jnp.concatenate([ok_ref[blk]] * Q_PER_KV, axis=0) > 0.5
                blk += 1
                s = lax.dot_general(q, keys, _NT, preferred_element_type=jnp.float32) * HEAD_DIM ** -0.5
                scores.append(jnp.where(ok, s, NEG))
                values.append(vals)
                oks.append(ok)
            m = functools.reduce(jnp.maximum, [jnp.max(s, axis=-1, keepdims=True) for s in scores])
            probs = [jnp.where(ok, jnp.exp(s - m), 0.0) for s, ok in zip(scores, oks)]
            den = sum(jnp.sum(pr, axis=-1, keepdims=True) for pr in probs)
            o = sum(jnp.dot(_bf(pr), vals, preferred_element_type=jnp.float32) for pr, vals in zip(probs, values))
            outs.append(o / den)
            lses.append(m + jnp.log(den))
        top = functools.reduce(jnp.maximum, lses)
        w = [jnp.exp(l - top) for l in lses]
        wsum = sum(w)
        merged = sum((wg / wsum) * og for wg, og in zip(w, outs))
        for r, h in enumerate(heads):
            a_ref[0, :, h * HEAD_DIM:(h + 1) * HEAD_DIM] = merged[r * SUBLANES:(r + 1) * SUBLANES].astype(a_ref.dtype)


def _attn_sample(q, kv_new, caches, n_new):
    b = q.shape[0]
    rec = 2 * KV_HEADS * HEAD_DIM
    n_cache, table = _sample_key_blocks(n_new)
    cache_views, cache_specs = [], []
    rec_shape = (2, KV_HEADS, HEAD_DIM)
    for cache in caches:
        n_past = cache.shape[1]
        if n_past // DIL_KEYS <= SUBLANES:
            cache_views.append(cache)
            cache_specs.append(pl.BlockSpec((1, n_past) + rec_shape, lambda bi: (bi, 0, 0, 0, 0)))
        else:
            cache_views.append(cache.reshape((b, n_past // RES, RES) + rec_shape))
            cache_specs.append(pl.BlockSpec((1, n_past // RES, n_new) + rec_shape, lambda bi: (bi, 0, 0, 0, 0, 0)))
    return pl.pallas_call(
        functools.partial(_attn_sample_kernel, n_cache=n_cache),
        grid=(b,),
        in_specs=[pl.BlockSpec((1, SUBLANES, Q_COLS), lambda bi: (bi, 0, 0)),
                  pl.BlockSpec((1, SUBLANES, KV_COLS), lambda bi: (bi, 0, 0)),
                  *cache_specs,
                  pl.BlockSpec(table.shape, lambda bi: (0, 0, 0))],
        out_specs=pl.BlockSpec((1, SUBLANES, D_MODEL), lambda bi: (bi, 0, 0)),
        out_shape=jax.ShapeDtypeStruct((b, SUBLANES, D_MODEL), jnp.bfloat16),
        compiler_params=_params("parallel"),
    )(q, kv_new, *cache_views, jnp.asarray(table))


def _rope_tables(pos):
    inv = ROPE_THETA ** (-2.0 * jnp.arange(ROT_HALF, dtype=jnp.float32) / ROT_DIM)
    ang = pos.astype(jnp.float32)[:, None] * inv[None, :]
    cos, sin = jnp.cos(ang), jnp.sin(ang)
    rest = HEAD_DIM - ROT_DIM
    one, zero = jnp.ones((pos.shape[0], rest), jnp.float32), jnp.zeros((pos.shape[0], rest), jnp.float32)
    zh = jnp.zeros_like(sin)
    return (jnp.concatenate([cos, cos, one], axis=1),
            jnp.concatenate([zh, sin, zero], axis=1),
            jnp.concatenate([-sin, zh, zero], axis=1))


SPREAD_RUNS = ((0, ROT_HALF), (ROT_DIM, HEAD_DIM // 2 + ROT_HALF), (ROT_HALF, ROT_DIM),
               (HEAD_DIM // 2 + ROT_HALF, HEAD_DIM))
UNSPREAD_RUNS = ((0, ROT_HALF), (HEAD_DIM // 2, HEAD_DIM // 2 + ROT_HALF), (ROT_HALF, HEAD_DIM // 2),
                 (HEAD_DIM // 2 + ROT_HALF, HEAD_DIM))


def _reorder_lanes(x, runs):
    return jnp.concatenate([x[..., a:b] for a, b in runs], axis=-1)


def _to_residue_major(a, b, t):
    return a.reshape(b, t // RES, RES, a.shape[-1]).swapaxes(1, 2).reshape(b * t, a.shape[-1])


def _from_residue_major(a, b, t):
    return a.reshape(b, RES, t // RES, a.shape[-1]).swapaxes(1, 2).reshape(b * t, a.shape[-1])


def _trunk(x, b, t, pos, state0, caches, n_new, weights, lbs, chunk, gla_heads, tm, tm_proj):
    (a_norm, a_w_in, a_out_norm, a_w_out, kv_norm, w_kv, k_norm,
     b_norm, b_w_q, q_norm, b_w_o, mlp_norm, mlp_w_up, mlp_w_down) = weights
    n = b * t
    tf = 1024
    fresh = caches is None
    rope = _rope_tables(pos)
    n_heads_q, n_heads_kv = Q_COLS // HEAD_DIM, KV_COLS // HEAD_DIM
    kv_normed = tuple(bool((hh // KV_HEADS) % 2 == 0) for hh in range(n_heads_kv))
    kv_gain = jnp.repeat(jnp.repeat(k_norm, KV_HEADS, axis=0), 2, axis=0)
    q_gains = [jnp.repeat(q_norm[j], Q_HEADS, axis=0) for j in range(DEPTH - N_A_LAYERS)]
    if fresh:
        cos, sin_fwd, sin_bwd = rope
        rope = tuple(_to_residue_major(_reorder_lanes(r, SPREAD_RUNS), 1, t) for r in (cos, sin_fwd + sin_bwd))
        kv_gain = _reorder_lanes(kv_gain, SPREAD_RUNS)
        q_gains = [_reorder_lanes(g, SPREAD_RUNS) for g in q_gains]
        rope_period = t // tm_proj
    else:
        rope = tuple(jnp.tile(r, (tm_proj // t, 1)) for r in rope)
        rope_period = 1
    finals = []
    kv32 = kv16 = None
    for layer in range(DEPTH):
        if layer < N_A_LAYERS:
            zs = _norm_matmul(x, a_norm[layer], (a_w_in, layer), tm,
                              (jnp.bfloat16, jnp.float32, jnp.bfloat16, jnp.bfloat16))
            a, s_fin = _gla([z.reshape(b, t, D_MODEL) for z in zs], lbs[layer], a_out_norm[layer], state0, layer,
                            chunk, gla_heads, n_new)
            finals.append(s_fin)
            x = _post_mlp(x, a.reshape(n, D_MODEL), (a_w_out, layer), mlp_norm[layer], (mlp_w_up, layer),
                          (mlp_w_down, layer), tm, tf)
            continue
        j = layer - N_A_LAYERS
        if j == 0:
            if fresh:
                x = _to_residue_major(x, b, t)
            kv32, kv16 = _proj_heads(x, kv_norm, (w_kv[None], 0), kv_gain, kv_normed, rope, rope_period, tm_proj,
                                     ((jnp.float32, 0, n_heads_kv), (jnp.bfloat16, 0, n_heads_kv)))
        q_gain = q_gains[j]
        q_normed = (True,) * n_heads_q
        if fresh:
            q_near, q_far = _proj_heads(x, b_norm[j], (b_w_q, j), q_gain, q_normed, rope, rope_period, tm_proj,
                                        ((jnp.float32, 0, Q_HEADS), (jnp.bfloat16, Q_HEADS, n_heads_q)))
            near = _attn_prompt(q_near, 0, kv32, 0, b, t)
            mid = _attn_prompt(q_far, 0, kv16, 1, b, t)
            (mixer_in,) = _attn_prompt(q_far, 1, kv16, 2, b, t, merge_with=(near, mid))
        else:
            (q,) = _proj_heads(x, b_norm[j], (b_w_q, j), q_gain, q_normed, rope, rope_period, tm_proj,
                               ((jnp.bfloat16, 0, n_heads_q),))
            a = _attn_sample(q.reshape(b, t, Q_COLS), kv32.reshape(b, t, KV_COLS), caches, n_new)
            mixer_in = a.reshape(n, D_MODEL)
        x = _post_mlp(x, mixer_in, (b_w_o, j), mlp_norm[layer], (mlp_w_up, layer), (mlp_w_down, layer), tm, tf)
    if fresh:
        x = _from_residue_major(x, b, t)
    return x, jnp.concatenate(finals), kv32


def _fresh_window(kv, b, t, group):
    rows, rec = min(WINDOWS[group], t), 2 * KV_HEADS * HEAD_DIM
    a = kv.reshape(b, RES, t // RES, KV_COLS)[:, :, (t - rows) // RES:, group * rec:(group + 1) * rec]
    a = a.swapaxes(1, 2).reshape(b, rows, 2, KV_HEADS, HEAD_DIM)
    return jnp.concatenate([_reorder_lanes(a[:, :, :1], UNSPREAD_RUNS), a[:, :, 1:]], axis=2)


def _cast_kernel(w_ref, o_ref, *spread_refs, spread):
    o_ref[...] = w_ref[...].astype(o_ref.dtype)
    for s_ref in spread_refs:
        for h, flag in enumerate(spread):
            cols = slice(h * HEAD_DIM, (h + 1) * HEAD_DIM)
            head = w_ref[:, cols]
            s_ref[:, cols] = (_reorder_lanes(head, SPREAD_RUNS) if flag else head).astype(s_ref.dtype)


def _to_bf16(w, spread=(), row_block=512):
    rows, cols = math.prod(w.shape[:-1]), w.shape[-1]
    spec = pl.BlockSpec((row_block, cols), lambda i: (i, 0))
    n_out = 2 if spread else 1
    outs = pl.pallas_call(
        functools.partial(_cast_kernel, spread=spread),
        grid=(rows // row_block,), in_specs=[spec], out_specs=[spec] * n_out,
        out_shape=[jax.ShapeDtypeStruct((rows, cols), jnp.bfloat16)] * n_out,
        compiler_params=_params("parallel"),
    )(w.reshape(rows, cols))
    outs = [o.reshape(w.shape) for o in outs]
    return outs if spread else outs[0]


def _prepare_weights(a_norm, a_w_in, a_out_norm, a_w_out, kv_norm, w_kv, k_norm, b_norm, b_w_q, q_norm, b_w_o,
                     mlp_norm, mlp_w_up, mlp_w_down):
    kv_is_k = tuple(bool((hh // KV_HEADS) % 2 == 0) for hh in range(KV_COLS // HEAD_DIM))
    w_kv_bf, w_kv_spread = _to_bf16(w_kv, spread=kv_is_k)
    w_q_bf, w_q_spread = _to_bf16(b_w_q, spread=(True,) * (Q_COLS // HEAD_DIM))
    w_in, w_out, w_o, w_up, w_down = (_to_bf16(w) for w in (a_w_in, a_w_out, b_w_o, mlp_w_up, mlp_w_down))
    pack = lambda kv_w, q_w: (a_norm, w_in, a_out_norm, w_out, kv_norm, kv_w, k_norm,
                              b_norm, q_w, q_norm, w_o, mlp_norm, w_up, w_down)
    return pack(w_kv_bf, w_q_bf), pack(w_kv_spread, w_q_spread)


def kernel(x_prompt, x_sample, state_hgrn, cache_win1_kv, cache_win2_kv, cache_win3_kv, a_norm, a_w_in, a_lb_logits, a_out_norm, a_w_out, kv_norm, w_kv, k_norm, b_norm, b_w_q, q_norm, b_w_o, mlp_norm, mlp_w_up, mlp_w_down):
    bp, tp, d = x_prompt.shape
    bs, ts, _ = x_sample.shape
    sm = jax.nn.softmax(a_lb_logits.astype(jnp.float32), axis=0)
    lbs = jnp.cumsum(sm, axis=0) - sm[0]
    weights, weights_fresh = _prepare_weights(a_norm, a_w_in, a_out_norm, a_w_out, kv_norm, w_kv, k_norm, b_norm,
                                              b_w_q, q_norm, b_w_o, mlp_norm, mlp_w_up, mlp_w_down)

    zero_state = jnp.zeros((N_A_LAYERS, bp, A_HEADS, HEAD_DIM, HEAD_DIM), jnp.float32)
    y_p, st_p, kv_p = _trunk(x_prompt.reshape(bp * tp, d), bp, tp, jnp.arange(tp), zero_state, None, 128,
                             weights_fresh, lbs, chunk=128, gla_heads=8, tm=1024, tm_proj=256)

    xs = jnp.pad(x_sample, ((0, 0), (0, SUBLANES - ts), (0, 0))).reshape(bs * SUBLANES, d)
    caches = (cache_win1_kv, cache_win2_kv, cache_win3_kv)
    y_s, st_s, kv_s = _trunk(xs, bs, SUBLANES, PAST_LEN + jnp.arange(SUBLANES), state_hgrn, caches, ts,
                             weights, lbs, chunk=SUBLANES, gla_heads=A_HEADS, tm=bs * SUBLANES,
                             tm_proj=bs * SUBLANES)

    kv_s = kv_s.reshape(bs, SUBLANES, N_GROUPS, 2, KV_HEADS, HEAD_DIM)[:, :ts]
    win_p = [_fresh_window(kv_p, bp, tp, g) for g in range(N_GROUPS)]
    win_s = [kv_s[:, :, g] for g in range(N_GROUPS)]
    return (y_p.reshape(bp, tp, d), y_s.reshape(bs, SUBLANES, d)[:, :ts], st_p, st_s,
            win_p[0], win_p[1], win_p[2], win_s[0], win_s[1], win_s[2])
```

```python
import functools
import math

import jax
import jax.numpy as jnp
import numpy as np
from jax import lax
from jax.experimental import pallas as pl
from jax.experimental.pallas import tpu as pltpu

D_MODEL = 1024
DEPTH = 4
N_A_LAYERS = DEPTH // 2
HEAD_DIM = 128
A_HEADS = D_MODEL // HEAD_DIM
Q_HEADS = D_MODEL // HEAD_DIM
KV_HEADS = 2
Q_PER_KV = Q_HEADS // KV_HEADS
N_GROUPS = 3
WINDOWS = (128, 512, 2048)
DILATIONS = (1, 4, 16)
DIL_KEYS = 128
ROT_DIM = HEAD_DIM // 4
ROT_HALF = ROT_DIM // 2
ROPE_THETA = 500000.0
D_FF = 4 * D_MODEL
EPS = 1e-6
NEG = -1e30
EXP_CLAMP = 80.0
PAST_LEN = 8192

KV_COLS = N_GROUPS * 2 * KV_HEADS * HEAD_DIM
Q_COLS = N_GROUPS * Q_HEADS * HEAD_DIM
SUBLANES = 8
VMEM_LIMIT = 56 * 1024 * 1024

ROW_TILE = 1024
FF_TILE = 1024
PROJ_ROW_TILE = 256
GLA_CHUNK = 128
ATTN_BLOCKS_PER_STEP = 4

_NT = (((1,), (1,)), ((), ()))
_TN = (((0,), (0,)), ((), ()))


def _params(*sem):
    return pltpu.CompilerParams(dimension_semantics=sem, vmem_limit_bytes=VMEM_LIMIT)


def _bf(x):
    return x.astype(jnp.bfloat16)


def _rms_rows(x, gain):
    return x * lax.rsqrt(jnp.mean(x * x, axis=-1, keepdims=True) + EPS) * gain


def _norm_matmul_kernel(x_ref, g_ref, w_ref, *out_refs):
    xn = _bf(_rms_rows(x_ref[...], g_ref[...]))
    tn = w_ref.shape[1] // len(out_refs)
    for part, o_ref in enumerate(out_refs):
        o_ref[...] = jnp.dot(xn, w_ref[:, part * tn:(part + 1) * tn],
                             preferred_element_type=jnp.float32).astype(o_ref.dtype)


def _norm_matmul(x, gain, w, tm, out_dtypes):
    n, d = x.shape
    w, layer = w
    tn = w.shape[2] // len(out_dtypes)
    return pl.pallas_call(
        _norm_matmul_kernel,
        grid=(n // tm,),
        in_specs=[pl.BlockSpec((tm, d), lambda i: (i, 0)),
                  pl.BlockSpec((1, d), lambda i: (0, 0)),
                  pl.BlockSpec((None,) + w.shape[1:], lambda i: (layer, 0, 0))],
        out_specs=[pl.BlockSpec((tm, tn), lambda i: (i, 0)) for _ in out_dtypes],
        out_shape=[jax.ShapeDtypeStruct((n, tn), dt) for dt in out_dtypes],
        compiler_params=_params("parallel"),
    )(x, gain.reshape(1, d), w)


PROJ_HEAD_GROUP = 4


def _proj_heads_kernel(x_ref, g_ref, w_ref, hg_ref, *refs, normed, outs):
    rope, out_refs = [r[...] for r in refs[:len(refs) - len(outs)]], refs[len(refs) - len(outs):]
    xn = _bf(_rms_rows(x_ref[...], g_ref[...]))
    width = PROJ_HEAD_GROUP * HEAD_DIM
    for g0 in range(0, len(normed), PROJ_HEAD_GROUP):
        z = jnp.dot(xn, w_ref[:, g0 * HEAD_DIM:g0 * HEAD_DIM + width], preferred_element_type=jnp.float32)
        hs = range(g0, g0 + PROJ_HEAD_GROUP)
        zs = {h: z[:, (h - g0) * HEAD_DIM:(h - g0 + 1) * HEAD_DIM] for h in hs}
        normed_hs = [h for h in hs if normed[h]]
        ms = {h: jnp.mean(zs[h] * zs[h], axis=-1, keepdims=True) for h in normed_hs}
        for h in normed_hs:
            zs[h] = zs[h] * lax.rsqrt(ms[h] + EPS) * hg_ref[h:h + 1, :]
        if len(rope) == 3:
            cos, sin_fwd, sin_bwd = rope
            fwd = {h: pltpu.roll(zs[h], ROT_HALF, 1) for h in normed_hs}
            bwd = {h: pltpu.roll(zs[h], HEAD_DIM - ROT_HALF, 1) for h in normed_hs}
            for h in normed_hs:
                zs[h] = zs[h] * cos + fwd[h] * sin_fwd + bwd[h] * sin_bwd
        else:
            cos, sin = rope
            partner = {h: pltpu.roll(zs[h], HEAD_DIM // 2, 1) for h in normed_hs}
            for h in normed_hs:
                zs[h] = zs[h] * cos + partner[h] * sin
        for h in hs:
            for o_ref, (_, h0, h1) in zip(out_refs, outs):
                if h0 <= h < h1:
                    o_ref[:, (h - h0) * HEAD_DIM:(h - h0 + 1) * HEAD_DIM] = zs[h].astype(o_ref.dtype)


def _proj_heads(x, gain, w, head_gain, normed, rope, rope_period_blocks, tm, outs):
    n, d = x.shape
    w, layer = w
    m = w.shape[2]
    rope_spec = pl.BlockSpec((tm, HEAD_DIM), lambda i: (i % rope_period_blocks, 0))
    return pl.pallas_call(
        functools.partial(_proj_heads_kernel, normed=normed, outs=outs),
        grid=(n // tm,),
        in_specs=[pl.BlockSpec((tm, d), lambda i: (i, 0)),
                  pl.BlockSpec((1, d), lambda i: (0, 0)),
                  pl.BlockSpec((None, d, m), lambda i: (layer, 0, 0)),
                  pl.BlockSpec(head_gain.shape, lambda i: (0, 0)),
                  *[rope_spec] * len(rope)],
        out_specs=[pl.BlockSpec((tm, (h1 - h0) * HEAD_DIM), lambda i: (i, 0)) for _, h0, h1 in outs],
        out_shape=[jax.ShapeDtypeStruct((n, (h1 - h0) * HEAD_DIM), dt) for dt, h0, h1 in outs],
        compiler_params=_params("parallel"),
    )(x, gain.reshape(1, d), w, head_gain, *rope)


LOG2E = 1.4426950408889634
MXU_SCAN_MIN_CHUNK = 128


TILE_LEVELS = int(math.log2(SUBLANES))


def _gla_tables(chunk):
    n_levels = int(math.log2(chunk))
    t, s = np.arange(chunk)[:, None], np.arange(chunk)[None, :]
    mats, masks, uppers = [], [t == s], []
    for lvl in range(1, n_levels + 1):
        half = 1 << (lvl - 1)
        same_half = (t >> (lvl - 1)) == (s >> (lvl - 1))
        upper = (t & half) != 0
        if lvl <= TILE_LEVELS:
            mats.append(np.where(upper, same_half & (s <= t), same_half & (s > t)))
        masks.append((t >> lvl) == (s >> lvl))
        uppers.append(np.broadcast_to(upper, (chunk, HEAD_DIM)))
    mats += [s <= t, s > t]
    scan = np.concatenate(mats)
    if chunk < MXU_SCAN_MIN_CHUNK:
        scan = np.zeros((2 * SUBLANES, HEAD_DIM))
    return scan.astype(np.float32), np.stack(masks).astype(np.float32), np.stack(uppers).astype(np.float32)


def _gla_kernel(zq_ref, zf_ref, zi_ref, zo_ref, lb_ref, og_ref, s0_ref, scan_ref, same_ref, upper_ref,
                a_ref, sfin_ref, st_ref, *, chunk, heads, t_valid):
    c = pl.program_id(2)
    n_levels = int(math.log2(chunk))
    mxu_scan = chunk >= MXU_SCAN_MIN_CHUNK
    row = lax.broadcasted_iota(jnp.int32, (chunk, HEAD_DIM), 0)

    @pl.when(c == 0)
    def _():
        for h in range(heads):
            st_ref[h] = s0_ref[0, 0, h].T

    hs = range(heads)
    cols = [slice(h * HEAD_DIM, (h + 1) * HEAD_DIM) for h in hs]
    qs, ks, vs, log_fs = [], [], [], []
    for h in hs:
        zq, zf = zq_ref[0, :, cols[h]].astype(jnp.float32), zf_ref[0, :, cols[h]]
        lb = lb_ref[:, cols[h]]
        q = zq / (1.0 + jnp.exp(-zq))
        log_f = LOG2E * (jnp.minimum(zf, 0.0) - jnp.log(1.0 + jnp.exp(-jnp.abs(zf)))
                         + jnp.log(1.0 + lb * jnp.exp(jnp.minimum(-zf, EXP_CLAMP))))
        k = (1.0 - lb) / (1.0 + jnp.exp(zf))
        if t_valid < chunk:
            live = row < t_valid
            log_f = jnp.where(live, log_f, 0.0)
            k = jnp.where(live, k, 0.0)
        qs.append(q), ks.append(k), vs.append(zi_ref[0, :, cols[h]]), log_fs.append(log_f)

    scores = [same_ref[0] * lax.dot_general(_bf(qs[h]), _bf(ks[h]), _NT, preferred_element_type=jnp.float32)
              for h in hs]
    q_ins, k_ends, decays = [], [], []
    if mxu_scan:
        scan = scan_ref[...]
        x_alls = []
        for h in hs:
            g2 = log_fs[h]
            hi = _bf(g2)
            rest = g2 - hi.astype(jnp.float32)
            mid = _bf(rest)
            lo = _bf(rest - mid.astype(jnp.float32))
            x_alls.append(jnp.dot(scan, hi, preferred_element_type=jnp.float32)
                          + jnp.dot(scan, mid, preferred_element_type=jnp.float32)
                          + jnp.dot(scan, lo, preferred_element_type=jnp.float32))
        prefixes = [x[TILE_LEVELS * chunk:(TILE_LEVELS + 1) * chunk] for x in x_alls]
        for lvl in range(1, n_levels + 1):
            half = 1 << (lvl - 1)
            for h in hs:
                q, k, pre = qs[h], ks[h], prefixes[h]
                if half >= SUBLANES:
                    zeros = jnp.zeros((half, HEAD_DIM), jnp.float32)
                    q_parts, k_parts = [], []
                    for start in range(0, chunk, 2 * half):
                        low, up = slice(start, start + half), slice(start + half, start + 2 * half)
                        edge = pre[start + half - 1:start + half]
                        k_parts += [k[low] * jnp.exp2(edge - pre[low]), zeros]
                        q_parts += [zeros, q[up] * jnp.exp2(pre[up] - edge)]
                    q_l, k_l = jnp.concatenate(q_parts, axis=0), jnp.concatenate(k_parts, axis=0)
                else:
                    e = jnp.exp2(x_alls[h][(lvl - 1) * chunk:lvl * chunk])
                    e_up = e * upper_ref[lvl - 1]
                    q_l, k_l = q * e_up, k * (e - e_up)
                s_l = lax.dot_general(_bf(q_l), _bf(k_l), _NT, preferred_element_type=jnp.float32)
                scores[h] = scores[h] + same_ref[lvl] * s_l
        for h in hs:
            q_ins.append(qs[h] * jnp.exp2(prefixes[h]))
            k_ends.append(ks[h] * jnp.exp2(x_alls[h][(TILE_LEVELS + 1) * chunk:]))
            decays.append(jnp.exp2(prefixes[h][chunk - 1:chunk]))
    else:
        for h in hs:
            q, k, pre, tot = qs[h], ks[h], log_fs[h], log_fs[h]
            for lvl in range(1, n_levels + 1):
                half = 1 << (lvl - 1)
                upper = (row & half) != 0
                e = jnp.exp2(jnp.where(upper, pre, tot - pre))
                q_l = jnp.where(upper, q * e, 0.0)
                k_l = jnp.where(upper, 0.0, k * e)
                s_l = lax.dot_general(_bf(q_l), _bf(k_l), _NT, preferred_element_type=jnp.float32)
                scores[h] = scores[h] + same_ref[lvl] * s_l
                tot_sib = jnp.where(upper, pltpu.roll(tot, half, 0), pltpu.roll(tot, chunk - half, 0))
                pre = pre + jnp.where(upper, tot_sib, 0.0)
                tot = tot + tot_sib
            q_ins.append(q * jnp.exp2(pre)), k_ends.append(k * jnp.exp2(tot - pre))
            decays.append(jnp.exp2(tot[0:1, :]))

    for h in hs:
        st = st_ref[h]
        o = lax.dot_general(_bf(q_ins[h]), _bf(st), _NT, preferred_element_type=jnp.float32)
        o = o + jnp.dot(_bf(scores[h]), _bf(vs[h]), preferred_element_type=jnp.float32)
        st_ref[h] = st * decays[h] + lax.dot_general(_bf(vs[h]), _bf(k_ends[h]), _TN,
                                                     preferred_element_type=jnp.float32)
        zo = zo_ref[0, :, cols[h]].astype(jnp.float32)
        a = _rms_rows(o, og_ref[:, cols[h]]) / (1.0 + jnp.exp(-zo))
        a_ref[0, :, cols[h]] = a.astype(a_ref.dtype)

    @pl.when(c == pl.num_programs(2) - 1)
    def _():
        for h in range(heads):
            sfin_ref[0, 0, h] = st_ref[h].T


def _gla(zs, lb, out_gain, states, layer, chunk, heads, t_valid):
    b, t, _ = zs[0].shape
    hw = heads * HEAD_DIM
    nh = A_HEADS // heads
    scan, same, upper = _gla_tables(chunk)
    zspec = pl.BlockSpec((1, chunk, hw), lambda bi, hi, ci: (bi, ci, hi))
    vec_spec = pl.BlockSpec((1, hw), lambda bi, hi, ci: (0, hi))
    st_block = (1, 1, heads, HEAD_DIM, HEAD_DIM)
    whole = lambda a: pl.BlockSpec(a.shape, lambda bi, hi, ci: (0,) * a.ndim)
    return pl.pallas_call(
        functools.partial(_gla_kernel, chunk=chunk, heads=heads, t_valid=t_valid),
        grid=(b, nh, t // chunk),
        in_specs=[zspec, zspec, zspec, zspec, vec_spec, vec_spec,
                  pl.BlockSpec(st_block, lambda bi, hi, ci: (layer, bi, hi, 0, 0)),
                  whole(scan), whole(same), whole(upper)],
        out_specs=[pl.BlockSpec((1, chunk, hw), lambda bi, hi, ci: (bi, ci, hi)),
                   pl.BlockSpec(st_block, lambda bi, hi, ci: (0, bi, hi, 0, 0))],
        out_shape=[jax.ShapeDtypeStruct((b, t, D_MODEL), jnp.bfloat16),
                   jax.ShapeDtypeStruct((1,) + states.shape[1:], jnp.float32)],
        scratch_shapes=[pltpu.VMEM((heads, HEAD_DIM, HEAD_DIM), jnp.float32)],
        compiler_params=_params("parallel", "parallel", "arbitrary"),
    )(*zs, lb.reshape(1, D_MODEL), out_gain.reshape(1, D_MODEL), states,
      jnp.asarray(scan, jnp.bfloat16), jnp.asarray(same), jnp.asarray(upper))


def _post_mlp_kernel(x_ref, a_ref, wp_ref, g_ref, wu_ref, wd_ref, out_ref, xn_ref):
    @pl.when(pl.program_id(1) == 0)
    def _():
        x1 = x_ref[...] + jnp.dot(a_ref[...], wp_ref[...], preferred_element_type=jnp.float32)
        out_ref[...] = x1
        xn_ref[...] = _bf(_rms_rows(x1, g_ref[...]))

    hdn = jnp.maximum(jnp.dot(xn_ref[...], wu_ref[...], preferred_element_type=jnp.float32), 0.0)
    out_ref[...] += jnp.dot(_bf(hdn * hdn), wd_ref[...], preferred_element_type=jnp.float32)


def _post_mlp(x, a, wp, gain, wu, wd, tm, tf):
    n, d = x.shape
    (wp, lp), (wu, lu), (wd, ld) = wp, wu, wd
    f = wu.shape[2]
    row_spec = pl.BlockSpec((tm, d), lambda i, j: (i, 0))
    return pl.pallas_call(
        _post_mlp_kernel,
        grid=(n // tm, f // tf),
        in_specs=[row_spec, row_spec,
                  pl.BlockSpec((None, d, d), lambda i, j: (lp, 0, 0)),
                  pl.BlockSpec((1, d), lambda i, j: (0, 0)),
                  pl.BlockSpec((None, d, tf), lambda i, j: (lu, 0, j)),
                  pl.BlockSpec((None, tf, d), lambda i, j: (ld, j, 0))],
        out_specs=row_spec,
        out_shape=jax.ShapeDtypeStruct((n, d), jnp.float32),
        scratch_shapes=[pltpu.VMEM((tm, d), jnp.bfloat16)],
        compiler_params=_params("parallel", "arbitrary"),
    )(x, a, wp, gain.reshape(1, d), wu, wd)


RES = DILATIONS[-1]


def _block_pieces(group):
    pieces = RES // DILATIONS[group]
    return pieces, DIL_KEYS // pieces


def _band_bias(group):
    pieces, per = _block_pieces(group)
    i = np.arange(DIL_KEYS)
    idx = pieces * (i % per) + i // per
    k_true = np.concatenate([idx, DIL_KEYS + idx])
    delta = (DIL_KEYS + idx)[:, None] - k_true[None, :]
    band = (delta >= 0) & (delta <= DIL_KEYS)
    first = band & (np.arange(2 * DIL_KEYS) >= DIL_KEYS)[None, :]
    return np.where(np.stack([first, band]), 0.0, NEG).astype(np.float32)


def _attn_prompt_kernel(bias_ref, q_ref, kp_ref, kc_ref, vp_ref, vc_ref, *refs, pieces, per, qb, n_merge):
    p = DIL_KEYS
    others, out_refs = refs[:2 * n_merge], refs[2 * n_merge:]

    def rows(ref, j, cols, dtype=jnp.bfloat16):
        return jnp.concatenate([ref[0, c, 0, j * per:(j + 1) * per, cols] for c in range(pieces)],
                               axis=0).astype(dtype)

    def put(ref, j, cols, val):
        for c in range(pieces):
            ref[0, c, 0, j * per:(j + 1) * per, cols] = val[c * per:(c + 1) * per].astype(ref.dtype)

    lane = lax.broadcasted_iota(jnp.int32, (p, HEAD_DIM), 1)
    every = slice(None)
    for j in range(qb):
        table = jnp.minimum(pl.program_id(2), 1) if j == 0 else 1
        bias = jnp.concatenate([bias_ref[table]] * Q_PER_KV, axis=0)
        lse_all = jnp.zeros((p, HEAD_DIM), jnp.float32)
        o_heads = {}
        for kvh in range(KV_HEADS):
            kcols = slice(kvh * HEAD_DIM, (kvh + 1) * HEAD_DIM)
            if j == 0:
                k_before, v_before = rows(kp_ref, 0, kcols), rows(vp_ref, 0, kcols)
            else:
                k_before, v_before = rows(kc_ref, j - 1, kcols), rows(vc_ref, j - 1, kcols)
            keys = jnp.concatenate([k_before, rows(kc_ref, j, kcols)], axis=0)
            vals = jnp.concatenate([v_before, rows(vc_ref, j, kcols)], axis=0)
            heads = [kvh * Q_PER_KV + r for r in range(Q_PER_KV)]
            q = jnp.concatenate([rows(q_ref, j, slice(h * HEAD_DIM, (h + 1) * HEAD_DIM)) for h in heads], axis=0)
            s2 = (lax.dot_general(q, keys, _NT, preferred_element_type=jnp.float32) * (HEAD_DIM ** -0.5 * LOG2E)
                  + bias)
            m2 = jnp.max(s2, axis=-1, keepdims=True)
            pr = jnp.exp2(s2 - m2)
            den = jnp.sum(pr, axis=-1, keepdims=True)
            o = jnp.dot(_bf(pr), vals, preferred_element_type=jnp.float32) / den
            lse = m2 * (1.0 / LOG2E) + jnp.log(den)
            for r, h in enumerate(heads):
                o_heads[h] = o[r * p:(r + 1) * p]
                lse_all = jnp.where(lane == h, lse[r * p:(r + 1) * p], lse_all)
        if n_merge == 0:
            o_ref, lse_ref = out_refs
            for h in range(Q_HEADS):
                put(o_ref, j, slice(h * HEAD_DIM, (h + 1) * HEAD_DIM), o_heads[h])
            put(lse_ref, j, every, lse_all)
        else:
            (a_ref,) = out_refs
            lses = [rows(others[2 * g + 1], j, every, jnp.float32) for g in range(n_merge)] + [lse_all]
            top = functools.reduce(jnp.maximum, lses)
            w = [jnp.exp(l - top) for l in lses]
            inv = 1.0 / sum(w)
            w = [wg * inv for wg in w[:-1]]
            for h in range(Q_HEADS):
                cols = slice(h * HEAD_DIM, (h + 1) * HEAD_DIM)
                own = o_heads[h]
                put(a_ref, j, cols, own + sum(w[g][:, h:h + 1] * (rows(others[2 * g], j, cols, jnp.float32) - own)
                                              for g in range(n_merge)))


def _attn_prompt(q, q_col, kv, group, b, s, merge_with=(), qb=ATTN_BLOCKS_PER_STEP):
    dil = DILATIONS[group]
    pieces, per = _block_pieces(group)
    lr = s // RES
    qb = min(qb, lr // per)
    view = lambda a: a.reshape(b, pieces, dil, lr, a.shape[-1])
    kw = KV_HEADS * HEAD_DIM

    def spec(width, col):
        return pl.BlockSpec((1, pieces, 1, qb * per, width), lambda bi, r, n: (bi, 0, r, n, col))

    def before_spec(col):
        return pl.BlockSpec((1, pieces, 1, per, kw), lambda bi, r, n: (bi, 0, r, jnp.maximum(n * qb - 1, 0), col))

    band = jnp.asarray(_band_bias(group))
    other_args = [view(a) for pair in merge_with for a in pair]
    other_specs = [spec(D_MODEL, 0), spec(HEAD_DIM, 0)] * len(merge_with)
    o_shape = jax.ShapeDtypeStruct((b, pieces, dil, lr, D_MODEL), jnp.bfloat16)
    if merge_with:
        out_specs, out_shape = [spec(D_MODEL, 0)], [o_shape]
    else:
        out_specs = [spec(D_MODEL, 0), spec(HEAD_DIM, 0)]
        out_shape = [o_shape, jax.ShapeDtypeStruct((b, pieces, dil, lr, HEAD_DIM), jnp.float32)]
    outs = pl.pallas_call(
        functools.partial(_attn_prompt_kernel, pieces=pieces, per=per, qb=qb, n_merge=len(merge_with)),
        grid=(b, dil, lr // (per * qb)),
        in_specs=[pl.BlockSpec(band.shape, lambda bi, r, n: (0, 0, 0)),
                  spec(D_MODEL, q_col),
                  before_spec(2 * group), spec(kw, 2 * group),
                  before_spec(2 * group + 1), spec(kw, 2 * group + 1),
                  *other_specs],
        out_specs=out_specs,
        out_shape=out_shape,
        compiler_params=_params("parallel", "parallel", "arbitrary"),
    )(band, view(q), view(kv), view(kv), view(kv), view(kv), *other_args)
    return tuple(a.reshape(b * s, a.shape[-1]) for a in outs)


def _sample_key_blocks(n_new):
    n_cache, tables = [], []
    i = np.arange(DIL_KEYS)
    for g, (win, dil) in enumerate(zip(WINDOWS, DILATIONS)):
        n_past = min(win, PAST_LEN)
        if n_past // DIL_KEYS <= SUBLANES:
            blocks = [DIL_KEYS * k + i for k in range(n_past // DIL_KEYS)]
        else:
            assert n_past == RES * DIL_KEYS and n_new <= RES and dil == RES
            blocks = [RES * i + r for r in range(n_new)]
        n_cache.append(len(blocks))
        blocks.append(np.where(i < SUBLANES, n_past + i, -10 ** 9))
        for pos in blocks:
            t = np.arange(SUBLANES)[:, None]
            delta = n_past + t - pos[None, :]
            ok = (delta >= 0) & (delta % dil == 0) & (delta // dil <= DIL_KEYS)
            ok = np.where(t < n_new, ok, True)
            tables.append(ok)
    return n_cache, np.stack(tables).astype(np.float32)


def _attn_sample_kernel(q_ref, kvn_ref, c1_ref, c2_ref, c3_ref, ok_ref, a_ref, *, n_cache):
    rec = 2 * KV_HEADS * HEAD_DIM
    caches = (c1_ref, c2_ref, c3_ref)
    pad = jnp.zeros((DIL_KEYS - SUBLANES, HEAD_DIM), jnp.float32)

    def cache_rows(g, k, is_v, kvh):
        ref = caches[g]
        if len(ref.shape) == 5:
            return ref[0, k * DIL_KEYS:(k + 1) * DIL_KEYS, is_v, kvh, :]
        return ref[0, :, k, is_v, kvh, :]

    for kvh in range(KV_HEADS):
        heads = [kvh * Q_PER_KV + r for r in range(Q_PER_KV)]
        outs, lses = [], []
        blk = 0
        for g in range(N_GROUPS):
            q = jnp.concatenate(
                [q_ref[0, :, (g * Q_HEADS + h) * HEAD_DIM:(g * Q_HEADS + h + 1) * HEAD_DIM] for h in heads], axis=0)
            scores, values, oks = [], [], []
            for k in range(n_cache[g] + 1):
                if k < n_cache[g]:
                    keys, vals = _bf(cache_rows(g, k, 0, kvh)), _bf(cache_rows(g, k, 1, kvh))
                else:
                    base = g * rec + kvh * HEAD_DIM
                    keys = _bf(jnp.concatenate([kvn_ref[0, :, base:base + HEAD_DIM], pad], axis=0))
                    vals = _bf(jnp.concatenate(
                        [kvn_ref[0, :, base + KV_HEADS * HEAD_DIM:base + (KV_HEADS + 1) * HEAD_DIM], pad], axis=0))
                ok = jnp.concatenate([ok_ref[blk]] * Q_PER_KV, axis=0) > 0.5
                blk += 1
                s = lax.dot_general(q, keys, _NT, preferred_element_type=jnp.float32) * HEAD_DIM ** -0.5
                scores.append(jnp.where(ok, s, NEG))
                values.append(vals)
                oks.append(ok)
            m = functools.reduce(jnp.maximum, [jnp.max(s, axis=-1, keepdims=True) for s in scores])
            probs = [jnp.where(ok, jnp.exp(s - m), 0.0) for s, ok in zip(scores, oks)]
            den = sum(jnp.sum(pr, axis=-1, keepdims=True) for pr in probs)
            o = sum(jnp.dot(_bf(pr), vals, preferred_element_type=jnp.float32) for pr, vals in zip(probs, values))
            outs.append(o / den)
            lses.append(m + jnp.log(den))
        top = functools.reduce(jnp.maximum, lses)
        w = [jnp.exp(l - top) for l in lses]
        wsum = sum(w)
        merged = sum((wg / wsum) * og for wg, og in zip(w, outs))
        for r, h in enumerate(heads):
            a_ref[0, :, h * HEAD_DIM:(h + 1) * HEAD_DIM] = merged[r * SUBLANES:(r + 1) * SUBLANES].astype(a_ref.dtype)


def _attn_sample(q, kv_new, caches, n_new):
    b = q.shape[0]
    rec = 2 * KV_HEADS * HEAD_DIM
    n_cache, table = _sample_key_blocks(n_new)
    cache_views, cache_specs = [], []
    rec_shape = (2, KV_HEADS, HEAD_DIM)
    for cache in caches:
        n_past = cache.shape[1]
        if n_past // DIL_KEYS <= SUBLANES:
            cache_views.append(cache)
            cache_specs.append(pl.BlockSpec((1, n_past) + rec_shape, lambda bi: (bi, 0, 0, 0, 0)))
        else:
            cache_views.append(cache.reshape((b, n_past // RES, RES) + rec_shape))
            cache_specs.append(pl.BlockSpec((1, n_past // RES, n_new) + rec_shape, lambda bi: (bi, 0, 0, 0, 0, 0)))
    return pl.pallas_call(
        functools.partial(_attn_sample_kernel, n_cache=n_cache),
        grid=(b,),
        in_specs=[pl.BlockSpec((1, SUBLANES, Q_COLS), lambda bi: (bi, 0, 0)),
                  pl.BlockSpec((1, SUBLANES, KV_COLS), lambda bi: (bi, 0, 0)),
                  *cache_specs,
                  pl.BlockSpec(table.shape, lambda bi: (0, 0, 0))],
        out_specs=pl.BlockSpec((1, SUBLANES, D_MODEL), lambda bi: (bi, 0, 0)),
        out_shape=jax.ShapeDtypeStruct((b, SUBLANES, D_MODEL), jnp.bfloat16),
        compiler_params=_params("parallel"),
    )(q, kv_new, *cache_views, jnp.asarray(table))


def _rope_tables(pos):
    inv = ROPE_THETA ** (-2.0 * jnp.arange(ROT_HALF, dtype=jnp.float32) / ROT_DIM)
    ang = pos.astype(jnp.float32)[:, None] * inv[None, :]
    cos, sin = jnp.cos(ang), jnp.sin(ang)
    rest = HEAD_DIM - ROT_DIM
    one, zero = jnp.ones((pos.shape[0], rest), jnp.float32), jnp.zeros((pos.shape[0], rest), jnp.float32)
    zh = jnp.zeros_like(sin)
    return (jnp.concatenate([cos, cos, one], axis=1),
            jnp.concatenate([zh, sin, zero], axis=1),
            jnp.concatenate([-sin, zh, zero], axis=1))


SPREAD_RUNS = ((0, ROT_HALF), (ROT_DIM, HEAD_DIM // 2 + ROT_HALF), (ROT_HALF, ROT_DIM),
               (HEAD_DIM // 2 + ROT_HALF, HEAD_DIM))
UNSPREAD_RUNS = ((0, ROT_HALF), (HEAD_DIM // 2, HEAD_DIM // 2 + ROT_HALF), (ROT_HALF, HEAD_DIM // 2),
                 (HEAD_DIM // 2 + ROT_HALF, HEAD_DIM))


def _reorder_lanes(x, runs):
    return jnp.concatenate([x[..., a:b] for a, b in runs], axis=-1)


def _to_residue_major(a, b, t):
    return a.reshape(b, t // RES, RES, a.shape[-1]).swapaxes(1, 2).reshape(b * t, a.shape[-1])


def _from_residue_major(a, b, t):
    return a.reshape(b, RES, t // RES, a.shape[-1]).swapaxes(1, 2).reshape(b * t, a.shape[-1])


def _trunk(x, b, t, pos, state0, caches, n_new, weights, lbs, chunk, gla_heads, tm, tm_proj):
    (a_norm, a_w_in, a_out_norm, a_w_out, kv_norm, w_kv, k_norm,
     b_norm, b_w_q, q_norm, b_w_o, mlp_norm, mlp_w_up, mlp_w_down) = weights
    n = b * t
    tf = FF_TILE
    fresh = caches is None
    rope = _rope_tables(pos)
    n_heads_q, n_heads_kv = Q_COLS // HEAD_DIM, KV_COLS // HEAD_DIM
    kv_normed = tuple(bool((hh // KV_HEADS) % 2 == 0) for hh in range(n_heads_kv))
    kv_gain = jnp.repeat(jnp.repeat(k_norm, KV_HEADS, axis=0), 2, axis=0)
    q_gains = [jnp.repeat(q_norm[j], Q_HEADS, axis=0) for j in range(DEPTH - N_A_LAYERS)]
    if fresh:
        cos, sin_fwd, sin_bwd = rope
        rope = tuple(_to_residue_major(_reorder_lanes(r, SPREAD_RUNS), 1, t) for r in (cos, sin_fwd + sin_bwd))
        kv_gain = _reorder_lanes(kv_gain, SPREAD_RUNS)
        q_gains = [_reorder_lanes(g, SPREAD_RUNS) for g in q_gains]
        rope_period = t // tm_proj
    else:
        rope = tuple(jnp.tile(r, (tm_proj // t, 1)) for r in rope)
        rope_period = 1
    finals = []
    kv32 = kv16 = None
    for layer in range(DEPTH):
        if layer < N_A_LAYERS:
            zs = _norm_matmul(x, a_norm[layer], (a_w_in, layer), tm,
                              (jnp.bfloat16, jnp.float32, jnp.bfloat16, jnp.bfloat16))
            a, s_fin = _gla([z.reshape(b, t, D_MODEL) for z in zs], lbs[layer], a_out_norm[layer], state0, layer,
                            chunk, gla_heads, n_new)
            finals.append(s_fin)
            x = _post_mlp(x, a.reshape(n, D_MODEL), (a_w_out, layer), mlp_norm[layer], (mlp_w_up, layer),
                          (mlp_w_down, layer), tm, tf)
            continue
        j = layer - N_A_LAYERS
        if j == 0:
            if fresh:
                x = _to_residue_major(x, b, t)
            kv32, kv16 = _proj_heads(x, kv_norm, (w_kv[None], 0), kv_gain, kv_normed, rope, rope_period, tm_proj,
                                     ((jnp.float32, 0, n_heads_kv), (jnp.bfloat16, 0, n_heads_kv)))
        q_gain = q_gains[j]
        q_normed = (True,) * n_heads_q
        if fresh:
            q_near, q_far = _proj_heads(x, b_norm[j], (b_w_q, j), q_gain, q_normed, rope, rope_period, tm_proj,
                                        ((jnp.float32, 0, Q_HEADS), (jnp.bfloat16, Q_HEADS, n_heads_q)))
            near = _attn_prompt(q_near, 0, kv32, 0, b, t)
            mid = _attn_prompt(q_far, 0, kv16, 1, b, t)
            (mixer_in,) = _attn_prompt(q_far, 1, kv16, 2, b, t, merge_with=(near, mid))
        else:
            (q,) = _proj_heads(x, b_norm[j], (b_w_q, j), q_gain, q_normed, rope, rope_period, tm_proj,
                               ((jnp.bfloat16, 0, n_heads_q),))
            a = _attn_sample(q.reshape(b, t, Q_COLS), kv32.reshape(b, t, KV_COLS), caches, n_new)
            mixer_in = a.reshape(n, D_MODEL)
        x = _post_mlp(x, mixer_in, (b_w_o, j), mlp_norm[layer], (mlp_w_up, layer), (mlp_w_down, layer), tm, tf)
    if fresh:
        x = _from_residue_major(x, b, t)
    return x, jnp.concatenate(finals), kv32


def _fresh_window(kv, b, t, group):
    rows, rec = min(WINDOWS[group], t), 2 * KV_HEADS * HEAD_DIM
    a = kv.reshape(b, RES, t // RES, KV_COLS)[:, :, (t - rows) // RES:, group * rec:(group + 1) * rec]
    a = a.swapaxes(1, 2).reshape(b, rows, 2, KV_HEADS, HEAD_DIM)
    is_k = (jnp.arange(2) == 0)[:, None, None]
    return jnp.where(is_k, _reorder_lanes(a, UNSPREAD_RUNS), a)


def _cast_kernel(w_ref, o_ref, *spread_refs, spread):
    o_ref[...] = w_ref[...].astype(o_ref.dtype)
    for s_ref in spread_refs:
        for h, flag in enumerate(spread):
            cols = slice(h * HEAD_DIM, (h + 1) * HEAD_DIM)
            head = w_ref[:, cols]
            s_ref[:, cols] = (_reorder_lanes(head, SPREAD_RUNS) if flag else head).astype(s_ref.dtype)


def _to_bf16(w, spread=(), row_block=512):
    rows, cols = math.prod(w.shape[:-1]), w.shape[-1]
    spec = pl.BlockSpec((row_block, cols), lambda i: (i, 0))
    n_out = 2 if spread else 1
    outs = pl.pallas_call(
        functools.partial(_cast_kernel, spread=spread),
        grid=(rows // row_block,), in_specs=[spec], out_specs=[spec] * n_out,
        out_shape=[jax.ShapeDtypeStruct((rows, cols), jnp.bfloat16)] * n_out,
        compiler_params=_params("parallel"),
    )(w.reshape(rows, cols))
    outs = [o.reshape(w.shape) for o in outs]
    return outs if spread else outs[0]


def _prepare_weights(a_norm, a_w_in, a_out_norm, a_w_out, kv_norm, w_kv, k_norm, b_norm, b_w_q, q_norm, b_w_o,
                     mlp_norm, mlp_w_up, mlp_w_down):
    kv_is_k = tuple(bool((hh // KV_HEADS) % 2 == 0) for hh in range(KV_COLS // HEAD_DIM))
    w_kv_bf, w_kv_spread = _to_bf16(w_kv, spread=kv_is_k)
    w_q_bf, w_q_spread = _to_bf16(b_w_q, spread=(True,) * (Q_COLS // HEAD_DIM))
    w_in, w_out, w_o, w_up, w_down = (_to_bf16(w) for w in (a_w_in, a_w_out, b_w_o, mlp_w_up, mlp_w_down))
    pack = lambda kv_w, q_w: (a_norm, w_in, a_out_norm, w_out, kv_norm, kv_w, k_norm,
                              b_norm, q_w, q_norm, w_o, mlp_norm, w_up, w_down)
    return pack(w_kv_bf, w_q_bf), pack(w_kv_spread, w_q_spread)


def kernel(x_prompt, x_sample, state_hgrn, cache_win1_kv, cache_win2_kv, cache_win3_kv, a_norm, a_w_in, a_lb_logits, a_out_norm, a_w_out, kv_norm, w_kv, k_norm, b_norm, b_w_q, q_norm, b_w_o, mlp_norm, mlp_w_up, mlp_w_down):
    bp, tp, d = x_prompt.shape
    bs, ts, _ = x_sample.shape
    sm = jax.nn.softmax(a_lb_logits.astype(jnp.float32), axis=0)
    lbs = jnp.cumsum(sm, axis=0) - sm[0]
    weights, weights_fresh = _prepare_weights(a_norm, a_w_in, a_out_norm, a_w_out, kv_norm, w_kv, k_norm, b_norm,
                                              b_w_q, q_norm, b_w_o, mlp_norm, mlp_w_up, mlp_w_down)

    zero_state = jnp.zeros((N_A_LAYERS, bp, A_HEADS, HEAD_DIM, HEAD_DIM), jnp.float32)
    y_p, st_p, kv_p = _trunk(x_prompt.reshape(bp * tp, d), bp, tp, jnp.arange(tp), zero_state, None, GLA_CHUNK,
                             weights_fresh, lbs, chunk=GLA_CHUNK, gla_heads=A_HEADS, tm=ROW_TILE,
                             tm_proj=PROJ_ROW_TILE)

    xs = jnp.pad(x_sample, ((0, 0), (0, SUBLANES - ts), (0, 0))).reshape(bs * SUBLANES, d)
    caches = (cache_win1_kv, cache_win2_kv, cache_win3_kv)
    y_s, st_s, kv_s = _trunk(xs, bs, SUBLANES, PAST_LEN + jnp.arange(SUBLANES), state_hgrn, caches, ts,
                             weights, lbs, chunk=SUBLANES, gla_heads=A_HEADS, tm=bs * SUBLANES,
                             tm_proj=bs * SUBLANES)

    kv_s = kv_s.reshape(bs, SUBLANES, N_GROUPS, 2, KV_HEADS, HEAD_DIM)[:, :ts]
    win_p = [_fresh_window(kv_p, bp, tp, g) for g in range(N_GROUPS)]
    win_s = [kv_s[:, :, g] for g in range(N_GROUPS)]
    return (y_p.reshape(bp, tp, d), y_s.reshape(bs, SUBLANES, d)[:, :ts], st_p, st_s,
            win_p[0], win_p[1], win_p[2], win_s[0], win_s[1], win_s[2])
```

```python
import functools
import math

import jax
import jax.numpy as jnp
import numpy as np
from jax import lax
from jax.experimental import pallas as pl
from jax.experimental.pallas import tpu as pltpu

D_MODEL = 1024
DEPTH = 4
N_A_LAYERS = DEPTH // 2
HEAD_DIM = 128
A_HEADS = D_MODEL // HEAD_DIM
Q_HEADS = D_MODEL // HEAD_DIM
KV_HEADS = 2
Q_PER_KV = Q_HEADS // KV_HEADS
N_GROUPS = 3
WINDOWS = (128, 512, 2048)
DILATIONS = (1, 4, 16)
DIL_KEYS = 128
ROT_DIM = HEAD_DIM // 4
ROT_HALF = ROT_DIM // 2
ROPE_THETA = 500000.0
D_FF = 4 * D_MODEL
EPS = 1e-6
NEG = -1e30
EXP_CLAMP = 80.0
PAST_LEN = 8192

KV_COLS = N_GROUPS * 2 * KV_HEADS * HEAD_DIM
Q_COLS = N_GROUPS * Q_HEADS * HEAD_DIM
SUBLANES = 8
VMEM_LIMIT = 56 * 1024 * 1024

ROW_TILE = 1024
FF_TILE = 1024
PROJ_ROW_TILE = 256
GLA_CHUNK = 128
ATTN_BLOCKS_PER_STEP = 8

_NT = (((1,), (1,)), ((), ()))
_TN = (((0,), (0,)), ((), ()))


def _params(*sem):
    return pltpu.CompilerParams(dimension_semantics=sem, vmem_limit_bytes=VMEM_LIMIT)


def _bf(x):
    return x.astype(jnp.bfloat16)


def _rms_rows(x, gain):
    return x * lax.rsqrt(jnp.mean(x * x, axis=-1, keepdims=True) + EPS) * gain


def _norm_matmul_kernel(x_ref, g_ref, w_ref, *out_refs):
    xn = _bf(_rms_rows(x_ref[...], g_ref[...]))
    tn = w_ref.shape[1] // len(out_refs)
    for part, o_ref in enumerate(out_refs):
        o_ref[...] = jnp.dot(xn, w_ref[:, part * tn:(part + 1) * tn],
                             preferred_element_type=jnp.float32).astype(o_ref.dtype)


def _norm_matmul(x, gain, w, tm, out_dtypes):
    n, d = x.shape
    w, layer = w
    tn = w.shape[2] // len(out_dtypes)
    return pl.pallas_call(
        _norm_matmul_kernel,
        grid=(n // tm,),
        in_specs=[pl.BlockSpec((tm, d), lambda i: (i, 0)),
                  pl.BlockSpec((1, d), lambda i: (0, 0)),
                  pl.BlockSpec((None,) + w.shape[1:], lambda i: (layer, 0, 0))],
        out_specs=[pl.BlockSpec((tm, tn), lambda i: (i, 0)) for _ in out_dtypes],
        out_shape=[jax.ShapeDtypeStruct((n, tn), dt) for dt in out_dtypes],
        compiler_params=_params("parallel"),
    )(x, gain.reshape(1, d), w)


PROJ_HEAD_GROUP = 4


def _proj_heads_kernel(x_ref, g_ref, w_ref, hg_ref, *refs, normed, outs):
    rope, out_refs = [r[...] for r in refs[:len(refs) - len(outs)]], refs[len(refs) - len(outs):]
    xn = _bf(_rms_rows(x_ref[...], g_ref[...]))
    width = PROJ_HEAD_GROUP * HEAD_DIM
    for g0 in range(0, len(normed), PROJ_HEAD_GROUP):
        z = jnp.dot(xn, w_ref[:, g0 * HEAD_DIM:g0 * HEAD_DIM + width], preferred_element_type=jnp.float32)
        hs = range(g0, g0 + PROJ_HEAD_GROUP)
        zs = {h: z[:, (h - g0) * HEAD_DIM:(h - g0 + 1) * HEAD_DIM] for h in hs}
        normed_hs = [h for h in hs if normed[h]]
        ms = {h: jnp.mean(zs[h] * zs[h], axis=-1, keepdims=True) for h in normed_hs}
        for h in normed_hs:
            zs[h] = zs[h] * lax.rsqrt(ms[h] + EPS) * hg_ref[h:h + 1, :]
        if len(rope) == 3:
            cos, sin_fwd, sin_bwd = rope
            fwd = {h: pltpu.roll(zs[h], ROT_HALF, 1) for h in normed_hs}
            bwd = {h: pltpu.roll(zs[h], HEAD_DIM - ROT_HALF, 1) for h in normed_hs}
            for h in normed_hs:
                zs[h] = zs[h] * cos + fwd[h] * sin_fwd + bwd[h] * sin_bwd
        else:
            cos, sin = rope
            partner = {h: pltpu.roll(zs[h], HEAD_DIM // 2, 1) for h in normed_hs}
            for h in normed_hs:
                zs[h] = zs[h] * cos + partner[h] * sin
        for h in hs:
            for o_ref, (_, h0, h1) in zip(out_refs, outs):
                if h0 <= h < h1:
                    o_ref[:, (h - h0) * HEAD_DIM:(h - h0 + 1) * HEAD_DIM] = zs[h].astype(o_ref.dtype)


def _proj_heads(x, gain, w, head_gain, normed, rope, rope_period_blocks, tm, outs):
    n, d = x.shape
    w, layer = w
    m = w.shape[2]
    rope_spec = pl.BlockSpec((tm, HEAD_DIM), lambda i: (i % rope_period_blocks, 0))
    return pl.pallas_call(
        functools.partial(_proj_heads_kernel, normed=normed, outs=outs),
        grid=(n // tm,),
        in_specs=[pl.BlockSpec((tm, d), lambda i: (i, 0)),
                  pl.BlockSpec((1, d), lambda i: (0, 0)),
                  pl.BlockSpec((None, d, m), lambda i: (layer, 0, 0)),
                  pl.BlockSpec(head_gain.shape, lambda i: (0, 0)),
                  *[rope_spec] * len(rope)],
        out_specs=[pl.BlockSpec((tm, (h1 - h0) * HEAD_DIM), lambda i: (i, 0)) for _, h0, h1 in outs],
        out_shape=[jax.ShapeDtypeStruct((n, (h1 - h0) * HEAD_DIM), dt) for dt, h0, h1 in outs],
        compiler_params=_params("parallel"),
    )(x, gain.reshape(1, d), w, head_gain, *rope)


LOG2E = 1.4426950408889634
MXU_SCAN_MIN_CHUNK = 128


TILE_LEVELS = int(math.log2(SUBLANES))


def _gla_tables(chunk):
    n_levels = int(math.log2(chunk))
    t, s = np.arange(chunk)[:, None], np.arange(chunk)[None, :]
    mats, masks, uppers = [], [t == s], []
    for lvl in range(1, n_levels + 1):
        half = 1 << (lvl - 1)
        same_half = (t >> (lvl - 1)) == (s >> (lvl - 1))
        upper = (t & half) != 0
        if lvl <= TILE_LEVELS:
            mats.append(np.where(upper, same_half & (s <= t), same_half & (s > t)))
        masks.append((t >> lvl) == (s >> lvl))
        uppers.append(np.broadcast_to(upper, (chunk, HEAD_DIM)))
    mats += [s <= t, s > t]
    scan = np.concatenate(mats)
    if chunk < MXU_SCAN_MIN_CHUNK:
        scan = np.zeros((2 * SUBLANES, HEAD_DIM))
    return scan.astype(np.float32), np.stack(masks).astype(np.float32), np.stack(uppers).astype(np.float32)


def _gla_kernel(zq_ref, zf_ref, zi_ref, zo_ref, lb_ref, og_ref, s0_ref, scan_ref, same_ref, upper_ref,
                a_ref, sfin_ref, st_ref, *, chunk, heads, t_valid):
    c = pl.program_id(2)
    n_levels = int(math.log2(chunk))
    mxu_scan = chunk >= MXU_SCAN_MIN_CHUNK
    row = lax.broadcasted_iota(jnp.int32, (chunk, HEAD_DIM), 0)

    @pl.when(c == 0)
    def _():
        for h in range(heads):
            st_ref[h] = s0_ref[0, 0, h].T

    hs = range(heads)
    cols = [slice(h * HEAD_DIM, (h + 1) * HEAD_DIM) for h in hs]
    qs, ks, vs, log_fs = [], [], [], []
    for h in hs:
        zq, zf = zq_ref[0, :, cols[h]].astype(jnp.float32), zf_ref[0, :, cols[h]]
        lb = lb_ref[:, cols[h]]
        q = zq / (1.0 + jnp.exp(-zq))
        log_f = LOG2E * (jnp.minimum(zf, 0.0) - jnp.log(1.0 + jnp.exp(-jnp.abs(zf)))
                         + jnp.log(1.0 + lb * jnp.exp(jnp.minimum(-zf, EXP_CLAMP))))
        k = (1.0 - lb) / (1.0 + jnp.exp(zf))
        if t_valid < chunk:
            live = row < t_valid
            log_f = jnp.where(live, log_f, 0.0)
            k = jnp.where(live, k, 0.0)
        qs.append(q), ks.append(k), vs.append(zi_ref[0, :, cols[h]]), log_fs.append(log_f)

    scores = [same_ref[0] * lax.dot_general(_bf(qs[h]), _bf(ks[h]), _NT, preferred_element_type=jnp.float32)
              for h in hs]
    q_ins, k_ends, decays = [], [], []
    if mxu_scan:
        scan = scan_ref[...]
        x_alls = []
        for h in hs:
            g2 = log_fs[h]
            hi = _bf(g2)
            rest = g2 - hi.astype(jnp.float32)
            mid = _bf(rest)
            lo = _bf(rest - mid.astype(jnp.float32))
            x_alls.append(jnp.dot(scan, hi, preferred_element_type=jnp.float32)
                          + jnp.dot(scan, mid, preferred_element_type=jnp.float32)
                          + jnp.dot(scan, lo, preferred_element_type=jnp.float32))
        prefixes = [x[TILE_LEVELS * chunk:(TILE_LEVELS + 1) * chunk] for x in x_alls]
        for lvl in range(1, n_levels + 1):
            half = 1 << (lvl - 1)
            for h in hs:
                q, k, pre = qs[h], ks[h], prefixes[h]
                if half >= SUBLANES:
                    zeros = jnp.zeros((half, HEAD_DIM), jnp.float32)
                    q_parts, k_parts = [], []
                    for start in range(0, chunk, 2 * half):
                        low, up = slice(start, start + half), slice(start + half, start + 2 * half)
                        edge = pre[start + half - 1:start + half]
                        k_parts += [k[low] * jnp.exp2(edge - pre[low]), zeros]
                        q_parts += [zeros, q[up] * jnp.exp2(pre[up] - edge)]
                    q_l, k_l = jnp.concatenate(q_parts, axis=0), jnp.concatenate(k_parts, axis=0)
                else:
                    e = jnp.exp2(x_alls[h][(lvl - 1) * chunk:lvl * chunk])
                    e_up = e * upper_ref[lvl - 1]
                    q_l, k_l = q * e_up, k * (e - e_up)
                s_l = lax.dot_general(_bf(q_l), _bf(k_l), _NT, preferred_element_type=jnp.float32)
                scores[h] = scores[h] + same_ref[lvl] * s_l
        for h in hs:
            q_ins.append(qs[h] * jnp.exp2(prefixes[h]))
            k_ends.append(ks[h] * jnp.exp2(x_alls[h][(TILE_LEVELS + 1) * chunk:]))
            decays.append(jnp.exp2(prefixes[h][chunk - 1:chunk]))
    else:
        for h in hs:
            q, k, pre, tot = qs[h], ks[h], log_fs[h], log_fs[h]
            for lvl in range(1, n_levels + 1):
                half = 1 << (lvl - 1)
                upper = (row & half) != 0
                e = jnp.exp2(jnp.where(upper, pre, tot - pre))
                q_l = jnp.where(upper, q * e, 0.0)
                k_l = jnp.where(upper, 0.0, k * e)
                s_l = lax.dot_general(_bf(q_l), _bf(k_l), _NT, preferred_element_type=jnp.float32)
                scores[h] = scores[h] + same_ref[lvl] * s_l
                tot_sib = jnp.where(upper, pltpu.roll(tot, half, 0), pltpu.roll(tot, chunk - half, 0))
                pre = pre + jnp.where(upper, tot_sib, 0.0)
                tot = tot + tot_sib
            q_ins.append(q * jnp.exp2(pre)), k_ends.append(k * jnp.exp2(tot - pre))
            decays.append(jnp.exp2(tot[0:1, :]))

    for h in hs:
        st = st_ref[h]
        o = lax.dot_general(_bf(q_ins[h]), _bf(st), _NT, preferred_element_type=jnp.float32)
        o = o + jnp.dot(_bf(scores[h]), _bf(vs[h]), preferred_element_type=jnp.float32)
        st_ref[h] = st * decays[h] + lax.dot_general(_bf(vs[h]), _bf(k_ends[h]), _TN,
                                                     preferred_element_type=jnp.float32)
        zo = zo_ref[0, :, cols[h]].astype(jnp.float32)
        a = _rms_rows(o, og_ref[:, cols[h]]) / (1.0 + jnp.exp(-zo))
        a_ref[0, :, cols[h]] = a.astype(a_ref.dtype)

    @pl.when(c == pl.num_programs(2) - 1)
    def _():
        for h in range(heads):
            sfin_ref[0, 0, h] = st_ref[h].T


def _gla(zs, lb, out_gain, states, layer, chunk, heads, t_valid):
    b, t, _ = zs[0].shape
    hw = heads * HEAD_DIM
    nh = A_HEADS // heads
    scan, same, upper = _gla_tables(chunk)
    zspec = pl.BlockSpec((1, chunk, hw), lambda bi, hi, ci: (bi, ci, hi))
    vec_spec = pl.BlockSpec((1, hw), lambda bi, hi, ci: (0, hi))
    st_block = (1, 1, heads, HEAD_DIM, HEAD_DIM)
    whole = lambda a: pl.BlockSpec(a.shape, lambda bi, hi, ci: (0,) * a.ndim)
    return pl.pallas_call(
        functools.partial(_gla_kernel, chunk=chunk, heads=heads, t_valid=t_valid),
        grid=(b, nh, t // chunk),
        in_specs=[zspec, zspec, zspec, zspec, vec_spec, vec_spec,
                  pl.BlockSpec(st_block, lambda bi, hi, ci: (layer, bi, hi, 0, 0)),
                  whole(scan), whole(same), whole(upper)],
        out_specs=[pl.BlockSpec((1, chunk, hw), lambda bi, hi, ci: (bi, ci, hi)),
                   pl.BlockSpec(st_block, lambda bi, hi, ci: (0, bi, hi, 0, 0))],
        out_shape=[jax.ShapeDtypeStruct((b, t, D_MODEL), jnp.bfloat16),
                   jax.ShapeDtypeStruct((1,) + states.shape[1:], jnp.float32)],
        scratch_shapes=[pltpu.VMEM((heads, HEAD_DIM, HEAD_DIM), jnp.float32)],
        compiler_params=_params("parallel", "parallel", "arbitrary"),
    )(*zs, lb.reshape(1, D_MODEL), out_gain.reshape(1, D_MODEL), states,
      jnp.asarray(scan, jnp.bfloat16), jnp.asarray(same), jnp.asarray(upper))


def _post_mlp_kernel(x_ref, a_ref, wp_ref, g_ref, wu_ref, wd_ref, out_ref, xn_ref):
    @pl.when(pl.program_id(1) == 0)
    def _():
        x1 = x_ref[...] + jnp.dot(a_ref[...], wp_ref[...], preferred_element_type=jnp.float32)
        out_ref[...] = x1
        xn_ref[...] = _bf(_rms_rows(x1, g_ref[...]))

    hdn = jnp.maximum(jnp.dot(xn_ref[...], wu_ref[...], preferred_element_type=jnp.float32), 0.0)
    out_ref[...] += jnp.dot(_bf(hdn * hdn), wd_ref[...], preferred_element_type=jnp.float32)


def _post_mlp(x, a, wp, gain, wu, wd, tm, tf):
    n, d = x.shape
    (wp, lp), (wu, lu), (wd, ld) = wp, wu, wd
    f = wu.shape[2]
    row_spec = pl.BlockSpec((tm, d), lambda i, j: (i, 0))
    return pl.pallas_call(
        _post_mlp_kernel,
        grid=(n // tm, f // tf),
        in_specs=[row_spec, row_spec,
                  pl.BlockSpec((None, d, d), lambda i, j: (lp, 0, 0)),
                  pl.BlockSpec((1, d), lambda i, j: (0, 0)),
                  pl.BlockSpec((None, d, tf), lambda i, j: (lu, 0, j)),
                  pl.BlockSpec((None, tf, d), lambda i, j: (ld, j, 0))],
        out_specs=row_spec,
        out_shape=jax.ShapeDtypeStruct((n, d), jnp.float32),
        scratch_shapes=[pltpu.VMEM((tm, d), jnp.bfloat16)],
        compiler_params=_params("parallel", "arbitrary"),
    )(x, a, wp, gain.reshape(1, d), wu, wd)


RES = DILATIONS[-1]


def _block_pieces(group):
    pieces = RES // DILATIONS[group]
    return pieces, DIL_KEYS // pieces


def _band_bias(group):
    pieces, per = _block_pieces(group)
    i = np.arange(DIL_KEYS)
    idx = pieces * (i % per) + i // per
    k_true = np.concatenate([idx, DIL_KEYS + idx])
    delta = (DIL_KEYS + idx)[:, None] - k_true[None, :]
    band = (delta >= 0) & (delta <= DIL_KEYS)
    first = band & (np.arange(2 * DIL_KEYS) >= DIL_KEYS)[None, :]
    return np.where(np.stack([first, band]), 0.0, NEG).astype(np.float32)


def _attn_prompt_kernel(bias_ref, q_ref, kp_ref, kc_ref, vp_ref, vc_ref, *refs, pieces, per, qb, n_merge):
    p = DIL_KEYS
    others, out_refs = refs[:2 * n_merge], refs[2 * n_merge:]

    def rows(ref, j, cols, dtype=jnp.bfloat16):
        return jnp.concatenate([ref[0, c, 0, j * per:(j + 1) * per, cols] for c in range(pieces)],
                               axis=0).astype(dtype)

    def put(ref, j, cols, val):
        for c in range(pieces):
            ref[0, c, 0, j * per:(j + 1) * per, cols] = val[c * per:(c + 1) * per].astype(ref.dtype)

    lane = lax.broadcasted_iota(jnp.int32, (p, HEAD_DIM), 1)
    every = slice(None)
    for j in range(qb):
        table = jnp.minimum(pl.program_id(2), 1) if j == 0 else 1
        bias = jnp.concatenate([bias_ref[table]] * Q_PER_KV, axis=0)
        lse_all = jnp.zeros((p, HEAD_DIM), jnp.float32)
        o_heads = {}
        for kvh in range(KV_HEADS):
            kcols = slice(kvh * HEAD_DIM, (kvh + 1) * HEAD_DIM)
            if j == 0:
                k_before, v_before = rows(kp_ref, 0, kcols), rows(vp_ref, 0, kcols)
            else:
                k_before, v_before = rows(kc_ref, j - 1, kcols), rows(vc_ref, j - 1, kcols)
            keys = jnp.concatenate([k_before, rows(kc_ref, j, kcols)], axis=0)
            vals = jnp.concatenate([v_before, rows(vc_ref, j, kcols)], axis=0)
            heads = [kvh * Q_PER_KV + r for r in range(Q_PER_KV)]
            q = jnp.concatenate([rows(q_ref, j, slice(h * HEAD_DIM, (h + 1) * HEAD_DIM)) for h in heads], axis=0)
            s2 = (lax.dot_general(q, keys, _NT, preferred_element_type=jnp.float32) * (HEAD_DIM ** -0.5 * LOG2E)
                  + bias)
            m2 = jnp.max(s2, axis=-1, keepdims=True)
            pr = jnp.exp2(s2 - m2)
            den = jnp.sum(pr, axis=-1, keepdims=True)
            o = jnp.dot(_bf(pr), vals, preferred_element_type=jnp.float32) / den
            lse = m2 * (1.0 / LOG2E) + jnp.log(den)
            for r, h in enumerate(heads):
                o_heads[h] = o[r * p:(r + 1) * p]
                lse_all = jnp.where(lane == h, lse[r * p:(r + 1) * p], lse_all)
        if n_merge == 0:
            o_ref, lse_ref = out_refs
            for h in range(Q_HEADS):
                put(o_ref, j, slice(h * HEAD_DIM, (h + 1) * HEAD_DIM), o_heads[h])
            put(lse_ref, j, every, lse_all)
        else:
            (a_ref,) = out_refs
            lses = [rows(others[2 * g + 1], j, every, jnp.float32) for g in range(n_merge)] + [lse_all]
            top = functools.reduce(jnp.maximum, lses)
            w = [jnp.exp(l - top) for l in lses]
            inv = 1.0 / sum(w)
            w = [wg * inv for wg in w[:-1]]
            for h in range(Q_HEADS):
                cols = slice(h * HEAD_DIM, (h + 1) * HEAD_DIM)
                own = o_heads[h]
                put(a_ref, j, cols, own + sum(w[g][:, h:h + 1] * (rows(others[2 * g], j, cols, jnp.float32) - own)
                                              for g in range(n_merge)))


def _attn_prompt(q, q_col, kv, group, b, s, merge_with=(), qb=ATTN_BLOCKS_PER_STEP):
    dil = DILATIONS[group]
    pieces, per = _block_pieces(group)
    lr = s // RES
    qb = min(qb, lr // per)
    view = lambda a: a.reshape(b, pieces, dil, lr, a.shape[-1])
    kw = KV_HEADS * HEAD_DIM

    def spec(width, col):
        return pl.BlockSpec((1, pieces, 1, qb * per, width), lambda bi, r, n: (bi, 0, r, n, col))

    def before_spec(col):
        return pl.BlockSpec((1, pieces, 1, per, kw), lambda bi, r, n: (bi, 0, r, jnp.maximum(n * qb - 1, 0), col))

    band = jnp.asarray(_band_bias(group))
    other_args = [view(a) for pair in merge_with for a in pair]
    other_specs = [spec(D_MODEL, 0), spec(HEAD_DIM, 0)] * len(merge_with)
    o_shape = jax.ShapeDtypeStruct((b, pieces, dil, lr, D_MODEL), jnp.bfloat16)
    if merge_with:
        out_specs, out_shape = [spec(D_MODEL, 0)], [o_shape]
    else:
        out_specs = [spec(D_MODEL, 0), spec(HEAD_DIM, 0)]
        out_shape = [o_shape, jax.ShapeDtypeStruct((b, pieces, dil, lr, HEAD_DIM), jnp.float32)]
    outs = pl.pallas_call(
        functools.partial(_attn_prompt_kernel, pieces=pieces, per=per, qb=qb, n_merge=len(merge_with)),
        grid=(b, dil, lr // (per * qb)),
        in_specs=[pl.BlockSpec(band.shape, lambda bi, r, n: (0, 0, 0)),
                  spec(D_MODEL, q_col),
                  before_spec(2 * group), spec(kw, 2 * group),
                  before_spec(2 * group + 1), spec(kw, 2 * group + 1),
                  *other_specs],
        out_specs=out_specs,
        out_shape=out_shape,
        compiler_params=_params("parallel", "parallel", "arbitrary"),
    )(band, view(q), view(kv), view(kv), view(kv), view(kv), *other_args)
    return tuple(a.reshape(b * s, a.shape[-1]) for a in outs)


def _sample_key_blocks(n_new):
    n_cache, tables = [], []
    i = np.arange(DIL_KEYS)
    for g, (win, dil) in enumerate(zip(WINDOWS, DILATIONS)):
        n_past = min(win, PAST_LEN)
        if n_past // DIL_KEYS <= SUBLANES:
            blocks = [DIL_KEYS * k + i for k in range(n_past // DIL_KEYS)]
        else:
            assert n_past == RES * DIL_KEYS and n_new <= RES and dil == RES
            blocks = [RES * i + r for r in range(n_new)]
        n_cache.append(len(blocks))
        blocks.append(np.where(i < SUBLANES, n_past + i, -10 ** 9))
        for pos in blocks:
            t = np.arange(SUBLANES)[:, None]
            delta = n_past + t - pos[None, :]
            ok = (delta >= 0) & (delta % dil == 0) & (delta // dil <= DIL_KEYS)
            ok = np.where(t < n_new, ok, True)
            tables.append(ok)
    return n_cache, np.stack(tables).astype(np.float32)


def _attn_sample_kernel(q_ref, kvn_ref, c1_ref, c2_ref, c3_ref, ok_ref, a_ref, *, n_cache):
    rec = 2 * KV_HEADS * HEAD_DIM
    caches = (c1_ref, c2_ref, c3_ref)
    pad = jnp.zeros((DIL_KEYS - SUBLANES, HEAD_DIM), jnp.float32)

    def cache_rows(g, k, is_v, kvh):
        ref = caches[g]
        if len(ref.shape) == 5:
            return ref[0, k * DIL_KEYS:(k + 1) * DIL_KEYS, is_v, kvh, :]
        return ref[0, :, k, is_v, kvh, :]

    for kvh in range(KV_HEADS):
        heads = [kvh * Q_PER_KV + r for r in range(Q_PER_KV)]
        outs, lses = [], []
        blk = 0
        for g in range(N_GROUPS):
            q = jnp.concatenate(
                [q_ref[0, :, (g * Q_HEADS + h) * HEAD_DIM:(g * Q_HEADS + h + 1) * HEAD_DIM] for h in heads], axis=0)
            scores, values, oks = [], [], []
            for k in range(n_cache[g] + 1):
                if k < n_cache[g]:
                    keys, vals = _bf(cache_rows(g, k, 0, kvh)), _bf(cache_rows(g, k, 1, kvh))
                else:
                    base = g * rec + kvh * HEAD_DIM
                    keys = _bf(jnp.concatenate([kvn_ref[0, :, base:base + HEAD_DIM], pad], axis=0))
                    vals = _bf(jnp.concatenate(
                        [kvn_ref[0, :, base + KV_HEADS * HEAD_DIM:base + (KV_HEADS + 1) * HEAD_DIM], pad], axis=0))
                ok = jnp.concatenate([ok_ref[blk]] * Q_PER_KV, axis=0) > 0.5
                blk += 1
                s = lax.dot_general(q, keys, _NT, preferred_element_type=jnp.float32) * HEAD_DIM ** -0.5
                scores.append(jnp.where(ok, s, NEG))
                values.append(vals)
                oks.append(ok)
            m = functools.reduce(jnp.maximum, [jnp.max(s, axis=-1, keepdims=True) for s in scores])
            probs = [jnp.where(ok, jnp.exp(s - m), 0.0) for s, ok in zip(scores, oks)]
            den = sum(jnp.sum(pr, axis=-1, keepdims=True) for pr in probs)
            o = sum(jnp.dot(_bf(pr), vals, preferred_element_type=jnp.float32) for pr, vals in zip(probs, values))
            outs.append(o / den)
            lses.append(m + jnp.log(den))
        top = functools.reduce(jnp.maximum, lses)
        w = [jnp.exp(l - top) for l in lses]
        wsum = sum(w)
        merged = sum((wg / wsum) * og for wg, og in zip(w, outs))
        for r, h in enumerate(heads):
            a_ref[0, :, h * HEAD_DIM:(h + 1) * HEAD_DIM] = merged[r * SUBLANES:(r + 1) * SUBLANES].astype(a_ref.dtype)


def _attn_sample(q, kv_new, caches, n_new):
    b = q.shape[0]
    rec = 2 * KV_HEADS * HEAD_DIM
    n_cache, table = _sample_key_blocks(n_new)
    cache_views, cache_specs = [], []
    rec_shape = (2, KV_HEADS, HEAD_DIM)
    for cache in caches:
        n_past = cache.shape[1]
        if n_past // DIL_KEYS <= SUBLANES:
            cache_views.append(cache)
            cache_specs.append(pl.BlockSpec((1, n_past) + rec_shape, lambda bi: (bi, 0, 0, 0, 0)))
        else:
            cache_views.append(cache.reshape((b, n_past // RES, RES) + rec_shape))
            cache_specs.append(pl.BlockSpec((1, n_past // RES, n_new) + rec_shape, lambda bi: (bi, 0, 0, 0, 0, 0)))
    return pl.pallas_call(
        functools.partial(_attn_sample_kernel, n_cache=n_cache),
        grid=(b,),
        in_specs=[pl.BlockSpec((1, SUBLANES, Q_COLS), lambda bi: (bi, 0, 0)),
                  pl.BlockSpec((1, SUBLANES, KV_COLS), lambda bi: (bi, 0, 0)),
                  *cache_specs,
                  pl.BlockSpec(table.shape, lambda bi: (0, 0, 0))],
        out_specs=pl.BlockSpec((1, SUBLANES, D_MODEL), lambda bi: (bi, 0, 0)),
        out_shape=jax.ShapeDtypeStruct((b, SUBLANES, D_MODEL), jnp.bfloat16),
        compiler_params=_params("parallel"),
    )(q, kv_new, *cache_views, jnp.asarray(table))


def _rope_tables(pos):
    inv = ROPE_THETA ** (-2.0 * jnp.arange(ROT_HALF, dtype=jnp.float32) / ROT_DIM)
    ang = pos.astype(jnp.float32)[:, None] * inv[None, :]
    cos, sin = jnp.cos(ang), jnp.sin(ang)
    rest = HEAD_DIM - ROT_DIM
    one, zero = jnp.ones((pos.shape[0], rest), jnp.float32), jnp.zeros((pos.shape[0], rest), jnp.float32)
    zh = jnp.zeros_like(sin)
    return (jnp.concatenate([cos, cos, one], axis=1),
            jnp.concatenate([zh, sin, zero], axis=1),
            jnp.concatenate([-sin, zh, zero], axis=1))


SPREAD_RUNS = ((0, ROT_HALF), (ROT_DIM, HEAD_DIM // 2 + ROT_HALF), (ROT_HALF, ROT_DIM),
               (HEAD_DIM // 2 + ROT_HALF, HEAD_DIM))
UNSPREAD_RUNS = ((0, ROT_HALF), (HEAD_DIM // 2, HEAD_DIM // 2 + ROT_HALF), (ROT_HALF, HEAD_DIM // 2),
                 (HEAD_DIM // 2 + ROT_HALF, HEAD_DIM))


def _reorder_lanes(x, runs):
    return jnp.concatenate([x[..., a:b] for a, b in runs], axis=-1)


def _to_residue_major(a, b, t):
    return a.reshape(b, t // RES, RES, a.shape[-1]).swapaxes(1, 2).reshape(b * t, a.shape[-1])


def _from_residue_major(a, b, t):
    return a.reshape(b, RES, t // RES, a.shape[-1]).swapaxes(1, 2).reshape(b * t, a.shape[-1])


def _trunk(x, b, t, pos, state0, caches, n_new, weights, lbs, chunk, gla_heads, tm, tm_proj):
    (a_norm, a_w_in, a_out_norm, a_w_out, kv_norm, w_kv, k_norm,
     b_norm, b_w_q, q_norm, b_w_o, mlp_norm, mlp_w_up, mlp_w_down) = weights
    n = b * t
    tf = FF_TILE
    fresh = caches is None
    rope = _rope_tables(pos)
    n_heads_q, n_heads_kv = Q_COLS // HEAD_DIM, KV_COLS // HEAD_DIM
    kv_normed = tuple(bool((hh // KV_HEADS) % 2 == 0) for hh in range(n_heads_kv))
    kv_gain = jnp.repeat(jnp.repeat(k_norm, KV_HEADS, axis=0), 2, axis=0)
    q_gains = [jnp.repeat(q_norm[j], Q_HEADS, axis=0) for j in range(DEPTH - N_A_LAYERS)]
    if fresh:
        cos, sin_fwd, sin_bwd = rope
        rope = tuple(_to_residue_major(_reorder_lanes(r, SPREAD_RUNS), 1, t) for r in (cos, sin_fwd + sin_bwd))
        kv_gain = _reorder_lanes(kv_gain, SPREAD_RUNS)
        q_gains = [_reorder_lanes(g, SPREAD_RUNS) for g in q_gains]
        rope_period = t // tm_proj
    else:
        rope = tuple(jnp.tile(r, (tm_proj // t, 1)) for r in rope)
        rope_period = 1
    finals = []
    kv32 = kv16 = None
    for layer in range(DEPTH):
        if layer < N_A_LAYERS:
            zs = _norm_matmul(x, a_norm[layer], (a_w_in, layer), tm,
                              (jnp.bfloat16, jnp.float32, jnp.bfloat16, jnp.bfloat16))
            a, s_fin = _gla([z.reshape(b, t, D_MODEL) for z in zs], lbs[layer], a_out_norm[layer], state0, layer,
                            chunk, gla_heads, n_new)
            finals.append(s_fin)
            x = _post_mlp(x, a.reshape(n, D_MODEL), (a_w_out, layer), mlp_norm[layer], (mlp_w_up, layer),
                          (mlp_w_down, layer), tm, tf)
            continue
        j = layer - N_A_LAYERS
        if j == 0:
            if fresh:
                x = _to_residue_major(x, b, t)
            kv32, kv16 = _proj_heads(x, kv_norm, (w_kv[None], 0), kv_gain, kv_normed, rope, rope_period, tm_proj,
                                     ((jnp.float32, 0, n_heads_kv), (jnp.bfloat16, 0, n_heads_kv)))
        q_gain = q_gains[j]
        q_normed = (True,) * n_heads_q
        if fresh:
            q_near, q_far = _proj_heads(x, b_norm[j], (b_w_q, j), q_gain, q_normed, rope, rope_period, tm_proj,
                                        ((jnp.float32, 0, Q_HEADS), (jnp.bfloat16, Q_HEADS, n_heads_q)))
            near = _attn_prompt(q_near, 0, kv32, 0, b, t)
            mid = _attn_prompt(q_far, 0, kv16, 1, b, t)
            (mixer_in,) = _attn_prompt(q_far, 1, kv16, 2, b, t, merge_with=(near, mid))
        else:
            (q,) = _proj_heads(x, b_norm[j], (b_w_q, j), q_gain, q_normed, rope, rope_period, tm_proj,
                               ((jnp.bfloat16, 0, n_heads_q),))
            a = _attn_sample(q.reshape(b, t, Q_COLS), kv32.reshape(b, t, KV_COLS), caches, n_new)
            mixer_in = a.reshape(n, D_MODEL)
        x = _post_mlp(x, mixer_in, (b_w_o, j), mlp_norm[layer], (mlp_w_up, layer), (mlp_w_down, layer), tm, tf)
    if fresh:
        x = _from_residue_major(x, b, t)
    return x, jnp.concatenate(finals), kv32


def _fresh_window(kv, b, t, group):
    rows, rec = min(WINDOWS[group], t), 2 * KV_HEADS * HEAD_DIM
    a = kv.reshape(b, RES, t // RES, KV_COLS)[:, :, (t - rows) // RES:, group * rec:(group + 1) * rec]
    a = a.swapaxes(1, 2).reshape(b, rows, 2, KV_HEADS, HEAD_DIM)
    return jnp.concatenate([_reorder_lanes(a[:, :, :1], UNSPREAD_RUNS), a[:, :, 1:]], axis=2)


def _cast_kernel(w_ref, o_ref, *spread_refs, spread):
    o_ref[...] = w_ref[...].astype(o_ref.dtype)
    for s_ref in spread_refs:
        for h, flag in enumerate(spread):
            cols = slice(h * HEAD_DIM, (h + 1) * HEAD_DIM)
            head = w_ref[:, cols]
            s_ref[:, cols] = (_reorder_lanes(head, SPREAD_RUNS) if flag else head).astype(s_ref.dtype)


def _to_bf16(w, spread=(), row_block=512):
    rows, cols = math.prod(w.shape[:-1]), w.shape[-1]
    spec = pl.BlockSpec((row_block, cols), lambda i: (i, 0))
    n_out = 2 if spread else 1
    outs = pl.pallas_call(
        functools.partial(_cast_kernel, spread=spread),
        grid=(rows // row_block,), in_specs=[spec], out_specs=[spec] * n_out,
        out_shape=[jax.ShapeDtypeStruct((rows, cols), jnp.bfloat16)] * n_out,
        compiler_params=_params("parallel"),
    )(w.reshape(rows, cols))
    outs = [o.reshape(w.shape) for o in outs]
    return outs if spread else outs[0]


def _prepare_weights(a_norm, a_w_in, a_out_norm, a_w_out, kv_norm, w_kv, k_norm, b_norm, b_w_q, q_norm, b_w_o,
                     mlp_norm, mlp_w_up, mlp_w_down):
    kv_is_k = tuple(bool((hh // KV_HEADS) % 2 == 0) for hh in range(KV_COLS // HEAD_DIM))
    w_kv_bf, w_kv_spread = _to_bf16(w_kv, spread=kv_is_k)
    w_q_bf, w_q_spread = _to_bf16(b_w_q, spread=(True,) * (Q_COLS // HEAD_DIM))
    w_in, w_out, w_o, w_up, w_down = (_to_bf16(w) for w in (a_w_in, a_w_out, b_w_o, mlp_w_up, mlp_w_down))
    pack = lambda kv_w, q_w: (a_norm, w_in, a_out_norm, w_out, kv_norm, kv_w, k_norm,
                              b_norm, q_w, q_norm, w_o, mlp_norm, w_up, w_down)
    return pack(w_kv_bf, w_q_bf), pack(w_kv_spread, w_q_spread)


def kernel(x_prompt, x_sample, state_hgrn, cache_win1_kv, cache_win2_kv, cache_win3_kv, a_norm, a_w_in, a_lb_logits, a_out_norm, a_w_out, kv_norm, w_kv, k_norm, b_norm, b_w_q, q_norm, b_w_o, mlp_norm, mlp_w_up, mlp_w_down):
    bp, tp, d = x_prompt.shape
    bs, ts, _ = x_sample.shape
    sm = jax.nn.softmax(a_lb_logits.astype(jnp.float32), axis=0)
    lbs = jnp.cumsum(sm, axis=0) - sm[0]
    weights, weights_fresh = _prepare_weights(a_norm, a_w_in, a_out_norm, a_w_out, kv_norm, w_kv, k_norm, b_norm,
                                              b_w_q, q_norm, b_w_o, mlp_norm, mlp_w_up, mlp_w_down)

    zero_state = jnp.zeros((N_A_LAYERS, bp, A_HEADS, HEAD_DIM, HEAD_DIM), jnp.float32)
    y_p, st_p, kv_p = _trunk(x_prompt.reshape(bp * tp, d), bp, tp, jnp.arange(tp), zero_state, None, GLA_CHUNK,
                             weights_fresh, lbs, chunk=GLA_CHUNK, gla_heads=A_HEADS, tm=ROW_TILE,
                             tm_proj=PROJ_ROW_TILE)

    xs = jnp.pad(x_sample, ((0, 0), (0, SUBLANES - ts), (0, 0))).reshape(bs * SUBLANES, d)
    caches = (cache_win1_kv, cache_win2_kv, cache_win3_kv)
    y_s, st_s, kv_s = _trunk(xs, bs, SUBLANES, PAST_LEN + jnp.arange(SUBLANES), state_hgrn, caches, ts,
                             weights, lbs, chunk=SUBLANES, gla_heads=A_HEADS, tm=bs * SUBLANES,
                             tm_proj=bs * SUBLANES)

    kv_s = kv_s.reshape(bs, SUBLANES, N_GROUPS, 2, KV_HEADS, HEAD_DIM)[:, :ts]
    win_p = [_fresh_window(kv_p, bp, tp, g) for g in range(N_GROUPS)]
    win_s = [kv_s[:, :, g] for g in range(N_GROUPS)]
    return (y_p.reshape(bp, tp, d), y_s.reshape(bs, SUBLANES, d)[:, :ts], st_p, st_s,
            win_p[0], win_p[1], win_p[2], win_s[0], win_s[1], win_s[2])
```

```python
import functools
import math

import jax
import jax.numpy as jnp
import numpy as np
from jax import lax
from jax.experimental import pallas as pl
from jax.experimental.pallas import tpu as pltpu

D_MODEL = 1024
DEPTH = 4
N_A_LAYERS = DEPTH // 2
HEAD_DIM = 128
A_HEADS = D_MODEL // HEAD_DIM
Q_HEADS = D_MODEL // HEAD_DIM
KV_HEADS = 2
Q_PER_KV = Q_HEADS // KV_HEADS
N_GROUPS = 3
WINDOWS = (128, 512, 2048)
DILATIONS = (1, 4, 16)
DIL_KEYS = 128
ROT_DIM = HEAD_DIM // 4
ROT_HALF = ROT_DIM // 2
ROPE_THETA = 500000.0
D_FF = 4 * D_MODEL
EPS = 1e-6
NEG = -1e30
EXP_CLAMP = 80.0
PAST_LEN = 8192

KV_COLS = N_GROUPS * 2 * KV_HEADS * HEAD_DIM
Q_COLS = N_GROUPS * Q_HEADS * HEAD_DIM
SUBLANES = 8
VMEM_LIMIT = 56 * 1024 * 1024

ROW_TILE = 1024
FF_TILE = 1024
PROJ_ROW_TILE = 256
GLA_CHUNK = 128
ATTN_BLOCKS_PER_STEP = 8

_NT = (((1,), (1,)), ((), ()))
_TN = (((0,), (0,)), ((), ()))


def _params(*sem):
    return pltpu.CompilerParams(dimension_semantics=sem, vmem_limit_bytes=VMEM_LIMIT)


def _bf(x):
    return x.astype(jnp.bfloat16)


def _rms_rows(x, gain):
    return x * lax.rsqrt(jnp.mean(x * x, axis=-1, keepdims=True) + EPS) * gain


def _norm_matmul_kernel(x_ref, g_ref, w_ref, *out_refs):
    xn = _bf(_rms_rows(x_ref[...], g_ref[...]))
    tn = w_ref.shape[1] // len(out_refs)
    for part, o_ref in enumerate(out_refs):
        o_ref[...] = jnp.dot(xn, w_ref[:, part * tn:(part + 1) * tn],
                             preferred_element_type=jnp.float32).astype(o_ref.dtype)


def _norm_matmul(x, gain, w, tm, out_dtypes):
    n, d = x.shape
    w, layer = w
    tn = w.shape[2] // len(out_dtypes)
    return pl.pallas_call(
        _norm_matmul_kernel,
        grid=(n // tm,),
        in_specs=[pl.BlockSpec((tm, d), lambda i: (i, 0)),
                  pl.BlockSpec((1, d), lambda i: (0, 0)),
                  pl.BlockSpec((None,) + w.shape[1:], lambda i: (layer, 0, 0))],
        out_specs=[pl.BlockSpec((tm, tn), lambda i: (i, 0)) for _ in out_dtypes],
        out_shape=[jax.ShapeDtypeStruct((n, tn), dt) for dt in out_dtypes],
        compiler_params=_params("parallel"),
    )(x, gain.reshape(1, d), w)


PROJ_HEAD_GROUP = 4


def _proj_heads_kernel(x_ref, g_ref, w_ref, hg_ref, *refs, normed, outs, gain_of):
    rope, out_refs = [r[...] for r in refs[:len(refs) - len(outs)]], refs[len(refs) - len(outs):]
    xns = [_bf(_rms_rows(x_ref[...], g_ref[i:i + 1, :])) for i in range(g_ref.shape[0])]
    width = PROJ_HEAD_GROUP * HEAD_DIM
    for g0 in range(0, len(normed), PROJ_HEAD_GROUP):
        z = jnp.dot(xns[gain_of[g0 // PROJ_HEAD_GROUP]], w_ref[:, g0 * HEAD_DIM:g0 * HEAD_DIM + width],
                    preferred_element_type=jnp.float32)
        hs = range(g0, g0 + PROJ_HEAD_GROUP)
        zs = {h: z[:, (h - g0) * HEAD_DIM:(h - g0 + 1) * HEAD_DIM] for h in hs}
        normed_hs = [h for h in hs if normed[h]]
        ms = {h: jnp.mean(zs[h] * zs[h], axis=-1, keepdims=True) for h in normed_hs}
        for h in normed_hs:
            zs[h] = zs[h] * lax.rsqrt(ms[h] + EPS) * hg_ref[h:h + 1, :]
        if len(rope) == 3:
            cos, sin_fwd, sin_bwd = rope
            fwd = {h: pltpu.roll(zs[h], ROT_HALF, 1) for h in normed_hs}
            bwd = {h: pltpu.roll(zs[h], HEAD_DIM - ROT_HALF, 1) for h in normed_hs}
            for h in normed_hs:
                zs[h] = zs[h] * cos + fwd[h] * sin_fwd + bwd[h] * sin_bwd
        else:
            cos, sin = rope
            partner = {h: pltpu.roll(zs[h], HEAD_DIM // 2, 1) for h in normed_hs}
            for h in normed_hs:
                zs[h] = zs[h] * cos + partner[h] * sin
        for h in hs:
            for o_ref, (_, h0, h1) in zip(out_refs, outs):
                if h0 <= h < h1:
                    o_ref[:, (h - h0) * HEAD_DIM:(h - h0 + 1) * HEAD_DIM] = zs[h].astype(o_ref.dtype)


def _proj_heads(x, gain, w, head_gain, normed, rope, rope_period_blocks, tm, outs, gain_of=None):
    n, d = x.shape
    w, layer = w
    m = w.shape[2]
    gain = gain.reshape(-1, d)
    gain_of = gain_of or (0,) * (len(normed) // PROJ_HEAD_GROUP)
    rope_spec = pl.BlockSpec((tm, HEAD_DIM), lambda i: (i % rope_period_blocks, 0))
    return pl.pallas_call(
        functools.partial(_proj_heads_kernel, normed=normed, outs=outs, gain_of=gain_of),
        grid=(n // tm,),
        in_specs=[pl.BlockSpec((tm, d), lambda i: (i, 0)),
                  pl.BlockSpec(gain.shape, lambda i: (0, 0)),
                  pl.BlockSpec((None, d, m), lambda i: (layer, 0, 0)),
                  pl.BlockSpec(head_gain.shape, lambda i: (0, 0)),
                  *[rope_spec] * len(rope)],
        out_specs=[pl.BlockSpec((tm, (h1 - h0) * HEAD_DIM), lambda i: (i, 0)) for _, h0, h1 in outs],
        out_shape=[jax.ShapeDtypeStruct((n, (h1 - h0) * HEAD_DIM), dt) for dt, h0, h1 in outs],
        compiler_params=_params("parallel"),
    )(x, gain, w, head_gain, *rope)


LOG2E = 1.4426950408889634
MXU_SCAN_MIN_CHUNK = 128


TILE_LEVELS = int(math.log2(SUBLANES))


def _gla_tables(chunk):
    n_levels = int(math.log2(chunk))
    t, s = np.arange(chunk)[:, None], np.arange(chunk)[None, :]
    mats, masks, uppers = [], [t == s], []
    for lvl in range(1, n_levels + 1):
        half = 1 << (lvl - 1)
        same_half = (t >> (lvl - 1)) == (s >> (lvl - 1))
        upper = (t & half) != 0
        if lvl <= TILE_LEVELS:
            mats.append(np.where(upper, same_half & (s <= t), same_half & (s > t)))
        masks.append((t >> lvl) == (s >> lvl))
        uppers.append(np.broadcast_to(upper, (chunk, HEAD_DIM)))
    mats += [s <= t, s > t]
    scan = np.concatenate(mats)
    if chunk < MXU_SCAN_MIN_CHUNK:
        scan = np.zeros((2 * SUBLANES, HEAD_DIM))
    return scan.astype(np.float32), np.stack(masks).astype(np.float32), np.stack(uppers).astype(np.float32)


def _gla_kernel(zq_ref, zf_ref, zi_ref, zo_ref, lb_ref, og_ref, s0_ref, scan_ref, same_ref, upper_ref,
                a_ref, sfin_ref, st_ref, *, chunk, heads, t_valid):
    c = pl.program_id(2)
    n_levels = int(math.log2(chunk))
    mxu_scan = chunk >= MXU_SCAN_MIN_CHUNK
    row = lax.broadcasted_iota(jnp.int32, (chunk, HEAD_DIM), 0)

    @pl.when(c == 0)
    def _():
        for h in range(heads):
            st_ref[h] = s0_ref[0, 0, h].T

    hs = range(heads)
    cols = [slice(h * HEAD_DIM, (h + 1) * HEAD_DIM) for h in hs]
    qs, ks, vs, log_fs = [], [], [], []
    for h in hs:
        zq, zf = zq_ref[0, :, cols[h]].astype(jnp.float32), zf_ref[0, :, cols[h]]
        lb = lb_ref[:, cols[h]]
        q = zq / (1.0 + jnp.exp(-zq))
        log_f = LOG2E * (jnp.minimum(zf, 0.0) - jnp.log(1.0 + jnp.exp(-jnp.abs(zf)))
                         + jnp.log(1.0 + lb * jnp.exp(jnp.minimum(-zf, EXP_CLAMP))))
        k = (1.0 - lb) / (1.0 + jnp.exp(zf))
        if t_valid < chunk:
            live = row < t_valid
            log_f = jnp.where(live, log_f, 0.0)
            k = jnp.where(live, k, 0.0)
        qs.append(q), ks.append(k), vs.append(zi_ref[0, :, cols[h]]), log_fs.append(log_f)

    scores = [same_ref[0] * lax.dot_general(_bf(qs[h]), _bf(ks[h]), _NT, preferred_element_type=jnp.float32)
              for h in hs]
    q_ins, k_ends, decays = [], [], []
    if mxu_scan:
        scan = scan_ref[...]
        x_alls = []
        for h in hs:
            g2 = log_fs[h]
            hi = _bf(g2)
            rest = g2 - hi.astype(jnp.float32)
            mid = _bf(rest)
            lo = _bf(rest - mid.astype(jnp.float32))
            x_alls.append(jnp.dot(scan, hi, preferred_element_type=jnp.float32)
                          + jnp.dot(scan, mid, preferred_element_type=jnp.float32)
                          + jnp.dot(scan, lo, preferred_element_type=jnp.float32))
        prefixes = [x[TILE_LEVELS * chunk:(TILE_LEVELS + 1) * chunk] for x in x_alls]
        for lvl in range(1, n_levels + 1):
            half = 1 << (lvl - 1)
            for h in hs:
                q, k, pre = qs[h], ks[h], prefixes[h]
                if half >= SUBLANES:
                    zeros = jnp.zeros((half, HEAD_DIM), jnp.float32)
                    q_parts, k_parts = [], []
                    for start in range(0, chunk, 2 * half):
                        low, up = slice(start, start + half), slice(start + half, start + 2 * half)
                        edge = pre[start + half - 1:start + half]
                        k_parts += [k[low] * jnp.exp2(edge - pre[low]), zeros]
                        q_parts += [zeros, q[up] * jnp.exp2(pre[up] - edge)]
                    q_l, k_l = jnp.concatenate(q_parts, axis=0), jnp.concatenate(k_parts, axis=0)
                else:
                    e = jnp.exp2(x_alls[h][(lvl - 1) * chunk:lvl * chunk])
                    e_up = e * upper_ref[lvl - 1]
                    q_l, k_l = q * e_up, k * (e - e_up)
                s_l = lax.dot_general(_bf(q_l), _bf(k_l), _NT, preferred_element_type=jnp.float32)
                scores[h] = scores[h] + same_ref[lvl] * s_l
        for h in hs:
            q_ins.append(qs[h] * jnp.exp2(prefixes[h]))
            k_ends.append(ks[h] * jnp.exp2(x_alls[h][(TILE_LEVELS + 1) * chunk:]))
            decays.append(jnp.exp2(prefixes[h][chunk - 1:chunk]))
    else:
        for h in hs:
            q, k, pre, tot = qs[h], ks[h], log_fs[h], log_fs[h]
            for lvl in range(1, n_levels + 1):
                half = 1 << (lvl - 1)
                upper = (row & half) != 0
                e = jnp.exp2(jnp.where(upper, pre, tot - pre))
                q_l = jnp.where(upper, q * e, 0.0)
                k_l = jnp.where(upper, 0.0, k * e)
                s_l = lax.dot_general(_bf(q_l), _bf(k_l), _NT, preferred_element_type=jnp.float32)
                scores[h] = scores[h] + same_ref[lvl] * s_l
                tot_sib = jnp.where(upper, pltpu.roll(tot, half, 0), pltpu.roll(tot, chunk - half, 0))
                pre = pre + jnp.where(upper, tot_sib, 0.0)
                tot = tot + tot_sib
            q_ins.append(q * jnp.exp2(pre)), k_ends.append(k * jnp.exp2(tot - pre))
            decays.append(jnp.exp2(tot[0:1, :]))

    for h in hs:
        st = st_ref[h]
        o = lax.dot_general(_bf(q_ins[h]), _bf(st), _NT, preferred_element_type=jnp.float32)
        o = o + jnp.dot(_bf(scores[h]), _bf(vs[h]), preferred_element_type=jnp.float32)
        st_ref[h] = st * decays[h] + lax.dot_general(_bf(vs[h]), _bf(k_ends[h]), _TN,
                                                     preferred_element_type=jnp.float32)
        zo = zo_ref[0, :, cols[h]].astype(jnp.float32)
        a = _rms_rows(o, og_ref[:, cols[h]]) / (1.0 + jnp.exp(-zo))
        a_ref[0, :, cols[h]] = a.astype(a_ref.dtype)

    @pl.when(c == pl.num_programs(2) - 1)
    def _():
        for h in range(heads):
            sfin_ref[0, 0, h] = st_ref[h].T


def _gla(zs, lb, out_gain, states, layer, chunk, heads, t_valid):
    b, t, _ = zs[0].shape
    hw = heads * HEAD_DIM
    nh = A_HEADS // heads
    scan, same, upper = _gla_tables(chunk)
    zspec = pl.BlockSpec((1, chunk, hw), lambda bi, hi, ci: (bi, ci, hi))
    vec_spec = pl.BlockSpec((1, hw), lambda bi, hi, ci: (0, hi))
    st_block = (1, 1, heads, HEAD_DIM, HEAD_DIM)
    whole = lambda a: pl.BlockSpec(a.shape, lambda bi, hi, ci: (0,) * a.ndim)
    return pl.pallas_call(
        functools.partial(_gla_kernel, chunk=chunk, heads=heads, t_valid=t_valid),
        grid=(b, nh, t // chunk),
        in_specs=[zspec, zspec, zspec, zspec, vec_spec, vec_spec,
                  pl.BlockSpec(st_block, lambda bi, hi, ci: (layer, bi, hi, 0, 0)),
                  whole(scan), whole(same), whole(upper)],
        out_specs=[pl.BlockSpec((1, chunk, hw), lambda bi, hi, ci: (bi, ci, hi)),
                   pl.BlockSpec(st_block, lambda bi, hi, ci: (0, bi, hi, 0, 0))],
        out_shape=[jax.ShapeDtypeStruct((b, t, D_MODEL), jnp.bfloat16),
                   jax.ShapeDtypeStruct((1,) + states.shape[1:], jnp.float32)],
        scratch_shapes=[pltpu.VMEM((heads, HEAD_DIM, HEAD_DIM), jnp.float32)],
        compiler_params=_params("parallel", "parallel", "arbitrary"),
    )(*zs, lb.reshape(1, D_MODEL), out_gain.reshape(1, D_MODEL), states,
      jnp.asarray(scan, jnp.bfloat16), jnp.asarray(same), jnp.asarray(upper))


def _post_mlp_kernel(x_ref, a_ref, wp_ref, g_ref, wu_ref, wd_ref, out_ref, xn_ref):
    @pl.when(pl.program_id(1) == 0)
    def _():
        x1 = x_ref[...] + jnp.dot(a_ref[...], wp_ref[...], preferred_element_type=jnp.float32)
        out_ref[...] = x1
        xn_ref[...] = _bf(_rms_rows(x1, g_ref[...]))

    hdn = jnp.maximum(jnp.dot(xn_ref[...], wu_ref[...], preferred_element_type=jnp.float32), 0.0)
    out_ref[...] += jnp.dot(_bf(hdn * hdn), wd_ref[...], preferred_element_type=jnp.float32)


def _post_mlp(x, a, wp, gain, wu, wd, tm, tf):
    n, d = x.shape
    (wp, lp), (wu, lu), (wd, ld) = wp, wu, wd
    f = wu.shape[2]
    row_spec = pl.BlockSpec((tm, d), lambda i, j: (i, 0))
    return pl.pallas_call(
        _post_mlp_kernel,
        grid=(n // tm, f // tf),
        in_specs=[row_spec, row_spec,
                  pl.BlockSpec((None, d, d), lambda i, j: (lp, 0, 0)),
                  pl.BlockSpec((1, d), lambda i, j: (0, 0)),
                  pl.BlockSpec((None, d, tf), lambda i, j: (lu, 0, j)),
                  pl.BlockSpec((None, tf, d), lambda i, j: (ld, j, 0))],
        out_specs=row_spec,
        out_shape=jax.ShapeDtypeStruct((n, d), jnp.float32),
        scratch_shapes=[pltpu.VMEM((tm, d), jnp.bfloat16)],
        compiler_params=_params("parallel", "arbitrary"),
    )(x, a, wp, gain.reshape(1, d), wu, wd)


RES = DILATIONS[-1]


def _block_pieces(group):
    pieces = RES // DILATIONS[group]
    return pieces, DIL_KEYS // pieces


def _band_bias(group):
    pieces, per = _block_pieces(group)
    i = np.arange(DIL_KEYS)
    idx = pieces * (i % per) + i // per
    k_true = np.concatenate([idx, DIL_KEYS + idx])
    delta = (DIL_KEYS + idx)[:, None] - k_true[None, :]
    band = (delta >= 0) & (delta <= DIL_KEYS)
    first = band & (np.arange(2 * DIL_KEYS) >= DIL_KEYS)[None, :]
    return np.where(np.stack([first, band]), 0.0, NEG).astype(np.float32)


def _attn_prompt_kernel(bias_ref, q_ref, kp_ref, kc_ref, vp_ref, vc_ref, *refs, pieces, per, qb, n_merge):
    p = DIL_KEYS
    others, out_refs = refs[:2 * n_merge], refs[2 * n_merge:]

    def rows(ref, j, cols, dtype=jnp.bfloat16):
        return jnp.concatenate([ref[0, c, 0, j * per:(j + 1) * per, cols] for c in range(pieces)],
                               axis=0).astype(dtype)

    def put(ref, j, cols, val):
        for c in range(pieces):
            ref[0, c, 0, j * per:(j + 1) * per, cols] = val[c * per:(c + 1) * per].astype(ref.dtype)

    lane = lax.broadcasted_iota(jnp.int32, (p, HEAD_DIM), 1)
    every = slice(None)
    for j in range(qb):
        table = jnp.minimum(pl.program_id(2), 1) if j == 0 else 1
        bias = jnp.concatenate([bias_ref[table]] * Q_PER_KV, axis=0)
        lse_all = jnp.zeros((p, HEAD_DIM), jnp.float32)
        o_heads = {}
        for kvh in range(KV_HEADS):
            kcols = slice(kvh * HEAD_DIM, (kvh + 1) * HEAD_DIM)
            if j == 0:
                k_before, v_before = rows(kp_ref, 0, kcols), rows(vp_ref, 0, kcols)
            else:
                k_before, v_before = rows(kc_ref, j - 1, kcols), rows(vc_ref, j - 1, kcols)
            keys = jnp.concatenate([k_before, rows(kc_ref, j, kcols)], axis=0)
            vals = jnp.concatenate([v_before, rows(vc_ref, j, kcols)], axis=0)
            heads = [kvh * Q_PER_KV + r for r in range(Q_PER_KV)]
            q = jnp.concatenate([rows(q_ref, j, slice(h * HEAD_DIM, (h + 1) * HEAD_DIM)) for h in heads], axis=0)
            s2 = (lax.dot_general(q, keys, _NT, preferred_element_type=jnp.float32) * (HEAD_DIM ** -0.5 * LOG2E)
                  + bias)
            m2 = jnp.max(s2, axis=-1, keepdims=True)
            pr = jnp.exp2(s2 - m2)
            den = jnp.sum(pr, axis=-1, keepdims=True)
            o = jnp.dot(_bf(pr), vals, preferred_element_type=jnp.float32) / den
            lse = m2 * (1.0 / LOG2E) + jnp.log(den)
            for r, h in enumerate(heads):
                o_heads[h] = o[r * p:(r + 1) * p]
                lse_all = jnp.where(lane == h, lse[r * p:(r + 1) * p], lse_all)
        if n_merge == 0:
            o_ref, lse_ref = out_refs
            for h in range(Q_HEADS):
                put(o_ref, j, slice(h * HEAD_DIM, (h + 1) * HEAD_DIM), o_heads[h])
            put(lse_ref, j, every, lse_all)
        else:
            (a_ref,) = out_refs
            lses = [rows(others[2 * g + 1], j, every, jnp.float32) for g in range(n_merge)] + [lse_all]
            top = functools.reduce(jnp.maximum, lses)
            w = [jnp.exp(l - top) for l in lses]
            inv = 1.0 / sum(w)
            w = [wg * inv for wg in w[:-1]]
            for h in range(Q_HEADS):
                cols = slice(h * HEAD_DIM, (h + 1) * HEAD_DIM)
                own = o_heads[h]
                put(a_ref, j, cols, own + sum(w[g][:, h:h + 1] * (rows(others[2 * g], j, cols, jnp.float32) - own)
                                              for g in range(n_merge)))


def _attn_prompt(q, q_col, kv, group, b, s, merge_with=(), qb=ATTN_BLOCKS_PER_STEP):
    dil = DILATIONS[group]
    pieces, per = _block_pieces(group)
    lr = s // RES
    qb = min(qb, lr // per)
    view = lambda a: a.reshape(b, pieces, dil, lr, a.shape[-1])
    kw = KV_HEADS * HEAD_DIM

    def spec(width, col):
        return pl.BlockSpec((1, pieces, 1, qb * per, width), lambda bi, r, n: (bi, 0, r, n, col))

    def before_spec(col):
        return pl.BlockSpec((1, pieces, 1, per, kw), lambda bi, r, n: (bi, 0, r, jnp.maximum(n * qb - 1, 0), col))

    band = jnp.asarray(_band_bias(group))
    other_args = [view(a) for pair in merge_with for a in pair]
    other_specs = [spec(D_MODEL, 0), spec(HEAD_DIM, 0)] * len(merge_with)
    o_shape = jax.ShapeDtypeStruct((b, pieces, dil, lr, D_MODEL), jnp.bfloat16)
    if merge_with:
        out_specs, out_shape = [spec(D_MODEL, 0)], [o_shape]
    else:
        out_specs = [spec(D_MODEL, 0), spec(HEAD_DIM, 0)]
        out_shape = [o_shape, jax.ShapeDtypeStruct((b, pieces, dil, lr, HEAD_DIM), jnp.float32)]
    outs = pl.pallas_call(
        functools.partial(_attn_prompt_kernel, pieces=pieces, per=per, qb=qb, n_merge=len(merge_with)),
        grid=(b, dil, lr // (per * qb)),
        in_specs=[pl.BlockSpec(band.shape, lambda bi, r, n: (0, 0, 0)),
                  spec(D_MODEL, q_col),
                  before_spec(2 * group), spec(kw, 2 * group),
                  before_spec(2 * group + 1), spec(kw, 2 * group + 1),
                  *other_specs],
        out_specs=out_specs,
        out_shape=out_shape,
        compiler_params=_params("parallel", "parallel", "arbitrary"),
    )(band, view(q), view(kv), view(kv), view(kv), view(kv), *other_args)
    return tuple(a.reshape(b * s, a.shape[-1]) for a in outs)


def _sample_key_blocks(n_new):
    n_cache, tables = [], []
    i = np.arange(DIL_KEYS)
    for g, (win, dil) in enumerate(zip(WINDOWS, DILATIONS)):
        n_past = min(win, PAST_LEN)
        if n_past // DIL_KEYS <= SUBLANES:
            blocks = [DIL_KEYS * k + i for k in range(n_past // DIL_KEYS)]
        else:
            assert n_past == RES * DIL_KEYS and n_new <= RES and dil == RES
            blocks = [RES * i + r for r in range(n_new)]
        n_cache.append(len(blocks))
        blocks.append(np.where(i < SUBLANES, n_past + i, -10 ** 9))
        for pos in blocks:
            t = np.arange(SUBLANES)[:, None]
            delta = n_past + t - pos[None, :]
            ok = (delta >= 0) & (delta % dil == 0) & (delta // dil <= DIL_KEYS)
            ok = np.where(t < n_new, ok, True)
            tables.append(ok)
    return n_cache, np.stack(tables).astype(np.float32)


def _attn_sample_kernel(q_ref, kvn_ref, c1_ref, c2_ref, c3_ref, ok_ref, a_ref, *, n_cache):
    rec = 2 * KV_HEADS * HEAD_DIM
    caches = (c1_ref, c2_ref, c3_ref)
    pad = jnp.zeros((DIL_KEYS - SUBLANES, HEAD_DIM), jnp.float32)

    def cache_rows(g, k, is_v, kvh):
        ref = caches[g]
        if len(ref.shape) == 5:
            return ref[0, k * DIL_KEYS:(k + 1) * DIL_KEYS, is_v, kvh, :]
        return ref[0, :, k, is_v, kvh, :]

    for kvh in range(KV_HEADS):
        heads = [kvh * Q_PER_KV + r for r in range(Q_PER_KV)]
        outs, lses = [], []
        blk = 0
        for g in range(N_GROUPS):
            q = jnp.concatenate(
                [q_ref[0, :, (g * Q_HEADS + h) * HEAD_DIM:(g * Q_HEADS + h + 1) * HEAD_DIM] for h in heads], axis=0)
            scores, values, oks = [], [], []
            for k in range(n_cache[g] + 1):
                if k < n_cache[g]:
                    keys, vals = _bf(cache_rows(g, k, 0, kvh)), _bf(cache_rows(g, k, 1, kvh))
                else:
                    base = g * rec + kvh * HEAD_DIM
                    keys = _bf(jnp.concatenate([kvn_ref[0, :, base:base + HEAD_DIM], pad], axis=0))
                    vals = _bf(jnp.concatenate(
                        [kvn_ref[0, :, base + KV_HEADS * HEAD_DIM:base + (KV_HEADS + 1) * HEAD_DIM], pad], axis=0))
                ok = jnp.concatenate([ok_ref[blk]] * Q_PER_KV, axis=0) > 0.5
                blk += 1
                s = lax.dot_general(q, keys, _NT, preferred_element_type=jnp.float32) * HEAD_DIM ** -0.5
                scores.append(jnp.where(ok, s, NEG))
                values.append(vals)
                oks.append(ok)
            m = functools.reduce(jnp.maximum, [jnp.max(s, axis=-1, keepdims=True) for s in scores])
            probs = [jnp.where(ok, jnp.exp(s - m), 0.0) for s, ok in zip(scores, oks)]
            den = sum(jnp.sum(pr, axis=-1, keepdims=True) for pr in probs)
            o = sum(jnp.dot(_bf(pr), vals, preferred_element_type=jnp.float32) for pr, vals in zip(probs, values))
            outs.append(o / den)
            lses.append(m + jnp.log(den))
        top = functools.reduce(jnp.maximum, lses)
        w = [jnp.exp(l - top) for l in lses]
        wsum = sum(w)
        merged = sum((wg / wsum) * og for wg, og in zip(w, outs))
        for r, h in enumerate(heads):
            a_ref[0, :, h * HEAD_DIM:(h + 1) * HEAD_DIM] = merged[r * SUBLANES:(r + 1) * SUBLANES].astype(a_ref.dtype)


def _attn_sample(q, kv_new, caches, n_new):
    b = q.shape[0]
    rec = 2 * KV_HEADS * HEAD_DIM
    n_cache, table = _sample_key_blocks(n_new)
    cache_views, cache_specs = [], []
    rec_shape = (2, KV_HEADS, HEAD_DIM)
    for cache in caches:
        n_past = cache.shape[1]
        if n_past // DIL_KEYS <= SUBLANES:
            cache_views.append(cache)
            cache_specs.append(pl.BlockSpec((1, n_past) + rec_shape, lambda bi: (bi, 0, 0, 0, 0)))
        else:
            cache_views.append(cache.reshape((b, n_past // RES, RES) + rec_shape))
            cache_specs.append(pl.BlockSpec((1, n_past // RES, n_new) + rec_shape, lambda bi: (bi, 0, 0, 0, 0, 0)))
    return pl.pallas_call(
        functools.partial(_attn_sample_kernel, n_cache=n_cache),
        grid=(b,),
        in_specs=[pl.BlockSpec((1, SUBLANES, Q_COLS), lambda bi: (bi, 0, 0)),
                  pl.BlockSpec((1, SUBLANES, KV_COLS), lambda bi: (bi, 0, 0)),
                  *cache_specs,
                  pl.BlockSpec(table.shape, lambda bi: (0, 0, 0))],
        out_specs=pl.BlockSpec((1, SUBLANES, D_MODEL), lambda bi: (bi, 0, 0)),
        out_shape=jax.ShapeDtypeStruct((b, SUBLANES, D_MODEL), jnp.bfloat16),
        compiler_params=_params("parallel"),
    )(q, kv_new, *cache_views, jnp.asarray(table))


def _rope_tables(pos):
    inv = ROPE_THETA ** (-2.0 * jnp.arange(ROT_HALF, dtype=jnp.float32) / ROT_DIM)
    ang = pos.astype(jnp.float32)[:, None] * inv[None, :]
    cos, sin = jnp.cos(ang), jnp.sin(ang)
    rest = HEAD_DIM - ROT_DIM
    one, zero = jnp.ones((pos.shape[0], rest), jnp.float32), jnp.zeros((pos.shape[0], rest), jnp.float32)
    zh = jnp.zeros_like(sin)
    return (jnp.concatenate([cos, cos, one], axis=1),
            jnp.concatenate([zh, sin, zero], axis=1),
            jnp.concatenate([-sin, zh, zero], axis=1))


SPREAD_RUNS = ((0, ROT_HALF), (ROT_DIM, HEAD_DIM // 2 + ROT_HALF), (ROT_HALF, ROT_DIM),
               (HEAD_DIM // 2 + ROT_HALF, HEAD_DIM))
UNSPREAD_RUNS = ((0, ROT_HALF), (HEAD_DIM // 2, HEAD_DIM // 2 + ROT_HALF), (ROT_HALF, HEAD_DIM // 2),
                 (HEAD_DIM // 2 + ROT_HALF, HEAD_DIM))


def _reorder_lanes(x, runs):
    return jnp.concatenate([x[..., a:b] for a, b in runs], axis=-1)


def _to_residue_major(a, b, t):
    return a.reshape(b, t // RES, RES, a.shape[-1]).swapaxes(1, 2).reshape(b * t, a.shape[-1])


def _from_residue_major(a, b, t):
    return a.reshape(b, RES, t // RES, a.shape[-1]).swapaxes(1, 2).reshape(b * t, a.shape[-1])


def _trunk(x, b, t, pos, state0, caches, n_new, weights, lbs, chunk, gla_heads, tm, tm_proj):
    (a_norm, a_w_in, a_out_norm, a_w_out, kv_norm, w_kv, k_norm,
     b_norm, b_w_q, q_norm, b_w_o, mlp_norm, mlp_w_up, mlp_w_down) = weights
    n = b * t
    tf = FF_TILE
    fresh = caches is None
    rope = _rope_tables(pos)
    n_heads_q, n_heads_kv = Q_COLS // HEAD_DIM, KV_COLS // HEAD_DIM
    kv_normed = tuple(bool((hh // KV_HEADS) % 2 == 0) for hh in range(n_heads_kv))
    kv_gain = jnp.repeat(jnp.repeat(k_norm, KV_HEADS, axis=0), 2, axis=0)
    q_gains = [jnp.repeat(q_norm[j], Q_HEADS, axis=0) for j in range(DEPTH - N_A_LAYERS)]
    if fresh:
        cos, sin_fwd, sin_bwd = rope
        rope = tuple(_to_residue_major(_reorder_lanes(r, SPREAD_RUNS), 1, t) for r in (cos, sin_fwd + sin_bwd))
        kv_gain = _reorder_lanes(kv_gain, SPREAD_RUNS)
        q_gains = [_reorder_lanes(g, SPREAD_RUNS) for g in q_gains]
        rope_period = t // tm_proj
    else:
        rope = tuple(jnp.tile(r, (tm_proj // t, 1)) for r in rope)
        rope_period = 1
    finals = []
    kv32 = kv16 = None
    for layer in range(DEPTH):
        if layer < N_A_LAYERS:
            zs = _norm_matmul(x, a_norm[layer], (a_w_in, layer), tm,
                              (jnp.bfloat16, jnp.float32, jnp.bfloat16, jnp.bfloat16))
            a, s_fin = _gla([z.reshape(b, t, D_MODEL) for z in zs], lbs[layer], a_out_norm[layer], state0, layer,
                            chunk, gla_heads, n_new)
            finals.append(s_fin)
            x = _post_mlp(x, a.reshape(n, D_MODEL), (a_w_out, layer), mlp_norm[layer], (mlp_w_up, layer),
                          (mlp_w_down, layer), tm, tf)
            continue
        j = layer - N_A_LAYERS
        q_gain = q_gains[j]
        q_normed = (True,) * n_heads_q
        kv_outs = ((jnp.float32, 0, n_heads_kv), (jnp.bfloat16, 0, n_heads_kv))
        q_outs = ((jnp.float32, 0, Q_HEADS), (jnp.bfloat16, Q_HEADS, n_heads_q))
        if fresh and j == 0:
            x = _to_residue_major(x, b, t)
            shift = lambda outs: tuple((dt, h0 + n_heads_kv, h1 + n_heads_kv) for dt, h0, h1 in outs)
            groups = (n_heads_kv // PROJ_HEAD_GROUP, n_heads_q // PROJ_HEAD_GROUP)
            kv32, kv16, q_near, q_far = _proj_heads(
                x, jnp.stack([kv_norm, b_norm[j]]), (jnp.concatenate([w_kv, b_w_q[j]], axis=1)[None], 0),
                jnp.concatenate([kv_gain, q_gain]), kv_normed + q_normed, rope, rope_period, tm_proj,
                kv_outs + shift(q_outs), gain_of=(0,) * groups[0] + (1,) * groups[1])
        elif fresh:
            q_near, q_far = _proj_heads(x, b_norm[j], (b_w_q, j), q_gain, q_normed, rope, rope_period, tm_proj, q_outs)
        elif j == 0:
            kv32, kv16 = _proj_heads(x, kv_norm, (w_kv[None], 0), kv_gain, kv_normed, rope, rope_period, tm_proj,
                                     kv_outs)
        if fresh:
            near = _attn_prompt(q_near, 0, kv32, 0, b, t)
            mid = _attn_prompt(q_far, 0, kv16, 1, b, t)
            (mixer_in,) = _attn_prompt(q_far, 1, kv16, 2, b, t, merge_with=(near, mid))
        else:
            (q,) = _proj_heads(x, b_norm[j], (b_w_q, j), q_gain, q_normed, rope, rope_period, tm_proj,
                               ((jnp.bfloat16, 0, n_heads_q),))
            a = _attn_sample(q.reshape(b, t, Q_COLS), kv32.reshape(b, t, KV_COLS), caches, n_new)
            mixer_in = a.reshape(n, D_MODEL)
        x = _post_mlp(x, mixer_in, (b_w_o, j), mlp_norm[layer], (mlp_w_up, layer), (mlp_w_down, layer), tm, tf)
    if fresh:
        x = _from_residue_major(x, b, t)
    return x, jnp.concatenate(finals), kv32


def _fresh_window(kv, b, t, group):
    rows, rec = min(WINDOWS[group], t), 2 * KV_HEADS * HEAD_DIM
    a = kv.reshape(b, RES, t // RES, KV_COLS)[:, :, (t - rows) // RES:, group * rec:(group + 1) * rec]
    a = a.swapaxes(1, 2).reshape(b, rows, 2, KV_HEADS, HEAD_DIM)
    return jnp.concatenate([_reorder_lanes(a[:, :, :1], UNSPREAD_RUNS), a[:, :, 1:]], axis=2)


def _cast_kernel(w_ref, o_ref, *spread_refs, spread):
    o_ref[...] = w_ref[...].astype(o_ref.dtype)
    for s_ref in spread_refs:
        for h, flag in enumerate(spread):
            cols = slice(h * HEAD_DIM, (h + 1) * HEAD_DIM)
            head = w_ref[:, cols]
            s_ref[:, cols] = (_reorder_lanes(head, SPREAD_RUNS) if flag else head).astype(s_ref.dtype)


def _to_bf16(w, spread=(), row_block=512):
    rows, cols = math.prod(w.shape[:-1]), w.shape[-1]
    spec = pl.BlockSpec((row_block, cols), lambda i: (i, 0))
    n_out = 2 if spread else 1
    outs = pl.pallas_call(
        functools.partial(_cast_kernel, spread=spread),
        grid=(rows // row_block,), in_specs=[spec], out_specs=[spec] * n_out,
        out_shape=[jax.ShapeDtypeStruct((rows, cols), jnp.bfloat16)] * n_out,
        compiler_params=_params("parallel"),
    )(w.reshape(rows, cols))
    outs = [o.reshape(w.shape) for o in outs]
    return outs if spread else outs[0]


def _prepare_weights(a_norm, a_w_in, a_out_norm, a_w_out, kv_norm, w_kv, k_norm, b_norm, b_w_q, q_norm, b_w_o,
                     mlp_norm, mlp_w_up, mlp_w_down):
    kv_is_k = tuple(bool((hh // KV_HEADS) % 2 == 0) for hh in range(KV_COLS // HEAD_DIM))
    w_kv_bf, w_kv_spread = _to_bf16(w_kv, spread=kv_is_k)
    w_q_bf, w_q_spread = _to_bf16(b_w_q, spread=(True,) * (Q_COLS // HEAD_DIM))
    w_in, w_out, w_o, w_up, w_down = (_to_bf16(w) for w in (a_w_in, a_w_out, b_w_o, mlp_w_up, mlp_w_down))
    pack = lambda kv_w, q_w: (a_norm, w_in, a_out_norm, w_out, kv_norm, kv_w, k_norm,
                              b_norm, q_w, q_norm, w_o, mlp_norm, w_up, w_down)
    return pack(w_kv_bf, w_q_bf), pack(w_kv_spread, w_q_spread)


def kernel(x_prompt, x_sample, state_hgrn, cache_win1_kv, cache_win2_kv, cache_win3_kv, a_norm, a_w_in, a_lb_logits, a_out_norm, a_w_out, kv_norm, w_kv, k_norm, b_norm, b_w_q, q_norm, b_w_o, mlp_norm, mlp_w_up, mlp_w_down):
    bp, tp, d = x_prompt.shape
    bs, ts, _ = x_sample.shape
    sm = jax.nn.softmax(a_lb_logits.astype(jnp.float32), axis=0)
    lbs = jnp.cumsum(sm, axis=0) - sm[0]
    weights, weights_fresh = _prepare_weights(a_norm, a_w_in, a_out_norm, a_w_out, kv_norm, w_kv, k_norm, b_norm,
                                              b_w_q, q_norm, b_w_o, mlp_norm, mlp_w_up, mlp_w_down)

    zero_state = jnp.zeros((N_A_LAYERS, bp, A_HEADS, HEAD_DIM, HEAD_DIM), jnp.float32)
    y_p, st_p, kv_p = _trunk(x_prompt.reshape(bp * tp, d), bp, tp, jnp.arange(tp), zero_state, None, GLA_CHUNK,
                             weights_fresh, lbs, chunk=GLA_CHUNK, gla_heads=A_HEADS, tm=ROW_TILE,
                             tm_proj=PROJ_ROW_TILE)

    xs = jnp.pad(x_sample, ((0, 0), (0, SUBLANES - ts), (0, 0))).reshape(bs * SUBLANES, d)
    caches = (cache_win1_kv, cache_win2_kv, cache_win3_kv)
    y_s, st_s, kv_s = _trunk(xs, bs, SUBLANES, PAST_LEN + jnp.arange(SUBLANES), state_hgrn, caches, ts,
                             weights, lbs, chunk=SUBLANES, gla_heads=A_HEADS, tm=bs * SUBLANES,
                             tm_proj=bs * SUBLANES)

    kv_s = kv_s.reshape(bs, SUBLANES, N_GROUPS, 2, KV_HEADS, HEAD_DIM)[:, :ts]
    win_p = [_fresh_window(kv_p, bp, tp, g) for g in range(N_GROUPS)]
    win_s = [kv_s[:, :, g] for g in range(N_GROUPS)]
    return (y_p.reshape(bp, tp, d), y_s.reshape(bs, SUBLANES, d)[:, :ts], st_p, st_s,
            win_p[0], win_p[1], win_p[2], win_s[0], win_s[1], win_s[2])
```

```python
import functools
import math

import jax
import jax.numpy as jnp
import numpy as np
from jax import lax
from jax.experimental import pallas as pl
from jax.experimental.pallas import tpu as pltpu

D_MODEL = 1024
DEPTH = 4
N_A_LAYERS = DEPTH // 2
HEAD_DIM = 128
A_HEADS = D_MODEL // HEAD_DIM
Q_HEADS = D_MODEL // HEAD_DIM
KV_HEADS = 2
Q_PER_KV = Q_HEADS // KV_HEADS
N_GROUPS = 3
WINDOWS = (128, 512, 2048)
DILATIONS = (1, 4, 16)
DIL_KEYS = 128
ROT_DIM = HEAD_DIM // 4
ROT_HALF = ROT_DIM // 2
ROPE_THETA = 500000.0
D_FF = 4 * D_MODEL
EPS = 1e-6
NEG = -1e30
EXP_CLAMP = 80.0
PAST_LEN = 8192

KV_COLS = N_GROUPS * 2 * KV_HEADS * HEAD_DIM
Q_COLS = N_GROUPS * Q_HEADS * HEAD_DIM
SUBLANES = 8
VMEM_LIMIT = 56 * 1024 * 1024

ROW_TILE = 1024
FF_TILE = 1024
PROJ_ROW_TILE = 256
GLA_CHUNK = 128
ATTN_BLOCKS_PER_STEP = 8

_NT = (((1,), (1,)), ((), ()))
_TN = (((0,), (0,)), ((), ()))


def _params(*sem):
    return pltpu.CompilerParams(dimension_semantics=sem, vmem_limit_bytes=VMEM_LIMIT)


def _bf(x):
    return x.astype(jnp.bfloat16)


def _rms_rows(x, gain):
    return x * lax.rsqrt(jnp.mean(x * x, axis=-1, keepdims=True) + EPS) * gain


def _norm_matmul_kernel(x_ref, g_ref, w_ref, *out_refs):
    xn = _bf(_rms_rows(x_ref[...], g_ref[...]))
    tn = w_ref.shape[1] // len(out_refs)
    for part, o_ref in enumerate(out_refs):
        o_ref[...] = jnp.dot(xn, w_ref[:, part * tn:(part + 1) * tn],
                             preferred_element_type=jnp.float32).astype(o_ref.dtype)


def _norm_matmul(x, gain, w, tm, out_dtypes):
    n, d = x.shape
    w, layer = w
    tn = w.shape[2] // len(out_dtypes)
    return pl.pallas_call(
        _norm_matmul_kernel,
        grid=(n // tm,),
        in_specs=[pl.BlockSpec((tm, d), lambda i: (i, 0)),
                  pl.BlockSpec((1, d), lambda i: (0, 0)),
                  pl.BlockSpec((None,) + w.shape[1:], lambda i: (layer, 0, 0))],
        out_specs=[pl.BlockSpec((tm, tn), lambda i: (i, 0)) for _ in out_dtypes],
        out_shape=[jax.ShapeDtypeStruct((n, tn), dt) for dt in out_dtypes],
        compiler_params=_params("parallel"),
    )(x, gain.reshape(1, d), w)


PROJ_HEAD_GROUP = 4


def _proj_heads_kernel(x_ref, g_ref, w_ref, hg_ref, *refs, normed, outs, gain_of):
    rope, out_refs = [r[...] for r in refs[:len(refs) - len(outs)]], refs[len(refs) - len(outs):]
    xns = [_bf(_rms_rows(x_ref[...], g_ref[i:i + 1, :])) for i in range(g_ref.shape[0])]
    width = PROJ_HEAD_GROUP * HEAD_DIM
    for g0 in range(0, len(normed), PROJ_HEAD_GROUP):
        z = jnp.dot(xns[gain_of[g0 // PROJ_HEAD_GROUP]], w_ref[:, g0 * HEAD_DIM:g0 * HEAD_DIM + width],
                    preferred_element_type=jnp.float32)
        hs = range(g0, g0 + PROJ_HEAD_GROUP)
        zs = {h: z[:, (h - g0) * HEAD_DIM:(h - g0 + 1) * HEAD_DIM] for h in hs}
        normed_hs = [h for h in hs if normed[h]]
        ms = {h: jnp.mean(zs[h] * zs[h], axis=-1, keepdims=True) for h in normed_hs}
        for h in normed_hs:
            zs[h] = zs[h] * lax.rsqrt(ms[h] + EPS) * hg_ref[h:h + 1, :]
        if len(rope) == 3:
            cos, sin_fwd, sin_bwd = rope
            fwd = {h: pltpu.roll(zs[h], ROT_HALF, 1) for h in normed_hs}
            bwd = {h: pltpu.roll(zs[h], HEAD_DIM - ROT_HALF, 1) for h in normed_hs}
            for h in normed_hs:
                zs[h] = zs[h] * cos + fwd[h] * sin_fwd + bwd[h] * sin_bwd
        else:
            cos, sin = rope
            partner = {h: pltpu.roll(zs[h], HEAD_DIM // 2, 1) for h in normed_hs}
            for h in normed_hs:
                zs[h] = zs[h] * cos + partner[h] * sin
        for h in hs:
            for o_ref, (_, h0, h1) in zip(out_refs, outs):
                if h0 <= h < h1:
                    o_ref[:, (h - h0) * HEAD_DIM:(h - h0 + 1) * HEAD_DIM] = zs[h].astype(o_ref.dtype)


def _proj_heads(x, gain, w, head_gain, normed, rope, rope_period_blocks, tm, outs, gain_of=None):
    n, d = x.shape
    w, layer = w
    m = w.shape[2]
    gain = gain.reshape(-1, d)
    gain_of = gain_of or (0,) * (len(normed) // PROJ_HEAD_GROUP)
    rope_spec = pl.BlockSpec((tm, HEAD_DIM), lambda i: (i % rope_period_blocks, 0))
    return pl.pallas_call(
        functools.partial(_proj_heads_kernel, normed=normed, outs=outs, gain_of=gain_of),
        grid=(n // tm,),
        in_specs=[pl.BlockSpec((tm, d), lambda i: (i, 0)),
                  pl.BlockSpec(gain.shape, lambda i: (0, 0)),
                  pl.BlockSpec((None, d, m), lambda i: (layer, 0, 0)),
                  pl.BlockSpec(head_gain.shape, lambda i: (0, 0)),
                  *[rope_spec] * len(rope)],
        out_specs=[pl.BlockSpec((tm, (h1 - h0) * HEAD_DIM), lambda i: (i, 0)) for _, h0, h1 in outs],
        out_shape=[jax.ShapeDtypeStruct((n, (h1 - h0) * HEAD_DIM), dt) for dt, h0, h1 in outs],
        compiler_params=_params("parallel"),
    )(x, gain, w, head_gain, *rope)


LOG2E = 1.4426950408889634
MXU_SCAN_MIN_CHUNK = 128


TILE_LEVELS = int(math.log2(SUBLANES))


def _gla_tables(chunk):
    n_levels = int(math.log2(chunk))
    t, s = np.arange(chunk)[:, None], np.arange(chunk)[None, :]
    mats, masks, uppers = [], [t == s], []
    for lvl in range(1, n_levels + 1):
        half = 1 << (lvl - 1)
        same_half = (t >> (lvl - 1)) == (s >> (lvl - 1))
        upper = (t & half) != 0
        if lvl <= TILE_LEVELS:
            mats.append(np.where(upper, same_half & (s <= t), same_half & (s > t)))
        masks.append((t >> lvl) == (s >> lvl))
        uppers.append(np.broadcast_to(upper, (chunk, HEAD_DIM)))
    mats += [s <= t, s > t]
    scan = np.concatenate(mats)
    if chunk < MXU_SCAN_MIN_CHUNK:
        scan = np.zeros((2 * SUBLANES, HEAD_DIM))
    return scan.astype(np.float32), np.stack(masks).astype(np.float32), np.stack(uppers).astype(np.float32)


def _gla_kernel(zq_ref, zf_ref, zi_ref, zo_ref, lb_ref, og_ref, s0_ref, scan_ref, same_ref, upper_ref,
                a_ref, sfin_ref, st_ref, *, chunk, heads, t_valid):
    c = pl.program_id(2)
    n_levels = int(math.log2(chunk))
    mxu_scan = chunk >= MXU_SCAN_MIN_CHUNK
    row = lax.broadcasted_iota(jnp.int32, (chunk, HEAD_DIM), 0)

    @pl.when(c == 0)
    def _():
        for h in range(heads):
            st_ref[h] = s0_ref[0, 0, h].T

    hs = range(heads)
    cols = [slice(h * HEAD_DIM, (h + 1) * HEAD_DIM) for h in hs]
    qs, ks, vs, log_fs = [], [], [], []
    for h in hs:
        zq, zf = zq_ref[0, :, cols[h]].astype(jnp.float32), zf_ref[0, :, cols[h]]
        lb = lb_ref[:, cols[h]]
        q = zq / (1.0 + jnp.exp(-zq))
        log_f = LOG2E * (jnp.minimum(zf, 0.0) - jnp.log(1.0 + jnp.exp(-jnp.abs(zf)))
                         + jnp.log(1.0 + lb * jnp.exp(jnp.minimum(-zf, EXP_CLAMP))))
        k = (1.0 - lb) / (1.0 + jnp.exp(zf))
        if t_valid < chunk:
            live = row < t_valid
            log_f = jnp.where(live, log_f, 0.0)
            k = jnp.where(live, k, 0.0)
        qs.append(q), ks.append(k), vs.append(zi_ref[0, :, cols[h]]), log_fs.append(log_f)

    scores = [same_ref[0] * lax.dot_general(_bf(qs[h]), _bf(ks[h]), _NT, preferred_element_type=jnp.float32)
              for h in hs]
    q_ins, k_ends, decays = [], [], []
    if mxu_scan:
        scan = scan_ref[...]
        x_alls = []
        for h in hs:
            g2 = log_fs[h]
            hi = _bf(g2)
            rest = g2 - hi.astype(jnp.float32)
            mid = _bf(rest)
            lo = _bf(rest - mid.astype(jnp.float32))
            wide = jnp.dot(scan, jnp.concatenate([hi, mid, lo], axis=1), preferred_element_type=jnp.float32)
            x_alls.append(wide[:, :HEAD_DIM] + wide[:, HEAD_DIM:2 * HEAD_DIM] + wide[:, 2 * HEAD_DIM:])
        prefixes = [x[TILE_LEVELS * chunk:(TILE_LEVELS + 1) * chunk] for x in x_alls]
        for lvl in range(1, n_levels + 1):
            half = 1 << (lvl - 1)
            for h in hs:
                q, k, pre = qs[h], ks[h], prefixes[h]
                if half >= SUBLANES:
                    zeros = jnp.zeros((half, HEAD_DIM), jnp.float32)
                    q_parts, k_parts = [], []
                    for start in range(0, chunk, 2 * half):
                        low, up = slice(start, start + half), slice(start + half, start + 2 * half)
                        edge = pre[start + half - 1:start + half]
                        k_parts += [k[low] * jnp.exp2(edge - pre[low]), zeros]
                        q_parts += [zeros, q[up] * jnp.exp2(pre[up] - edge)]
                    q_l, k_l = jnp.concatenate(q_parts, axis=0), jnp.concatenate(k_parts, axis=0)
                else:
                    e = jnp.exp2(x_alls[h][(lvl - 1) * chunk:lvl * chunk])
                    e_up = e * upper_ref[lvl - 1]
                    q_l, k_l = q * e_up, k * (e - e_up)
                s_l = lax.dot_general(_bf(q_l), _bf(k_l), _NT, preferred_element_type=jnp.float32)
                scores[h] = scores[h] + same_ref[lvl] * s_l
        for h in hs:
            q_ins.append(qs[h] * jnp.exp2(prefixes[h]))
            k_ends.append(ks[h] * jnp.exp2(x_alls[h][(TILE_LEVELS + 1) * chunk:]))
            decays.append(jnp.exp2(prefixes[h][chunk - 1:chunk]))
    else:
        for h in hs:
            q, k, pre, tot = qs[h], ks[h], log_fs[h], log_fs[h]
            for lvl in range(1, n_levels + 1):
                half = 1 << (lvl - 1)
                upper = (row & half) != 0
                e = jnp.exp2(jnp.where(upper, pre, tot - pre))
                q_l = jnp.where(upper, q * e, 0.0)
                k_l = jnp.where(upper, 0.0, k * e)
                s_l = lax.dot_general(_bf(q_l), _bf(k_l), _NT, preferred_element_type=jnp.float32)
                scores[h] = scores[h] + same_ref[lvl] * s_l
                tot_sib = jnp.where(upper, pltpu.roll(tot, half, 0), pltpu.roll(tot, chunk - half, 0))
                pre = pre + jnp.where(upper, tot_sib, 0.0)
                tot = tot + tot_sib
            q_ins.append(q * jnp.exp2(pre)), k_ends.append(k * jnp.exp2(tot - pre))
            decays.append(jnp.exp2(tot[0:1, :]))

    for h in hs:
        st = st_ref[h]
        o = lax.dot_general(_bf(q_ins[h]), _bf(st), _NT, preferred_element_type=jnp.float32)
        o = o + jnp.dot(_bf(scores[h]), _bf(vs[h]), preferred_element_type=jnp.float32)
        st_ref[h] = st * decays[h] + lax.dot_general(_bf(vs[h]), _bf(k_ends[h]), _TN,
                                                     preferred_element_type=jnp.float32)
        zo = zo_ref[0, :, cols[h]].astype(jnp.float32)
        a = _rms_rows(o, og_ref[:, cols[h]]) / (1.0 + jnp.exp(-zo))
        a_ref[0, :, cols[h]] = a.astype(a_ref.dtype)

    @pl.when(c == pl.num_programs(2) - 1)
    def _():
        for h in range(heads):
            sfin_ref[0, 0, h] = st_ref[h].T


def _gla(zs, lb, out_gain, states, layer, chunk, heads, t_valid):
    b, t, _ = zs[0].shape
    hw = heads * HEAD_DIM
    nh = A_HEADS // heads
    scan, same, upper = _gla_tables(chunk)
    zspec = pl.BlockSpec((1, chunk, hw), lambda bi, hi, ci: (bi, ci, hi))
    vec_spec = pl.BlockSpec((1, hw), lambda bi, hi, ci: (0, hi))
    st_block = (1, 1, heads, HEAD_DIM, HEAD_DIM)
    whole = lambda a: pl.BlockSpec(a.shape, lambda bi, hi, ci: (0,) * a.ndim)
    return pl.pallas_call(
        functools.partial(_gla_kernel, chunk=chunk, heads=heads, t_valid=t_valid),
        grid=(b, nh, t // chunk),
        in_specs=[zspec, zspec, zspec, zspec, vec_spec, vec_spec,
                  pl.BlockSpec(st_block, lambda bi, hi, ci: (layer, bi, hi, 0, 0)),
                  whole(scan), whole(same), whole(upper)],
        out_specs=[pl.BlockSpec((1, chunk, hw), lambda bi, hi, ci: (bi, ci, hi)),
                   pl.BlockSpec(st_block, lambda bi, hi, ci: (0, bi, hi, 0, 0))],
        out_shape=[jax.ShapeDtypeStruct((b, t, D_MODEL), jnp.bfloat16),
                   jax.ShapeDtypeStruct((1,) + states.shape[1:], jnp.float32)],
        scratch_shapes=[pltpu.VMEM((heads, HEAD_DIM, HEAD_DIM), jnp.float32)],
        compiler_params=_params("parallel", "parallel", "arbitrary"),
    )(*zs, lb.reshape(1, D_MODEL), out_gain.reshape(1, D_MODEL), states,
      jnp.asarray(scan, jnp.bfloat16), jnp.asarray(same), jnp.asarray(upper))


def _post_mlp_kernel(x_ref, a_ref, wp_ref, g_ref, wu_ref, wd_ref, out_ref, xn_ref):
    @pl.when(pl.program_id(1) == 0)
    def _():
        x1 = x_ref[...] + jnp.dot(a_ref[...], wp_ref[...], preferred_element_type=jnp.float32)
        out_ref[...] = x1
        xn_ref[...] = _bf(_rms_rows(x1, g_ref[...]))

    hdn = jnp.maximum(jnp.dot(xn_ref[...], wu_ref[...], preferred_element_type=jnp.float32), 0.0)
    out_ref[...] += jnp.dot(_bf(hdn * hdn), wd_ref[...], preferred_element_type=jnp.float32)


def _post_mlp(x, a, wp, gain, wu, wd, tm, tf):
    n, d = x.shape
    (wp, lp), (wu, lu), (wd, ld) = wp, wu, wd
    f = wu.shape[2]
    row_spec = pl.BlockSpec((tm, d), lambda i, j: (i, 0))
    return pl.pallas_call(
        _post_mlp_kernel,
        grid=(n // tm, f // tf),
        in_specs=[row_spec, row_spec,
                  pl.BlockSpec((None, d, d), lambda i, j: (lp, 0, 0)),
                  pl.BlockSpec((1, d), lambda i, j: (0, 0)),
                  pl.BlockSpec((None, d, tf), lambda i, j: (lu, 0, j)),
                  pl.BlockSpec((None, tf, d), lambda i, j: (ld, j, 0))],
        out_specs=row_spec,
        out_shape=jax.ShapeDtypeStruct((n, d), jnp.float32),
        scratch_shapes=[pltpu.VMEM((tm, d), jnp.bfloat16)],
        compiler_params=_params("parallel", "arbitrary"),
    )(x, a, wp, gain.reshape(1, d), wu, wd)


RES = DILATIONS[-1]


def _block_pieces(group):
    pieces = RES // DILATIONS[group]
    return pieces, DIL_KEYS // pieces


def _band_bias(group):
    pieces, per = _block_pieces(group)
    i = np.arange(DIL_KEYS)
    idx = pieces * (i % per) + i // per
    k_true = np.concatenate([idx, DIL_KEYS + idx])
    delta = (DIL_KEYS + idx)[:, None] - k_true[None, :]
    band = (delta >= 0) & (delta <= DIL_KEYS)
    first = band & (np.arange(2 * DIL_KEYS) >= DIL_KEYS)[None, :]
    return np.where(np.stack([first, band]), 0.0, NEG).astype(np.float32)


def _attn_prompt_kernel(bias_ref, q_ref, kp_ref, kc_ref, vp_ref, vc_ref, *refs, pieces, per, qb, n_merge):
    p = DIL_KEYS
    others, out_refs = refs[:2 * n_merge], refs[2 * n_merge:]

    def rows(ref, j, cols, dtype=jnp.bfloat16):
        return jnp.concatenate([ref[0, c, 0, j * per:(j + 1) * per, cols] for c in range(pieces)],
                               axis=0).astype(dtype)

    def put(ref, j, cols, val):
        for c in range(pieces):
            ref[0, c, 0, j * per:(j + 1) * per, cols] = val[c * per:(c + 1) * per].astype(ref.dtype)

    lane = lax.broadcasted_iota(jnp.int32, (p, HEAD_DIM), 1)
    every = slice(None)
    for j in range(qb):
        table = jnp.minimum(pl.program_id(2), 1) if j == 0 else 1
        bias = jnp.concatenate([bias_ref[table]] * Q_PER_KV, axis=0)
        lse_all = jnp.zeros((p, HEAD_DIM), jnp.float32)
        o_heads = {}
        for kvh in range(KV_HEADS):
            kcols = slice(kvh * HEAD_DIM, (kvh + 1) * HEAD_DIM)
            if j == 0:
                k_before, v_before = rows(kp_ref, 0, kcols), rows(vp_ref, 0, kcols)
            else:
                k_before, v_before = rows(kc_ref, j - 1, kcols), rows(vc_ref, j - 1, kcols)
            keys = jnp.concatenate([k_before, rows(kc_ref, j, kcols)], axis=0)
            vals = jnp.concatenate([v_before, rows(vc_ref, j, kcols)], axis=0)
            heads = [kvh * Q_PER_KV + r for r in range(Q_PER_KV)]
            q = jnp.concatenate([rows(q_ref, j, slice(h * HEAD_DIM, (h + 1) * HEAD_DIM)) for h in heads], axis=0)
            s2 = (lax.dot_general(q, keys, _NT, preferred_element_type=jnp.float32) * (HEAD_DIM ** -0.5 * LOG2E)
                  + bias)
            m2 = jnp.max(s2, axis=-1, keepdims=True)
            pr = jnp.exp2(s2 - m2)
            den = jnp.sum(pr, axis=-1, keepdims=True)
            o = jnp.dot(_bf(pr), vals, preferred_element_type=jnp.float32) / den
            lse = m2 * (1.0 / LOG2E) + jnp.log(den)
            for r, h in enumerate(heads):
                o_heads[h] = o[r * p:(r + 1) * p]
                lse_all = jnp.where(lane == h, lse[r * p:(r + 1) * p], lse_all)
        if n_merge == 0:
            o_ref, lse_ref = out_refs
            for h in range(Q_HEADS):
                put(o_ref, j, slice(h * HEAD_DIM, (h + 1) * HEAD_DIM), o_heads[h])
            put(lse_ref, j, every, lse_all)
        else:
            (a_ref,) = out_refs
            lses = [rows(others[2 * g + 1], j, every, jnp.float32) for g in range(n_merge)] + [lse_all]
            top = functools.reduce(jnp.maximum, lses)
            w = [jnp.exp(l - top) for l in lses]
            inv = 1.0 / sum(w)
            w = [wg * inv for wg in w[:-1]]
            for h in range(Q_HEADS):
                cols = slice(h * HEAD_DIM, (h + 1) * HEAD_DIM)
                own = o_heads[h]
                put(a_ref, j, cols, own + sum(w[g][:, h:h + 1] * (rows(others[2 * g], j, cols, jnp.float32) - own)
                                              for g in range(n_merge)))


def _attn_prompt(q, q_col, kv, group, b, s, merge_with=(), qb=ATTN_BLOCKS_PER_STEP):
    dil = DILATIONS[group]
    pieces, per = _block_pieces(group)
    lr = s // RES
    qb = min(qb, lr // per)
    view = lambda a: a.reshape(b, pieces, dil, lr, a.shape[-1])
    kw = KV_HEADS * HEAD_DIM

    def spec(width, col):
        return pl.BlockSpec((1, pieces, 1, qb * per, width), lambda bi, r, n: (bi, 0, r, n, col))

    def before_spec(col):
        return pl.BlockSpec((1, pieces, 1, per, kw), lambda bi, r, n: (bi, 0, r, jnp.maximum(n * qb - 1, 0), col))

    band = jnp.asarray(_band_bias(group))
    other_args = [view(a) for pair in merge_with for a in pair]
    other_specs = [spec(D_MODEL, 0), spec(HEAD_DIM, 0)] * len(merge_with)
    o_shape = jax.ShapeDtypeStruct((b, pieces, dil, lr, D_MODEL), jnp.bfloat16)
    if merge_with:
        out_specs, out_shape = [spec(D_MODEL, 0)], [o_shape]
    else:
        out_specs = [spec(D_MODEL, 0), spec(HEAD_DIM, 0)]
        out_shape = [o_shape, jax.ShapeDtypeStruct((b, pieces, dil, lr, HEAD_DIM), jnp.float32)]
    outs = pl.pallas_call(
        functools.partial(_attn_prompt_kernel, pieces=pieces, per=per, qb=qb, n_merge=len(merge_with)),
        grid=(b, dil, lr // (per * qb)),
        in_specs=[pl.BlockSpec(band.shape, lambda bi, r, n: (0, 0, 0)),
                  spec(D_MODEL, q_col),
                  before_spec(2 * group), spec(kw, 2 * group),
                  before_spec(2 * group + 1), spec(kw, 2 * group + 1),
                  *other_specs],
        out_specs=out_specs,
        out_shape=out_shape,
        compiler_params=_params("parallel", "parallel", "arbitrary"),
    )(band, view(q), view(kv), view(kv), view(kv), view(kv), *other_args)
    return tuple(a.reshape(b * s, a.shape[-1]) for a in outs)


def _sample_key_blocks(n_new):
    n_cache, tables = [], []
    i = np.arange(DIL_KEYS)
    for g, (win, dil) in enumerate(zip(WINDOWS, DILATIONS)):
        n_past = min(win, PAST_LEN)
        if n_past // DIL_KEYS <= SUBLANES:
            blocks = [DIL_KEYS * k + i for k in range(n_past // DIL_KEYS)]
        else:
            assert n_past == RES * DIL_KEYS and n_new <= RES and dil == RES
            blocks = [RES * i + r for r in range(n_new)]
        n_cache.append(len(blocks))
        blocks.append(np.where(i < SUBLANES, n_past + i, -10 ** 9))
        for pos in blocks:
            t = np.arange(SUBLANES)[:, None]
            delta = n_past + t - pos[None, :]
            ok = (delta >= 0) & (delta % dil == 0) & (delta // dil <= DIL_KEYS)
            ok = np.where(t < n_new, ok, True)
            tables.append(ok)
    return n_cache, np.stack(tables).astype(np.float32)


def _attn_sample_kernel(q_ref, kvn_ref, c1_ref, c2_ref, c3_ref, ok_ref, a_ref, *, n_cache):
    rec = 2 * KV_HEADS * HEAD_DIM
    caches = (c1_ref, c2_ref, c3_ref)
    pad = jnp.zeros((DIL_KEYS - SUBLANES, HEAD_DIM), jnp.float32)

    def cache_rows(g, k, is_v, kvh):
        ref = caches[g]
        if len(ref.shape) == 5:
            return ref[0, k * DIL_KEYS:(k + 1) * DIL_KEYS, is_v, kvh, :]
        return ref[0, :, k, is_v, kvh, :]

    for kvh in range(KV_HEADS):
        heads = [kvh * Q_PER_KV + r for r in range(Q_PER_KV)]
        outs, lses = [], []
        blk = 0
        for g in range(N_GROUPS):
            q = jnp.concatenate(
                [q_ref[0, :, (g * Q_HEADS + h) * HEAD_DIM:(g * Q_HEADS + h + 1) * HEAD_DIM] for h in heads], axis=0)
            scores, values, oks = [], [], []
            for k in range(n_cache[g] + 1):
                if k < n_cache[g]:
                    keys, vals = _bf(cache_rows(g, k, 0, kvh)), _bf(cache_rows(g, k, 1, kvh))
                else:
                    base = g * rec + kvh * HEAD_DIM
                    keys = _bf(jnp.concatenate([kvn_ref[0, :, base:base + HEAD_DIM], pad], axis=0))
                    vals = _bf(jnp.concatenate(
                        [kvn_ref[0, :, base + KV_HEADS * HEAD_DIM:base + (KV_HEADS + 1) * HEAD_DIM], pad], axis=0))
                ok = jnp.concatenate([ok_ref[blk]] * Q_PER_KV, axis=0) > 0.5
                blk += 1
                s = lax.dot_general(q, keys, _NT, preferred_element_type=jnp.float32) * HEAD_DIM ** -0.5
                scores.append(jnp.where(ok, s, NEG))
                values.append(vals)
                oks.append(ok)
            m = functools.reduce(jnp.maximum, [jnp.max(s, axis=-1, keepdims=True) for s in scores])
            probs = [jnp.where(ok, jnp.exp(s - m), 0.0) for s, ok in zip(scores, oks)]
            den = sum(jnp.sum(pr, axis=-1, keepdims=True) for pr in probs)
            o = sum(jnp.dot(_bf(pr), vals, preferred_element_type=jnp.float32) for pr, vals in zip(probs, values))
            outs.append(o / den)
            lses.append(m + jnp.log(den))
        top = functools.reduce(jnp.maximum, lses)
        w = [jnp.exp(l - top) for l in lses]
        wsum = sum(w)
        merged = sum((wg / wsum) * og for wg, og in zip(w, outs))
        for r, h in enumerate(heads):
            a_ref[0, :, h * HEAD_DIM:(h + 1) * HEAD_DIM] = merged[r * SUBLANES:(r + 1) * SUBLANES].astype(a_ref.dtype)


def _attn_sample(q, kv_new, caches, n_new):
    b = q.shape[0]
    rec = 2 * KV_HEADS * HEAD_DIM
    n_cache, table = _sample_key_blocks(n_new)
    cache_views, cache_specs = [], []
    rec_shape = (2, KV_HEADS, HEAD_DIM)
    for cache in caches:
        n_past = cache.shape[1]
        if n_past // DIL_KEYS <= SUBLANES:
            cache_views.append(cache)
            cache_specs.append(pl.BlockSpec((1, n_past) + rec_shape, lambda bi: (bi, 0, 0, 0, 0)))
        else:
            cache_views.append(cache.reshape((b, n_past // RES, RES) + rec_shape))
            cache_specs.append(pl.BlockSpec((1, n_past // RES, n_new) + rec_shape, lambda bi: (bi, 0, 0, 0, 0, 0)))
    return pl.pallas_call(
        functools.partial(_attn_sample_kernel, n_cache=n_cache),
        grid=(b,),
        in_specs=[pl.BlockSpec((1, SUBLANES, Q_COLS), lambda bi: (bi, 0, 0)),
                  pl.BlockSpec((1, SUBLANES, KV_COLS), lambda bi: (bi, 0, 0)),
                  *cache_specs,
                  pl.BlockSpec(table.shape, lambda bi: (0, 0, 0))],
        out_specs=pl.BlockSpec((1, SUBLANES, D_MODEL), lambda bi: (bi, 0, 0)),
        out_shape=jax.ShapeDtypeStruct((b, SUBLANES, D_MODEL), jnp.bfloat16),
        compiler_params=_params("parallel"),
    )(q, kv_new, *cache_views, jnp.asarray(table))


def _rope_tables(pos):
    inv = ROPE_THETA ** (-2.0 * jnp.arange(ROT_HALF, dtype=jnp.float32) / ROT_DIM)
    ang = pos.astype(jnp.float32)[:, None] * inv[None, :]
    cos, sin = jnp.cos(ang), jnp.sin(ang)
    rest = HEAD_DIM - ROT_DIM
    one, zero = jnp.ones((pos.shape[0], rest), jnp.float32), jnp.zeros((pos.shape[0], rest), jnp.float32)
    zh = jnp.zeros_like(sin)
    return (jnp.concatenate([cos, cos, one], axis=1),
            jnp.concatenate([zh, sin, zero], axis=1),
            jnp.concatenate([-sin, zh, zero], axis=1))


SPREAD_RUNS = ((0, ROT_HALF), (ROT_DIM, HEAD_DIM // 2 + ROT_HALF), (ROT_HALF, ROT_DIM),
               (HEAD_DIM // 2 + ROT_HALF, HEAD_DIM))
UNSPREAD_RUNS = ((0, ROT_HALF), (HEAD_DIM // 2, HEAD_DIM // 2 + ROT_HALF), (ROT_HALF, HEAD_DIM // 2),
                 (HEAD_DIM // 2 + ROT_HALF, HEAD_DIM))


def _reorder_lanes(x, runs):
    return jnp.concatenate([x[..., a:b] for a, b in runs], axis=-1)


def _to_residue_major(a, b, t):
    return a.reshape(b, t // RES, RES, a.shape[-1]).swapaxes(1, 2).reshape(b * t, a.shape[-1])


def _from_residue_major(a, b, t):
    return a.reshape(b, RES, t // RES, a.shape[-1]).swapaxes(1, 2).reshape(b * t, a.shape[-1])


def _trunk(x, b, t, pos, state0, caches, n_new, weights, lbs, chunk, gla_heads, tm, tm_proj):
    (a_norm, a_w_in, a_out_norm, a_w_out, kv_norm, w_kv, k_norm,
     b_norm, b_w_q, q_norm, b_w_o, mlp_norm, mlp_w_up, mlp_w_down) = weights
    n = b * t
    tf = FF_TILE
    fresh = caches is None
    rope = _rope_tables(pos)
    n_heads_q, n_heads_kv = Q_COLS // HEAD_DIM, KV_COLS // HEAD_DIM
    kv_normed = tuple(bool((hh // KV_HEADS) % 2 == 0) for hh in range(n_heads_kv))
    kv_gain = jnp.repeat(jnp.repeat(k_norm, KV_HEADS, axis=0), 2, axis=0)
    q_gains = [jnp.repeat(q_norm[j], Q_HEADS, axis=0) for j in range(DEPTH - N_A_LAYERS)]
    if fresh:
        cos, sin_fwd, sin_bwd = rope
        rope = tuple(_to_residue_major(_reorder_lanes(r, SPREAD_RUNS), 1, t) for r in (cos, sin_fwd + sin_bwd))
        kv_gain = _reorder_lanes(kv_gain, SPREAD_RUNS)
        q_gains = [_reorder_lanes(g, SPREAD_RUNS) for g in q_gains]
        rope_period = t // tm_proj
    else:
        rope = tuple(jnp.tile(r, (tm_proj // t, 1)) for r in rope)
        rope_period = 1
    finals = []
    kv32 = kv16 = None
    for layer in range(DEPTH):
        if layer < N_A_LAYERS:
            zs = _norm_matmul(x, a_norm[layer], (a_w_in, layer), tm,
                              (jnp.bfloat16, jnp.float32, jnp.bfloat16, jnp.bfloat16))
            a, s_fin = _gla([z.reshape(b, t, D_MODEL) for z in zs], lbs[layer], a_out_norm[layer], state0, layer,
                            chunk, gla_heads, n_new)
            finals.append(s_fin)
            x = _post_mlp(x, a.reshape(n, D_MODEL), (a_w_out, layer), mlp_norm[layer], (mlp_w_up, layer),
                          (mlp_w_down, layer), tm, tf)
            continue
        j = layer - N_A_LAYERS
        q_gain = q_gains[j]
        q_normed = (True,) * n_heads_q
        kv_outs = ((jnp.float32, 0, n_heads_kv), (jnp.bfloat16, 0, n_heads_kv))
        q_outs = ((jnp.float32, 0, Q_HEADS), (jnp.bfloat16, Q_HEADS, n_heads_q))
        if fresh and j == 0:
            x = _to_residue_major(x, b, t)
            shift = lambda outs: tuple((dt, h0 + n_heads_kv, h1 + n_heads_kv) for dt, h0, h1 in outs)
            groups = (n_heads_kv // PROJ_HEAD_GROUP, n_heads_q // PROJ_HEAD_GROUP)
            kv32, kv16, q_near, q_far = _proj_heads(
                x, jnp.stack([kv_norm, b_norm[j]]), (jnp.concatenate([w_kv, b_w_q[j]], axis=1)[None], 0),
                jnp.concatenate([kv_gain, q_gain]), kv_normed + q_normed, rope, rope_period, tm_proj,
                kv_outs + shift(q_outs), gain_of=(0,) * groups[0] + (1,) * groups[1])
        elif fresh:
            q_near, q_far = _proj_heads(x, b_norm[j], (b_w_q, j), q_gain, q_normed, rope, rope_period, tm_proj, q_outs)
        elif j == 0:
            kv32, kv16 = _proj_heads(x, kv_norm, (w_kv[None], 0), kv_gain, kv_normed, rope, rope_period, tm_proj,
                                     kv_outs)
        if fresh:
            near = _attn_prompt(q_near, 0, kv32, 0, b, t)
            mid = _attn_prompt(q_far, 0, kv16, 1, b, t)
            (mixer_in,) = _attn_prompt(q_far, 1, kv16, 2, b, t, merge_with=(near, mid))
        else:
            (q,) = _proj_heads(x, b_norm[j], (b_w_q, j), q_gain, q_normed, rope, rope_period, tm_proj,
                               ((jnp.bfloat16, 0, n_heads_q),))
            a = _attn_sample(q.reshape(b, t, Q_COLS), kv32.reshape(b, t, KV_COLS), caches, n_new)
            mixer_in = a.reshape(n, D_MODEL)
        x = _post_mlp(x, mixer_in, (b_w_o, j), mlp_norm[layer], (mlp_w_up, layer), (mlp_w_down, layer), tm, tf)
    if fresh:
        x = _from_residue_major(x, b, t)
    return x, jnp.concatenate(finals), kv32


def _fresh_window(kv, b, t, group):
    rows, rec = min(WINDOWS[group], t), 2 * KV_HEADS * HEAD_DIM
    a = kv.reshape(b, RES, t // RES, KV_COLS)[:, :, (t - rows) // RES:, group * rec:(group + 1) * rec]
    a = a.swapaxes(1, 2).reshape(b, rows, 2, KV_HEADS, HEAD_DIM)
    return jnp.concatenate([_reorder_lanes(a[:, :, :1], UNSPREAD_RUNS), a[:, :, 1:]], axis=2)


def _cast_kernel(w_ref, o_ref, *spread_refs, spread):
    o_ref[...] = w_ref[...].astype(o_ref.dtype)
    for s_ref in spread_refs:
        for h, flag in enumerate(spread):
            cols = slice(h * HEAD_DIM, (h + 1) * HEAD_DIM)
            head = w_ref[:, cols]
            s_ref[:, cols] = (_reorder_lanes(head, SPREAD_RUNS) if flag else head).astype(s_ref.dtype)


def _to_bf16(w, spread=(), row_block=512):
    rows, cols = math.prod(w.shape[:-1]), w.shape[-1]
    spec = pl.BlockSpec((row_block, cols), lambda i: (i, 0))
    n_out = 2 if spread else 1
    outs = pl.pallas_call(
        functools.partial(_cast_kernel, spread=spread),
        grid=(rows // row_block,), in_specs=[spec], out_specs=[spec] * n_out,
        out_shape=[jax.ShapeDtypeStruct((rows, cols), jnp.bfloat16)] * n_out,
        compiler_params=_params("parallel"),
    )(w.reshape(rows, cols))
    outs = [o.reshape(w.shape) for o in outs]
    return outs if spread else outs[0]


def _prepare_weights(a_norm, a_w_in, a_out_norm, a_w_out, kv_norm, w_kv, k_norm, b_norm, b_w_q, q_norm, b_w_o,
                     mlp_norm, mlp_w_up, mlp_w_down):
    kv_is_k = tuple(bool((hh // KV_HEADS) % 2 == 0) for hh in range(KV_COLS // HEAD_DIM))
    w_kv_bf, w_kv_spread = _to_bf16(w_kv, spread=kv_is_k)
    w_q_bf, w_q_spread = _to_bf16(b_w_q, spread=(True,) * (Q_COLS // HEAD_DIM))
    w_in, w_out, w_o, w_up, w_down = (_to_bf16(w) for w in (a_w_in, a_w_out, b_w_o, mlp_w_up, mlp_w_down))
    pack = lambda kv_w, q_w: (a_norm, w_in, a_out_norm, w_out, kv_norm, kv_w, k_norm,
                              b_norm, q_w, q_norm, w_o, mlp_norm, w_up, w_down)
    return pack(w_kv_bf, w_q_bf), pack(w_kv_spread, w_q_spread)


def kernel(x_prompt, x_sample, state_hgrn, cache_win1_kv, cache_win2_kv, cache_win3_kv, a_norm, a_w_in, a_lb_logits, a_out_norm, a_w_out, kv_norm, w_kv, k_norm, b_norm, b_w_q, q_norm, b_w_o, mlp_norm, mlp_w_up, mlp_w_down):
    bp, tp, d = x_prompt.shape
    bs, ts, _ = x_sample.shape
    sm = jax.nn.softmax(a_lb_logits.astype(jnp.float32), axis=0)
    lbs = jnp.cumsum(sm, axis=0) - sm[0]
    weights, weights_fresh = _prepare_weights(a_norm, a_w_in, a_out_norm, a_w_out, kv_norm, w_kv, k_norm, b_norm,
                                              b_w_q, q_norm, b_w_o, mlp_norm, mlp_w_up, mlp_w_down)

    zero_state = jnp.zeros((N_A_LAYERS, bp, A_HEADS, HEAD_DIM, HEAD_DIM), jnp.float32)
    y_p, st_p, kv_p = _trunk(x_prompt.reshape(bp * tp, d), bp, tp, jnp.arange(tp), zero_state, None, GLA_CHUNK,
                             weights_fresh, lbs, chunk=GLA_CHUNK, gla_heads=A_HEADS, tm=ROW_TILE,
                             tm_proj=PROJ_ROW_TILE)

    xs = jnp.pad(x_sample, ((0, 0), (0, SUBLANES - ts), (0, 0))).reshape(bs * SUBLANES, d)
    caches = (cache_win1_kv, cache_win2_kv, cache_win3_kv)
    y_s, st_s, kv_s = _trunk(xs, bs, SUBLANES, PAST_LEN + jnp.arange(SUBLANES), state_hgrn, caches, ts,
                             weights, lbs, chunk=SUBLANES, gla_heads=A_HEADS, tm=bs * SUBLANES,
                             tm_proj=bs * SUBLANES)

    kv_s = kv_s.reshape(bs, SUBLANES, N_GROUPS, 2, KV_HEADS, HEAD_DIM)[:, :ts]
    win_p = [_fresh_window(kv_p, bp, tp, g) for g in range(N_GROUPS)]
    win_s = [kv_s[:, :, g] for g in range(N_GROUPS)]
    return (y_p.reshape(bp, tp, d), y_s.reshape(bs, SUBLANES, d)[:, :ts], st_p, st_s,
            win_p[0], win_p[1], win_p[2], win_s[0], win_s[1], win_s[2])
```
